```python
import jax, jax.numpy as jnp
from jax import lax
import numpy as np

D_MODEL = 2048
BATCH = 1
SEQ = 8192
DEPTH = 4

N_META = 16
N_A_LAYERS = DEPTH // 2
N_B_LAYERS = DEPTH - N_A_LAYERS

MLSTM_HEADS = 8
MLSTM_QK_DIM = D_MODEL // 2 // MLSTM_HEADS
MLSTM_V_DIM = D_MODEL // MLSTM_HEADS
MLSTM_CHUNK = 64
MLSTM_QK_W = MLSTM_HEADS * MLSTM_QK_DIM
MLSTM_V_W = MLSTM_HEADS * MLSTM_V_DIM
MLSTM_IN_COLS = 2 * MLSTM_QK_W + 2 * MLSTM_V_W + 2 * MLSTM_HEADS
MLSTM_SPLITS = (MLSTM_QK_W, 2 * MLSTM_QK_W, 2 * MLSTM_QK_W + MLSTM_V_W,
                2 * MLSTM_QK_W + 2 * MLSTM_V_W, 2 * MLSTM_QK_W + 2 * MLSTM_V_W + MLSTM_HEADS)

SWA_HEADS = 32
SWA_KV_HEADS = 4
SWA_HEAD_DIM = D_MODEL // SWA_HEADS
SWA_GROUP = SWA_HEADS // SWA_KV_HEADS
WINDOW = 128

D_FF = 5504
RMS_EPS = 1e-6

kernel_name = "yoco_mlstm_swa_sink_macaron"


def rmsnorm(x, g):
    xf = x.astype(jnp.float32)
    y = xf * lax.rsqrt(jnp.mean(xf * xf, axis=-1, keepdims=True) + RMS_EPS)
    return (y * g.astype(jnp.float32)).astype(x.dtype)


def swiglu(x, w_in, w_out):
    gate, up = jnp.split(x @ w_in, 2, axis=-1)
    return (jax.nn.silu(gate) * up) @ w_out


def sandwich(x, g_pre, g_post, fn):
    return rmsnorm(fn(rmsnorm(x, g_pre)), g_post)


def mlstm_chunk(carry, inp):
    C, n, m = carry
    q, k, v, li, lf = inp
    T = q.shape[2]
    b = jnp.cumsum(lf, axis=-1)
    causal = jnp.tril(jnp.ones((T, T), dtype=bool))
    d_intra = b[..., :, None] - b[..., None, :] + li[..., None, :]
    d_intra = jnp.where(causal, d_intra, -jnp.inf)
    d_inter = b + m[..., None]
    m_t = jnp.maximum(d_inter, jnp.max(d_intra, axis=-1))
    w_intra = jnp.exp(d_intra - m_t[..., None])
    w_inter = jnp.exp(d_inter - m_t)
    s = jnp.einsum('bhtd,bhsd->bhts', q, k) * w_intra
    num = (w_inter[..., None] * jnp.einsum('bhtd,bhde->bhte', q, C)
           + jnp.einsum('bhts,bhse->bhte', s, v))
    den = w_inter * jnp.einsum('bhtd,bhd->bht', q, n) + jnp.sum(s, axis=-1)
    h = num / jnp.maximum(jnp.abs(den), jnp.exp(-m_t))[..., None]
    b_last = b[..., -1]
    d_state = b_last[..., None] - b + li
    m_new = jnp.maximum(b_last + m, jnp.max(d_state, axis=-1))
    w_s = jnp.exp(d_state - m_new[..., None])
    decay = jnp.exp(b_last + m - m_new)
    C_new = decay[..., None, None] * C + jnp.einsum('bhs,bhsd,bhse->bhde', w_s, k, v)
    n_new = decay[..., None] * n + jnp.einsum('bhs,bhsd->bhd', w_s, k)
    return (C_new, n_new, m_new), h


def mlstm_mixer(x, w_in, gate_bias, head_norm_g, w_out):
    B, L, _ = x.shape
    S = L - N_META
    NC = S // MLSTM_CHUNK
    H = MLSTM_HEADS
    q, k, v, o, ig, fg = jnp.split(x @ w_in, MLSTM_SPLITS, axis=-1)

    def heads(a, d):
        return a.reshape(B, L, H, d).transpose(0, 2, 1, 3).astype(jnp.float32)

    q = heads(q, MLSTM_QK_DIM) * (MLSTM_QK_DIM ** -0.5)
    k = heads(k, MLSTM_QK_DIM)
    v = heads(v, MLSTM_V_DIM)
    gb = gate_bias.astype(jnp.float32)
    li = (ig.astype(jnp.float32) + gb[:H]).transpose(0, 2, 1)
    lf = jax.nn.log_sigmoid(fg.astype(jnp.float32) + gb[H:]).transpose(0, 2, 1)

    carry = (jnp.zeros((B, H, MLSTM_QK_DIM, MLSTM_V_DIM), jnp.float32),
             jnp.zeros((B, H, MLSTM_QK_DIM), jnp.float32),
             jnp.zeros((B, H), jnp.float32))
    carry, h_meta = mlstm_chunk(carry, (q[:, :, :N_META], k[:, :, :N_META], v[:, :, :N_META],
                                        li[:, :, :N_META], lf[:, :, :N_META]))

    def to_chunks(a):
        a = a[:, :, N_META:]
        a = a.reshape(a.shape[:2] + (NC, MLSTM_CHUNK) + a.shape[3:])
        return jnp.moveaxis(a, 2, 0)

    _, h_chunks = lax.scan(mlstm_chunk, carry,
                           (to_chunks(q), to_chunks(k), to_chunks(v), to_chunks(li), to_chunks(lf)))
    h_real = jnp.moveaxis(h_chunks, 0, 2).reshape(B, H, S, MLSTM_V_DIM)
    h = jnp.concatenate([h_meta, h_real], axis=2)
    h = h * lax.rsqrt(jnp.mean(h * h, axis=-1, keepdims=True) + RMS_EPS)
    h = h.transpose(0, 2, 1, 3).reshape(B, L, MLSTM_V_W) * head_norm_g.astype(jnp.float32)
    h = h * jax.nn.sigmoid(o.astype(jnp.float32))
    return h.astype(x.dtype) @ w_out


def shared_kv(x, kv_norm_g, w_kv):
    B, L, _ = x.shape
    k, v = jnp.split(rmsnorm(x, kv_norm_g) @ w_kv, 2, axis=-1)
    k = k.reshape(B, L, SWA_KV_HEADS, SWA_HEAD_DIM)
    v = v.reshape(B, L, SWA_KV_HEADS, SWA_HEAD_DIM)
    return k[:, :N_META], v[:, :N_META], k[:, N_META:], v[:, N_META:]


def swa_mixer(x, k_meta, v_meta, k_real, v_real, w_q, sinks, w_o):
    B, S, _ = x.shape
    NB = S // WINDOW
    q = (x @ w_q).reshape(B, NB, WINDOW, SWA_KV_HEADS, SWA_GROUP, SWA_HEAD_DIM)
    q = q.astype(jnp.float32) * (SWA_HEAD_DIM ** -0.5)

    def band_blocks(a):
        ap = jnp.pad(a, ((0, 0), (WINDOW, 0), (0, 0), (0, 0)))
        ap = ap.reshape(B, NB + 1, WINDOW, SWA_KV_HEADS, SWA_HEAD_DIM)
        return jnp.concatenate([ap[:, :-1], ap[:, 1:]], axis=2)

    kb = band_blocks(k_real)
    vb = band_blocks(v_real)
    s_win = jnp.einsum('bnqkgd,bnrkd->bnkgqr', q, kb)
    qi = jnp.arange(WINDOW)[:, None]
    r = jnp.arange(2 * WINDOW)[None, :]
    band = (r > qi) & (r <= qi + WINDOW)
    not_pad = (jnp.arange(NB)[:, None, None] > 0) | (r >= WINDOW)[None]
    valid = band[None] & not_pad
    s_win = jnp.where(valid[None, :, None, None], s_win, -jnp.inf)
    s_meta = jnp.einsum('bnqkgd,bmkd->bnkgqm', q, k_meta)
    sink = jnp.broadcast_to(sinks.astype(jnp.float32).reshape(1, 1, SWA_KV_HEADS, SWA_GROUP, 1, 1),
                            s_meta.shape[:-1] + (1,))
    p = jax.nn.softmax(jnp.concatenate([s_win, s_meta, sink], axis=-1), axis=-1)
    out = (jnp.einsum('bnkgqr,bnrkd->bnqkgd', p[..., :2 * WINDOW], vb)
           + jnp.einsum('bnkgqm,bmkd->bnqkgd', p[..., 2 * WINDOW:2 * WINDOW + N_META], v_meta))
    return out.reshape(B, S, D_MODEL).astype(x.dtype) @ w_o


def setup_inputs(seed: int = 0) -> dict:
    key = jax.random.key(seed)
    ks = jax.random.split(key, 16)
    f32 = jnp.float32
    D, H = D_MODEL, MLSTM_HEADS

    def nrm(k, shape, fan_in):
        return jax.random.normal(k, shape, f32) * (fan_in ** -0.5)

    def gain(k, shape):
        return 1.0 + 0.05 * jax.random.normal(k, shape, f32)

    f_bias = jnp.linspace(3.0, 6.0, H, dtype=f32)[None] + 0.1 * jax.random.normal(ks[8], (N_A_LAYERS, H), f32)
    i_bias = 0.1 * jax.random.normal(ks[9], (N_A_LAYERS, H), f32)
    return {
        "x": jax.random.normal(ks[0], (BATCH, SEQ, D), f32),
        "meta_tokens": jax.random.normal(ks[1], (N_META, D), f32),
        "norm_pre": gain(ks[2], (DEPTH, 3, D)),
        "norm_post": gain(ks[3], (DEPTH, 3, D)),
        "ffn_w_in": nrm(ks[4], (DEPTH, 2, D, 2 * D_FF), D),
        "ffn_w_out": nrm(ks[5], (DEPTH, 2, D_FF, D), D_FF),
        "mlstm_w_in": nrm(ks[6], (N_A_LAYERS, D, MLSTM_IN_COLS), D),
        "mlstm_gate_bias": jnp.concatenate([i_bias, f_bias], axis=-1),
        "mlstm_norm_g": gain(ks[10], (N_A_LAYERS, MLSTM_V_W)),
        "mlstm_w_out": nrm(ks[7], (N_A_LAYERS, MLSTM_V_W, D), MLSTM_V_W),
        "kv_norm_g": gain(ks[11], (D,)),
        "w_kv": nrm(ks[12], (D, 2 * SWA_KV_HEADS * SWA_HEAD_DIM), D),
        "swa_w_q": nrm(ks[13], (N_B_LAYERS, D, SWA_HEADS * SWA_HEAD_DIM), D),
        "swa_sinks": 0.5 * jax.random.normal(ks[14], (N_B_LAYERS, SWA_HEADS), f32),
        "swa_w_o": nrm(ks[15], (N_B_LAYERS, SWA_HEADS * SWA_HEAD_DIM, D), SWA_HEADS * SWA_HEAD_DIM),
    }


def reference(x, meta_tokens, norm_pre, norm_post, ffn_w_in, ffn_w_out, mlstm_w_in,
              mlstm_gate_bias, mlstm_norm_g, mlstm_w_out, kv_norm_g, w_kv, swa_w_q,
              swa_sinks, swa_w_o):
    B = x.shape[0]
    meta = jnp.broadcast_to(meta_tokens[None].astype(x.dtype), (B, N_META, D_MODEL))
    h = jnp.concatenate([meta, x], axis=1)
    k_meta = v_meta = k_real = v_real = None
    for l in range(DEPTH):
        h = h + 0.5 * sandwich(h, norm_pre[l, 0], norm_post[l, 0],
                               lambda u: swiglu(u, ffn_w_in[l, 0], ffn_w_out[l, 0]))
        if l < N_A_LAYERS:
            h = h + sandwich(h, norm_pre[l, 1], norm_post[l, 1],
                             lambda u: mlstm_mixer(u, mlstm_w_in[l], mlstm_gate_bias[l],
                                                   mlstm_norm_g[l], mlstm_w_out[l]))
        else:
            j = l - N_A_LAYERS
            h = h + sandwich(h, norm_pre[l, 1], norm_post[l, 1],
                             lambda u: swa_mixer(u, k_meta, v_meta, k_real, v_real,
                                                 swa_w_q[j], swa_sinks[j], swa_w_o[j]))
        h = h + 0.5 * sandwich(h, norm_pre[l, 2], norm_post[l, 2],
                               lambda u: swiglu(u, ffn_w_in[l, 1], ffn_w_out[l, 1]))
        if l == N_A_LAYERS - 1:
            k_meta, v_meta, k_real, v_real = shared_kv(h, kv_norm_g, w_kv)
            h = h[:, N_META:]
    return h
```

```python
import functools

import numpy as np
import jax
import jax.numpy as jnp
from jax import lax
from jax.experimental import pallas as pl
from jax.experimental.pallas import tpu as pltpu

F32 = jnp.float32
BF16 = jnp.bfloat16

RMS_EPS = 1e-6
N_META = 16
MLSTM_HEADS = 8
SWA_HEADS = 32
SWA_KV_HEADS = 4
SWA_GROUP = SWA_HEADS // SWA_KV_HEADS
WINDOW = 128

V7X_LANES = 128
BF16_SUBLANES = 16
V7X_VMEM_BYTES = 64 * 1024 * 1024
VMEM_LIMIT = V7X_VMEM_BYTES * 3 // 4

FFN_TF = 512
MLSTM_CHUNK = 256
MASK_NEG = -1e30


def _row_tile(rows, cap):
    best = None
    for t in range(BF16_SUBLANES, cap + 1, BF16_SUBLANES):
        if rows % t == 0:
            best = t
    assert best is not None, (rows, cap)
    return best


def _params(*sem):
    return pltpu.CompilerParams(dimension_semantics=sem, vmem_limit_bytes=VMEM_LIMIT)


def _rms(x, g):
    return x * lax.rsqrt(jnp.mean(x * x, axis=-1, keepdims=True) + RMS_EPS) * g


def _ffn_kernel(h_ref, gpre_ref, gpost_ref, win_ref, wout_ref, o_ref, xn_ref, acc_ref, *, tf, nf):
    j = pl.program_id(1)

    @pl.when(j == 0)
    def _():
        xn_ref[...] = _rms(h_ref[...], gpre_ref[...]).astype(BF16)
        acc_ref[...] = jnp.zeros_like(acc_ref)

    gu = jnp.dot(xn_ref[...], win_ref[...], preferred_element_type=F32)
    g = gu[:, :tf]
    u = gu[:, tf:]
    a = (g * jax.nn.sigmoid(g) * u).astype(BF16)
    acc_ref[...] += jnp.dot(a, wout_ref[...], preferred_element_type=F32)

    @pl.when(j == nf - 1)
    def _():
        o_ref[...] = h_ref[...] + 0.5 * _rms(acc_ref[...], gpost_ref[...])


def _ffn(h, g_pre, g_post, w_in_p, w_out_p, rows, tm):
    d = h.shape[1]
    tf = FFN_TF
    nf = w_out_p.shape[0] // tf
    return pl.pallas_call(
        functools.partial(_ffn_kernel, tf=tf, nf=nf),
        grid=(rows // tm, nf),
        in_specs=[
            pl.BlockSpec((tm, d), lambda i, j: (i, 0)),
            pl.BlockSpec((1, d), lambda i, j: (0, 0)),
            pl.BlockSpec((1, d), lambda i, j: (0, 0)),
            pl.BlockSpec((d, 2 * tf), lambda i, j: (0, j)),
            pl.BlockSpec((tf, d), lambda i, j: (j, 0)),
        ],
        out_specs=pl.BlockSpec((tm, d), lambda i, j: (i, 0)),
        out_shape=jax.ShapeDtypeStruct((rows, d), F32),
        scratch_shapes=[pltpu.VMEM((tm, d), BF16), pltpu.VMEM((tm, d), F32)],
        compiler_params=_params("parallel", "arbitrary"),
        name="ffn",
    )(h, g_pre, g_post, w_in_p, w_out_p)


def _prep_ffn_weights(ffn_w_in, ffn_w_out):
    depth, two, d, dff2 = ffn_w_in.shape
    dff = dff2 // 2
    tf = FFN_TF
    nf = -(-dff // tf)
    pad = nf * tf - dff
    w = ffn_w_in.reshape(depth, two, d, 2, dff).astype(BF16)
    w = jnp.pad(w, ((0, 0), (0, 0), (0, 0), (0, 0), (0, pad)))
    w = w.reshape(depth, two, d, 2, nf, tf).transpose(0, 1, 2, 4, 3, 5).reshape(depth, two, d, nf * 2 * tf)
    wo = jnp.pad(ffn_w_out.astype(BF16), ((0, 0), (0, 0), (0, pad), (0, 0)))
    return w, wo


def _norm_matmul_kernel(h_ref, g_ref, w_ref, o_ref, xn_ref):
    @pl.when(pl.program_id(1) == 0)
    def _():
        xn_ref[...] = _rms(h_ref[...], g_ref[...]).astype(BF16)

    o_ref[...] = jnp.dot(xn_ref[...], w_ref[...], preferred_element_type=F32).astype(o_ref.dtype)


def _norm_matmul(h, g, w, rows, tm, tn):
    d = h.shape[1]
    n = w.shape[1]
    return pl.pallas_call(
        _norm_matmul_kernel,
        grid=(rows // tm, n // tn),
        in_specs=[
            pl.BlockSpec((tm, d), lambda i, j: (i, 0)),
            pl.BlockSpec((1, d), lambda i, j: (0, 0)),
            pl.BlockSpec((d, tn), lambda i, j: (0, j)),
        ],
        out_specs=pl.BlockSpec((tm, tn), lambda i, j: (i, j)),
        out_shape=jax.ShapeDtypeStruct((rows, n), BF16),
        scratch_shapes=[pltpu.VMEM((tm, d), BF16)],
        compiler_params=_params("parallel", "arbitrary"),
        name="norm_matmul",
    )(h, g, w)


def _gates_kernel(h_ref, g_ref, w_ref, b_ref, o_ref, *, heads):
    xn = _rms(h_ref[...], g_ref[...]).astype(BF16)
    z = jnp.dot(xn, w_ref[...], preferred_element_type=F32) + b_ref[...]
    log_sig = jnp.minimum(z, 0.0) - jnp.log1p(jnp.exp(-jnp.abs(z)))
    col = lax.broadcasted_iota(jnp.int32, z.shape, 1)
    o_ref[...] = jnp.where(col >= heads, log_sig, z)


def _gates(h, g, w_gates, bias, rows, tm):
    d = h.shape[1]
    return pl.pallas_call(
        functools.partial(_gates_kernel, heads=MLSTM_HEADS),
        grid=(rows // tm,),
        in_specs=[
            pl.BlockSpec((tm, d), lambda i: (i, 0)),
            pl.BlockSpec((1, d), lambda i: (0, 0)),
            pl.BlockSpec((d, V7X_LANES), lambda i: (0, 0)),
            pl.BlockSpec((1, V7X_LANES), lambda i: (0, 0)),
        ],
        out_specs=pl.BlockSpec((tm, V7X_LANES), lambda i: (i, 0)),
        out_shape=jax.ShapeDtypeStruct((rows, V7X_LANES), F32),
        compiler_params=_params("parallel"),
        name="mlstm_gates",
    )(h, g, w_gates, bias)


def _mlstm_head(hd, tc, qkvo_ref, gc_ref, gr_ref, ng_ref, o_ref, c_ref, n_ref, m_ref, *, heads, dk, dv):
    qk_w = heads * dk
    v_w = heads * dv
    q = qkvo_ref[0:tc, hd * dk:(hd + 1) * dk]
    k = qkvo_ref[0:tc, qk_w + hd * dk:qk_w + (hd + 1) * dk]
    v = qkvo_ref[0:tc, 2 * qk_w + hd * dv:2 * qk_w + (hd + 1) * dv]
    og = qkvo_ref[0:tc, 2 * qk_w + v_w + hd * dv:2 * qk_w + v_w + (hd + 1) * dv]
    li_col = gc_ref[0:tc, hd:hd + 1]
    lf_col = gc_ref[0:tc, heads + hd:heads + hd + 1]
    li_row = gr_ref[hd:hd + 1, 0:tc]
    lf_row = gr_ref[heads + hd:heads + hd + 1, 0:tc]
    scale = dk ** -0.5

    t_idx = lax.broadcasted_iota(jnp.int32, (tc, tc), 0)
    s_idx = lax.broadcasted_iota(jnp.int32, (tc, tc), 1)
    causal = s_idx <= t_idx
    b_col = jnp.sum(jnp.where(causal, lf_row, 0.0), axis=1, keepdims=True)
    b_row = jnp.sum(jnp.where(t_idx <= s_idx, lf_col, 0.0), axis=0, keepdims=True)

    m_prev = m_ref[hd:hd + 1, 0:1]
    d_intra = jnp.where(causal, b_col - b_row + li_row, -jnp.inf)
    d_inter = b_col + m_prev
    m_t = jnp.maximum(d_inter, jnp.max(d_intra, axis=1, keepdims=True))
    w_intra = jnp.exp(d_intra - m_t)
    w_inter = jnp.exp(d_inter - m_t)

    s = lax.dot_general(q, k, (((1,), (1,)), ((), ())), preferred_element_type=F32) * scale * w_intra
    c_prev = c_ref[hd]
    n_prev = n_ref[hd:hd + 1, :]
    q_c = jnp.dot(q, c_prev.astype(BF16), preferred_element_type=F32) * scale
    num = w_inter * q_c + jnp.dot(s.astype(BF16), v, preferred_element_type=F32)
    q_n = jnp.sum(q.astype(F32) * n_prev, axis=1, keepdims=True) * scale
    den = w_inter * q_n + jnp.sum(s, axis=1, keepdims=True)
    hh = num / jnp.maximum(jnp.abs(den), jnp.exp(-m_t))
    hh = hh * lax.rsqrt(jnp.mean(hh * hh, axis=1, keepdims=True) + RMS_EPS)
    hh = hh * ng_ref[:, hd * dv:(hd + 1) * dv] * jax.nn.sigmoid(og.astype(F32))
    o_ref[0:tc, hd * dv:(hd + 1) * dv] = hh.astype(o_ref.dtype)

    b_last = b_col[tc - 1:tc, :]
    d_state = b_last - b_col + li_col
    m_new = jnp.maximum(b_last + m_prev, jnp.max(d_state, axis=0, keepdims=True))
    w_s = jnp.exp(d_state - m_new)
    decay = jnp.exp(b_last + m_prev - m_new)
    vw = (v.astype(F32) * w_s).astype(BF16)
    c_ref[hd] = decay * c_prev + lax.dot_general(k, vw, (((0,), (0,)), ((), ())), preferred_element_type=F32)
    n_ref[hd:hd + 1, :] = decay * n_prev + jnp.sum(k.astype(F32) * w_s, axis=0, keepdims=True)
    m_ref[hd:hd + 1, :] = jnp.broadcast_to(m_new, (1, m_ref.shape[1]))


def _mlstm_kernel(qkvo_ref, gc_ref, gr_ref, ng_ref, o_ref, c_ref, n_ref, m_ref, *, heads, dk, dv, t_meta, t_chunk):
    step = pl.program_id(0)
    head = functools.partial(_mlstm_head, qkvo_ref=qkvo_ref, gc_ref=gc_ref, gr_ref=gr_ref, ng_ref=ng_ref,
                             o_ref=o_ref, c_ref=c_ref, n_ref=n_ref, m_ref=m_ref, heads=heads, dk=dk, dv=dv)

    @pl.when(step == 0)
    def _():
        c_ref[...] = jnp.zeros_like(c_ref)
        n_ref[...] = jnp.zeros_like(n_ref)
        m_ref[...] = jnp.zeros_like(m_ref)
        o_ref[...] = jnp.zeros_like(o_ref)
        for hd in range(heads):
            head(hd, t_meta)

    @pl.when(step > 0)
    def _():
        for hd in range(heads):
            head(hd, t_chunk)


def _mlstm_recurrence(qkvo, gates, gates_t, norm_g, rows_real):
    heads = MLSTM_HEADS
    rows = qkvo.shape[0]
    v_w = norm_g.shape[1]
    dv = v_w // heads
    dk = (qkvo.shape[1] - 2 * v_w) // (2 * heads)
    t = MLSTM_CHUNK
    nc = rows_real // t
    assert rows_real % t == 0 and rows - rows_real == N_META and gates_t.shape[1] == (nc + 1) * t
    blk = lambda c: (c + nc) % (nc + 1)
    return pl.pallas_call(
        functools.partial(_mlstm_kernel, heads=heads, dk=dk, dv=dv, t_meta=N_META, t_chunk=t),
        grid=(nc + 1,),
        in_specs=[
            pl.BlockSpec((t, qkvo.shape[1]), lambda c: (blk(c), 0)),
            pl.BlockSpec((t, V7X_LANES), lambda c: (blk(c), 0)),
            pl.BlockSpec((2 * heads, t), lambda c: (0, blk(c))),
            pl.BlockSpec((1, v_w), lambda c: (0, 0)),
        ],
        out_specs=pl.BlockSpec((t, v_w), lambda c: (blk(c), 0)),
        out_shape=jax.ShapeDtypeStruct((rows, v_w), BF16),
        scratch_shapes=[pltpu.VMEM((heads, dk, dv), F32), pltpu.VMEM((heads, dk), F32),
                        pltpu.VMEM((heads, V7X_LANES), F32)],
        compiler_params=_params("arbitrary"),
        name="mlstm_recurrence",
    )(qkvo, gates, gates_t, norm_g)


def _proj_res_kernel(a_ref, w_ref, g_ref, h_ref, o_ref):
    y = jnp.dot(a_ref[...], w_ref[...], preferred_element_type=F32)
    o_ref[...] = h_ref[...] + _rms(y, g_ref[...])


def _proj_res(a, w, g, h, rows, tm):
    k = a.shape[1]
    d = w.shape[1]
    return pl.pallas_call(
        _proj_res_kernel,
        grid=(rows // tm,),
        in_specs=[
            pl.BlockSpec((tm, k), lambda i: (i, 0)),
            pl.BlockSpec((k, d), lambda i: (0, 0)),
            pl.BlockSpec((1, d), lambda i: (0, 0)),
            pl.BlockSpec((tm, d), lambda i: (i, 0)),
        ],
        out_specs=pl.BlockSpec((tm, d), lambda i: (i, 0)),
        out_shape=jax.ShapeDtypeStruct((rows, d), F32),
        compiler_params=_params("parallel"),
        name="proj_res",
    )(a, w, g, h)


def _swa_kernel(sink_ref, q_ref, kp_ref, kc_ref, km_ref, vp_ref, vc_ref, vm_ref, bias_ref, o_ref, *, kvh, grp, hd, n_keys):
    w = q_ref.shape[0]
    pad = n_keys - 2 * w - km_ref.shape[0]
    zpad = jnp.zeros((pad, hd), BF16)
    bias = bias_ref[0]
    for h in range(kvh):
        base = h * grp * hd
        qh = jnp.concatenate([q_ref[:, base + g * hd:base + (g + 1) * hd] for g in range(grp)], axis=0)
        cols = slice(h * hd, (h + 1) * hd)
        kk = jnp.concatenate([kp_ref[:, cols], kc_ref[:, cols], km_ref[:, cols], zpad], axis=0)
        vv = jnp.concatenate([vp_ref[:, cols], vc_ref[:, cols], vm_ref[:, cols], zpad], axis=0)
        s = lax.dot_general(qh, kk, (((1,), (1,)), ((), ())), preferred_element_type=F32) * (hd ** -0.5) + bias
        sink = jnp.concatenate([jnp.full((w, 1), sink_ref[h * grp + g], F32) for g in range(grp)], axis=0)
        m = jnp.maximum(jnp.max(s, axis=1, keepdims=True), sink)
        p = jnp.exp(s - m)
        denom = jnp.sum(p, axis=1, keepdims=True) + jnp.exp(sink - m)
        o = jnp.dot(p.astype(BF16), vv, preferred_element_type=F32) / denom
        for g in range(grp):
            o_ref[:, base + g * hd:base + (g + 1) * hd] = o[g * w:(g + 1) * w, :].astype(o_ref.dtype)


def _swa_bias(n_keys):
    qi = np.arange(WINDOW)[:, None]
    r = np.arange(n_keys)[None, :]
    band = (r > qi) & (r <= qi + WINDOW) & (r < 2 * WINDOW)
    meta = (r >= 2 * WINDOW) & (r < 2 * WINDOW + N_META)
    later = band | meta
    first = (band & (r >= WINDOW)) | meta
    both = np.stack([first, later]).astype(np.float32)
    both = np.where(both > 0, 0.0, MASK_NEG).astype(np.float32)
    return jnp.asarray(np.tile(both, (1, SWA_GROUP, 1)))


def _swa(q, kv, sinks, rows):
    d = q.shape[1]
    hd = d // SWA_HEADS
    kv_w = SWA_KV_HEADS * hd
    nb = rows // WINDOW
    n_keys = -(-(2 * WINDOW + N_META) // V7X_LANES) * V7X_LANES
    bias = _swa_bias(n_keys)
    meta_blk = rows // N_META
    prev = lambda n: jnp.maximum(n - 1, 0)
    return pl.pallas_call(
        functools.partial(_swa_kernel, kvh=SWA_KV_HEADS, grp=SWA_GROUP, hd=hd, n_keys=n_keys),
        grid=(nb,),
        in_specs=[
            pl.BlockSpec(memory_space=pltpu.SMEM),
            pl.BlockSpec((WINDOW, d), lambda n: (n, 0)),
            pl.BlockSpec((WINDOW, kv_w), lambda n: (prev(n), 0)),
            pl.BlockSpec((WINDOW, kv_w), lambda n: (n, 0)),
            pl.BlockSpec((N_META, kv_w), lambda n: (meta_blk, 0)),
            pl.BlockSpec((WINDOW, kv_w), lambda n: (prev(n), 1)),
            pl.BlockSpec((WINDOW, kv_w), lambda n: (n, 1)),
            pl.BlockSpec((N_META, kv_w), lambda n: (meta_blk, 1)),
            pl.BlockSpec((1,) + bias.shape[1:], lambda n: (jnp.minimum(n, 1), 0, 0)),
        ],
        out_specs=pl.BlockSpec((WINDOW, d), lambda n: (n, 0)),
        out_shape=jax.ShapeDtypeStruct((rows, d), BF16),
        compiler_params=_params("parallel"),
        name="swa_attention",
    )(sinks, q, kv, kv, kv, kv, kv, kv, bias)


def kernel(x, meta_tokens, norm_pre, norm_post, ffn_w_in, ffn_w_out, mlstm_w_in, mlstm_gate_bias, mlstm_norm_g,
           mlstm_w_out, kv_norm_g, w_kv, swa_w_q, swa_sinks, swa_w_o):
    batch, seq, d = x.shape
    assert batch == 1 and meta_tokens.shape[0] == N_META and seq % MLSTM_CHUNK == 0 and seq % WINDOW == 0
    depth = norm_pre.shape[0]
    n_a = mlstm_w_in.shape[0]
    heads = MLSTM_HEADS
    rows_a = seq + N_META
    tm_a = _row_tile(rows_a, 512)
    tm_b = _row_tile(seq, 512)
    tm_proj_a = _row_tile(rows_a, 1024)
    tm_proj_b = _row_tile(seq, 1024)

    w_in_p, w_out_p = _prep_ffn_weights(ffn_w_in, ffn_w_out)
    gain = lambda g: g.reshape(1, -1)

    h = jnp.concatenate([x[0], meta_tokens.astype(x.dtype)], axis=0)
    rows, tm, tm_proj = rows_a, tm_a, tm_proj_a
    kv = None
    for l in range(depth):
        h = _ffn(h, gain(norm_pre[l, 0]), gain(norm_post[l, 0]), w_in_p[l, 0], w_out_p[l, 0], rows, tm)
        if l < n_a:
            n_main = mlstm_w_in.shape[2] - 2 * heads
            w_main = mlstm_w_in[l, :, :n_main].astype(BF16)
            w_gates = jnp.pad(mlstm_w_in[l, :, n_main:], ((0, 0), (0, V7X_LANES - 2 * heads))).astype(BF16)
            bias = jnp.pad(mlstm_gate_bias[l], (0, V7X_LANES - 2 * heads)).reshape(1, V7X_LANES)
            g_pre = gain(norm_pre[l, 1])
            qkvo = _norm_matmul(h, g_pre, w_main, rows, tm_proj, 1024)
            gates = _gates(h, g_pre, w_gates, bias, rows, tm_proj)
            n_chunks = seq // MLSTM_CHUNK + 1
            gates_t = jnp.pad(gates[:, :2 * heads].T, ((0, 0), (0, n_chunks * MLSTM_CHUNK - rows)))
            mixed = _mlstm_recurrence(qkvo, gates, gates_t, gain(mlstm_norm_g[l]), seq)
            h = _proj_res(mixed, mlstm_w_out[l].astype(BF16), gain(norm_post[l, 1]), h, rows, tm)
        else:
            j = l - n_a
            q = _norm_matmul(h, gain(norm_pre[l, 1]), swa_w_q[j].astype(BF16), rows, tm_proj, 1024)
            att = _swa(q, kv, swa_sinks[j], rows)
            h = _proj_res(att, swa_w_o[j].astype(BF16), gain(norm_post[l, 1]), h, rows, tm)
        if l == n_a - 1:
            h_full = _ffn(h, gain(norm_pre[l, 2]), gain(norm_post[l, 2]), w_in_p[l, 1], w_out_p[l, 1], rows, tm)
            kv = _norm_matmul(h_full, gain(kv_norm_g), w_kv.astype(BF16), rows, tm_proj, w_kv.shape[1])
            h = h_full
            rows, tm, tm_proj = seq, tm_b, tm_proj_b
        else:
            h = _ffn(h, gain(norm_pre[l, 2]), gain(norm_post[l, 2]), w_in_p[l, 1], w_out_p[l, 1], rows, tm)
    return h.reshape(batch, seq, d)
```

```python
import functools

import numpy as np
import jax
import jax.numpy as jnp
from jax import lax
from jax.experimental import pallas as pl
from jax.experimental.pallas import tpu as pltpu

F32 = jnp.float32
BF16 = jnp.bfloat16

RMS_EPS = 1e-6
N_META = 16
MLSTM_HEADS = 8
SWA_HEADS = 32
SWA_KV_HEADS = 4
SWA_GROUP = SWA_HEADS // SWA_KV_HEADS
WINDOW = 128

V7X_LANES = 128
BF16_SUBLANES = 16
V7X_VMEM_BYTES = 64 * 1024 * 1024
VMEM_LIMIT = V7X_VMEM_BYTES * 3 // 4

FFN_TF = 256
MLSTM_CHUNK = 256
MASK_NEG = -1e30


def _row_tile(rows, cap):
    best = None
    for t in range(BF16_SUBLANES, cap + 1, BF16_SUBLANES):
        if rows % t == 0:
            best = t
    assert best is not None, (rows, cap)
    return best


def _params(*sem, vmem=VMEM_LIMIT):
    return pltpu.CompilerParams(dimension_semantics=sem, vmem_limit_bytes=vmem)


def _rms(x, g):
    return x * lax.rsqrt(jnp.mean(x * x, axis=-1, keepdims=True) + RMS_EPS) * g


def _ffn_kernel(h_ref, gpre_ref, gpost_ref, wg_ref, wu_ref, wo_ref, o_ref, xn_ref, *, tf, nf, dff):
    j = pl.program_id(1)

    @pl.when(j == 0)
    def _():
        xn_ref[...] = _rms(h_ref[...], gpre_ref[...]).astype(BF16)
        o_ref[...] = jnp.zeros_like(o_ref)

    xn = xn_ref[...]
    g = jnp.dot(xn, wg_ref[...].astype(BF16), preferred_element_type=F32)
    u = jnp.dot(xn, wu_ref[...].astype(BF16), preferred_element_type=F32)
    a = g * jax.nn.sigmoid(g) * u
    first_new = (nf - 1) * tf - (dff - tf)
    col = lax.broadcasted_iota(jnp.int32, a.shape, 1)
    a = jnp.where((j < nf - 1) | (col >= first_new), a, 0.0).astype(BF16)
    o_ref[...] += jnp.dot(a, wo_ref[...].astype(BF16), preferred_element_type=F32)

    @pl.when(j == nf - 1)
    def _():
        o_ref[...] = h_ref[...] + 0.5 * _rms(o_ref[...], gpost_ref[...])


def _ffn(h, g_pre, g_post, w_in, w_out, rows, tm):
    d = h.shape[1]
    dff = w_out.shape[0]
    tf = FFN_TF
    nf = -(-dff // tf)
    assert nf >= 2 and dff % V7X_LANES == 0 and tf % V7X_LANES == 0
    off = lambda j, base=0: (jnp.minimum(j * (tf // V7X_LANES), (dff - tf) // V7X_LANES) + base // V7X_LANES) * V7X_LANES
    vmem = 3 * tm * d * 4 + tm * d * 2 + 2 * 3 * d * tf * 4 + 3 * d * tf * 2 + 4 * tm * tf * 4
    vmem = min(vmem * 5 // 4, V7X_VMEM_BYTES * 7 // 8)
    return pl.pallas_call(
        functools.partial(_ffn_kernel, tf=tf, nf=nf, dff=dff),
        grid=(rows // tm, nf),
        in_specs=[
            pl.BlockSpec((tm, d), lambda i, j: (i, 0), pipeline_mode=pl.Buffered(1)),
            pl.BlockSpec((1, d), lambda i, j: (0, 0)),
            pl.BlockSpec((1, d), lambda i, j: (0, 0)),
            pl.BlockSpec((pl.Element(d), pl.Element(tf)), lambda i, j: (0, off(j))),
            pl.BlockSpec((pl.Element(d), pl.Element(tf)), lambda i, j: (0, off(j, dff))),
            pl.BlockSpec((pl.Element(tf), pl.Element(d)), lambda i, j: (off(j), 0)),
        ],
        out_specs=pl.BlockSpec((tm, d), lambda i, j: (i, 0)),
        out_shape=jax.ShapeDtypeStruct((rows, d), F32),
        scratch_shapes=[pltpu.VMEM((tm, d), BF16)],
        compiler_params=_params("parallel", "arbitrary", vmem=vmem),
        name="ffn",
    )(h, g_pre, g_post, w_in, w_in, w_out)


def _norm_matmul_kernel(h_ref, g_ref, w_ref, o_ref, xn_ref):
    @pl.when(pl.program_id(1) == 0)
    def _():
        xn_ref[...] = _rms(h_ref[...], g_ref[...]).astype(BF16)

    o_ref[...] = jnp.dot(xn_ref[...], w_ref[...], preferred_element_type=F32).astype(o_ref.dtype)


def _norm_matmul(h, g, w, rows, tm, tn):
    d = h.shape[1]
    n = w.shape[1]
    return pl.pallas_call(
        _norm_matmul_kernel,
        grid=(rows // tm, n // tn),
        in_specs=[
            pl.BlockSpec((tm, d), lambda i, j: (i, 0)),
            pl.BlockSpec((1, d), lambda i, j: (0, 0)),
            pl.BlockSpec((d, tn), lambda i, j: (0, j)),
        ],
        out_specs=pl.BlockSpec((tm, tn), lambda i, j: (i, j)),
        out_shape=jax.ShapeDtypeStruct((rows, n), BF16),
        scratch_shapes=[pltpu.VMEM((tm, d), BF16)],
        compiler_params=_params("parallel", "arbitrary"),
        name="norm_matmul",
    )(h, g, w)


def _gates_kernel(h_ref, g_ref, w_ref, b_ref, o_ref, *, heads):
    xn = _rms(h_ref[...], g_ref[...]).astype(BF16)
    z = jnp.dot(xn, w_ref[...], preferred_element_type=F32) + b_ref[...]
    log_sig = jnp.minimum(z, 0.0) - jnp.log1p(jnp.exp(-jnp.abs(z)))
    col = lax.broadcasted_iota(jnp.int32, z.shape, 1)
    o_ref[...] = jnp.where(col >= heads, log_sig, z)


def _gates(h, g, w_gates, bias, rows, tm):
    d = h.shape[1]
    return pl.pallas_call(
        functools.partial(_gates_kernel, heads=MLSTM_HEADS),
        grid=(rows // tm,),
        in_specs=[
            pl.BlockSpec((tm, d), lambda i: (i, 0)),
            pl.BlockSpec((1, d), lambda i: (0, 0)),
            pl.BlockSpec((d, V7X_LANES), lambda i: (0, 0)),
            pl.BlockSpec((1, V7X_LANES), lambda i: (0, 0)),
        ],
        out_specs=pl.BlockSpec((tm, V7X_LANES), lambda i: (i, 0)),
        out_shape=jax.ShapeDtypeStruct((rows, V7X_LANES), F32),
        compiler_params=_params("parallel"),
        name="mlstm_gates",
    )(h, g, w_gates, bias)


def _mlstm_head(hd, tc, qkvo_ref, gc_ref, gr_ref, ng_ref, o_ref, c_ref, n_ref, m_ref, *, heads, dk, dv):
    qk_w = heads * dk
    v_w = heads * dv
    q = qkvo_ref[0:tc, hd * dk:(hd + 1) * dk]
    k = qkvo_ref[0:tc, qk_w + hd * dk:qk_w + (hd + 1) * dk]
    v = qkvo_ref[0:tc, 2 * qk_w + hd * dv:2 * qk_w + (hd + 1) * dv]
    og = qkvo_ref[0:tc, 2 * qk_w + v_w + hd * dv:2 * qk_w + v_w + (hd + 1) * dv]
    li_col = gc_ref[0:tc, hd:hd + 1]
    lf_col = gc_ref[0:tc, heads + hd:heads + hd + 1]
    li_row = gr_ref[hd:hd + 1, 0:tc]
    lf_row = gr_ref[heads + hd:heads + hd + 1, 0:tc]
    scale = dk ** -0.5

    t_idx = lax.broadcasted_iota(jnp.int32, (tc, tc), 0)
    s_idx = lax.broadcasted_iota(jnp.int32, (tc, tc), 1)
    causal = s_idx <= t_idx
    b_col = jnp.sum(jnp.where(causal, lf_row, 0.0), axis=1, keepdims=True)
    b_row = jnp.sum(jnp.where(t_idx <= s_idx, lf_col, 0.0), axis=0, keepdims=True)

    m_prev = m_ref[hd:hd + 1, 0:1]
    d_intra = jnp.where(causal, b_col - b_row + li_row, -jnp.inf)
    d_inter = b_col + m_prev
    m_t = jnp.maximum(d_inter, jnp.max(d_intra, axis=1, keepdims=True))
    w_intra = jnp.exp(d_intra - m_t)
    w_inter = jnp.exp(d_inter - m_t)

    s = lax.dot_general(q, k, (((1,), (1,)), ((), ())), preferred_element_type=F32) * scale * w_intra
    c_prev = c_ref[hd]
    n_prev = n_ref[hd:hd + 1, :]
    q_c = jnp.dot(q, c_prev.astype(BF16), preferred_element_type=F32) * scale
    num = w_inter * q_c + jnp.dot(s.astype(BF16), v, preferred_element_type=F32)
    q_n = jnp.sum(q.astype(F32) * n_prev, axis=1, keepdims=True) * scale
    den = w_inter * q_n + jnp.sum(s, axis=1, keepdims=True)
    hh = num / jnp.maximum(jnp.abs(den), jnp.exp(-m_t))
    hh = hh * lax.rsqrt(jnp.mean(hh * hh, axis=1, keepdims=True) + RMS_EPS)
    hh = hh * ng_ref[:, hd * dv:(hd + 1) * dv] * jax.nn.sigmoid(og.astype(F32))
    o_ref[0:tc, hd * dv:(hd + 1) * dv] = hh.astype(o_ref.dtype)

    b_last = b_col[tc - 1:tc, :]
    d_state = b_last - b_col + li_col
    m_new = jnp.maximum(b_last + m_prev, jnp.max(d_state, axis=0, keepdims=True))
    w_s = jnp.exp(d_state - m_new)
    decay = jnp.exp(b_last + m_prev - m_new)
    vw = (v.astype(F32) * w_s).astype(BF16)
    c_ref[hd] = decay * c_prev + lax.dot_general(k, vw, (((0,), (0,)), ((), ())), preferred_element_type=F32)
    n_ref[hd:hd + 1, :] = decay * n_prev + jnp.sum(k.astype(F32) * w_s, axis=0, keepdims=True)
    m_ref[hd:hd + 1, :] = jnp.broadcast_to(m_new, (1, m_ref.shape[1]))


def _mlstm_kernel(qkvo_ref, gc_ref, gr_ref, ng_ref, o_ref, c_ref, n_ref, m_ref, *, heads, dk, dv, t_meta, t_chunk):
    step = pl.program_id(0)
    head = functools.partial(_mlstm_head, qkvo_ref=qkvo_ref, gc_ref=gc_ref, gr_ref=gr_ref, ng_ref=ng_ref,
                             o_ref=o_ref, c_ref=c_ref, n_ref=n_ref, m_ref=m_ref, heads=heads, dk=dk, dv=dv)

    @pl.when(step == 0)
    def _():
        c_ref[...] = jnp.zeros_like(c_ref)
        n_ref[...] = jnp.zeros_like(n_ref)
        m_ref[...] = jnp.zeros_like(m_ref)
        o_ref[...] = jnp.zeros_like(o_ref)
        for hd in range(heads):
            head(hd, t_meta)

    @pl.when(step > 0)
    def _():
        for hd in range(heads):
            head(hd, t_chunk)


def _mlstm_recurrence(qkvo, gates, gates_t, norm_g, rows_real):
    heads = MLSTM_HEADS
    rows = qkvo.shape[0]
    v_w = norm_g.shape[1]
    dv = v_w // heads
    dk = (qkvo.shape[1] - 2 * v_w) // (2 * heads)
    t = MLSTM_CHUNK
    nc = rows_real // t
    assert rows_real % t == 0 and rows - rows_real == N_META and gates_t.shape[1] == (nc + 1) * t
    blk = lambda c: (c + nc) % (nc + 1)
    return pl.pallas_call(
        functools.partial(_mlstm_kernel, heads=heads, dk=dk, dv=dv, t_meta=N_META, t_chunk=t),
        grid=(nc + 1,),
        in_specs=[
            pl.BlockSpec((t, qkvo.shape[1]), lambda c: (blk(c), 0)),
            pl.BlockSpec((t, V7X_LANES), lambda c: (blk(c), 0)),
            pl.BlockSpec((2 * heads, t), lambda c: (0, blk(c))),
            pl.BlockSpec((1, v_w), lambda c: (0, 0)),
        ],
        out_specs=pl.BlockSpec((t, v_w), lambda c: (blk(c), 0)),
        out_shape=jax.ShapeDtypeStruct((rows, v_w), BF16),
        scratch_shapes=[pltpu.VMEM((heads, dk, dv), F32), pltpu.VMEM((heads, dk), F32),
                        pltpu.VMEM((heads, V7X_LANES), F32)],
        compiler_params=_params("arbitrary"),
        name="mlstm_recurrence",
    )(qkvo, gates, gates_t, norm_g)


def _proj_res_kernel(a_ref, w_ref, g_ref, h_ref, o_ref):
    y = jnp.dot(a_ref[...], w_ref[...], preferred_element_type=F32)
    o_ref[...] = h_ref[...] + _rms(y, g_ref[...])


def _proj_res(a, w, g, h, rows, tm):
    k = a.shape[1]
    d = w.shape[1]
    return pl.pallas_call(
        _proj_res_kernel,
        grid=(rows // tm,),
        in_specs=[
            pl.BlockSpec((tm, k), lambda i: (i, 0)),
            pl.BlockSpec((k, d), lambda i: (0, 0)),
            pl.BlockSpec((1, d), lambda i: (0, 0)),
            pl.BlockSpec((tm, d), lambda i: (i, 0)),
        ],
        out_specs=pl.BlockSpec((tm, d), lambda i: (i, 0)),
        out_shape=jax.ShapeDtypeStruct((rows, d), F32),
        compiler_params=_params("parallel"),
        name="proj_res",
    )(a, w, g, h)


def _swa_kernel(sink_ref, q_ref, kp_ref, kc_ref, km_ref, vp_ref, vc_ref, vm_ref, bias_ref, o_ref, *, kvh, grp, hd, n_keys):
    w = q_ref.shape[0]
    pad = n_keys - 2 * w - km_ref.shape[0]
    zpad = jnp.zeros((pad, hd), BF16)
    bias = bias_ref[0]
    for h in range(kvh):
        base = h * grp * hd
        qh = jnp.concatenate([q_ref[:, base + g * hd:base + (g + 1) * hd] for g in range(grp)], axis=0)
        cols = slice(h * hd, (h + 1) * hd)
        kk = jnp.concatenate([kp_ref[:, cols], kc_ref[:, cols], km_ref[:, cols], zpad], axis=0)
        vv = jnp.concatenate([vp_ref[:, cols], vc_ref[:, cols], vm_ref[:, cols], zpad], axis=0)
        s = lax.dot_general(qh, kk, (((1,), (1,)), ((), ())), preferred_element_type=F32) * (hd ** -0.5) + bias
        sink = jnp.concatenate([jnp.full((w, 1), sink_ref[h * grp + g], F32) for g in range(grp)], axis=0)
        m = jnp.maximum(jnp.max(s, axis=1, keepdims=True), sink)
        p = jnp.exp(s - m)
        denom = jnp.sum(p, axis=1, keepdims=True) + jnp.exp(sink - m)
        o = jnp.dot(p.astype(BF16), vv, preferred_element_type=F32) / denom
        for g in range(grp):
            o_ref[:, base + g * hd:base + (g + 1) * hd] = o[g * w:(g + 1) * w, :].astype(o_ref.dtype)


def _swa_bias(n_keys):
    qi = np.arange(WINDOW)[:, None]
    r = np.arange(n_keys)[None, :]
    band = (r > qi) & (r <= qi + WINDOW) & (r < 2 * WINDOW)
    meta = (r >= 2 * WINDOW) & (r < 2 * WINDOW + N_META)
    later = band | meta
    first = (band & (r >= WINDOW)) | meta
    both = np.stack([first, later]).astype(np.float32)
    both = np.where(both > 0, 0.0, MASK_NEG).astype(np.float32)
    return jnp.asarray(np.tile(both, (1, SWA_GROUP, 1)))


def _swa(q, kv, sinks, rows):
    d = q.shape[1]
    hd = d // SWA_HEADS
    kv_w = SWA_KV_HEADS * hd
    nb = rows // WINDOW
    n_keys = -(-(2 * WINDOW + N_META) // V7X_LANES) * V7X_LANES
    bias = _swa_bias(n_keys)
    meta_blk = rows // N_META
    prev = lambda n: jnp.maximum(n - 1, 0)
    return pl.pallas_call(
        functools.partial(_swa_kernel, kvh=SWA_KV_HEADS, grp=SWA_GROUP, hd=hd, n_keys=n_keys),
        grid=(nb,),
        in_specs=[
            pl.BlockSpec(memory_space=pltpu.SMEM),
            pl.BlockSpec((WINDOW, d), lambda n: (n, 0)),
            pl.BlockSpec((WINDOW, kv_w), lambda n: (prev(n), 0)),
            pl.BlockSpec((WINDOW, kv_w), lambda n: (n, 0)),
            pl.BlockSpec((N_META, kv_w), lambda n: (meta_blk, 0)),
            pl.BlockSpec((WINDOW, kv_w), lambda n: (prev(n), 1)),
            pl.BlockSpec((WINDOW, kv_w), lambda n: (n, 1)),
            pl.BlockSpec((N_META, kv_w), lambda n: (meta_blk, 1)),
            pl.BlockSpec((1,) + bias.shape[1:], lambda n: (jnp.minimum(n, 1), 0, 0)),
        ],
        out_specs=pl.BlockSpec((WINDOW, d), lambda n: (n, 0)),
        out_shape=jax.ShapeDtypeStruct((rows, d), BF16),
        compiler_params=_params("parallel"),
        name="swa_attention",
    )(sinks, q, kv, kv, kv, kv, kv, kv, bias)


def kernel(x, meta_tokens, norm_pre, norm_post, ffn_w_in, ffn_w_out, mlstm_w_in, mlstm_gate_bias, mlstm_norm_g,
           mlstm_w_out, kv_norm_g, w_kv, swa_w_q, swa_sinks, swa_w_o):
    batch, seq, d = x.shape
    assert batch == 1 and meta_tokens.shape[0] == N_META and seq % MLSTM_CHUNK == 0 and seq % WINDOW == 0
    depth = norm_pre.shape[0]
    n_a = mlstm_w_in.shape[0]
    heads = MLSTM_HEADS
    gain = lambda g: g.reshape(1, -1)

    h = jnp.concatenate([x[0], meta_tokens.astype(x.dtype)], axis=0)
    rows = seq + N_META
    tm, tm_proj = _row_tile(rows, 512), _row_tile(rows, 1024)
    kv = None
    for l in range(depth):
        h = _ffn(h, gain(norm_pre[l, 0]), gain(norm_post[l, 0]), ffn_w_in[l, 0], ffn_w_out[l, 0], rows, tm_proj)
        if l < n_a:
            n_main = mlstm_w_in.shape[2] - 2 * heads
            w_main = mlstm_w_in[l, :, :n_main].astype(BF16)
            w_gates = jnp.pad(mlstm_w_in[l, :, n_main:], ((0, 0), (0, V7X_LANES - 2 * heads))).astype(BF16)
            bias = jnp.pad(mlstm_gate_bias[l], (0, V7X_LANES - 2 * heads)).reshape(1, V7X_LANES)
            g_pre = gain(norm_pre[l, 1])
            qkvo = _norm_matmul(h, g_pre, w_main, rows, tm_proj, 1024)
            gates = _gates(h, g_pre, w_gates, bias, rows, tm_proj)
            n_chunks = seq // MLSTM_CHUNK + 1
            gates_t = jnp.pad(gates[:, :2 * heads].T, ((0, 0), (0, n_chunks * MLSTM_CHUNK - rows)))
            mixed = _mlstm_recurrence(qkvo, gates, gates_t, gain(mlstm_norm_g[l]), seq)
            h = _proj_res(mixed, mlstm_w_out[l].astype(BF16), gain(norm_post[l, 1]), h, rows, tm)
        else:
            j = l - n_a
            q = _norm_matmul(h, gain(norm_pre[l, 1]), swa_w_q[j].astype(BF16), rows, tm_proj, 1024)
            att = _swa(q, kv, swa_sinks[j], rows)
            h = _proj_res(att, swa_w_o[j].astype(BF16), gain(norm_post[l, 1]), h, rows, tm)
        if l == n_a - 1:
            h = _ffn(h, gain(norm_pre[l, 2]), gain(norm_post[l, 2]), ffn_w_in[l, 1], ffn_w_out[l, 1], rows, tm_proj)
            kv = _norm_matmul(h, gain(kv_norm_g), w_kv.astype(BF16), rows, tm_proj, w_kv.shape[1])
            rows = seq
            tm, tm_proj = _row_tile(rows, 512), _row_tile(rows, 1024)
        else:
            h = _ffn(h, gain(norm_pre[l, 2]), gain(norm_post[l, 2]), ffn_w_in[l, 1], ffn_w_out[l, 1], rows, tm_proj)
    return h.reshape(batch, seq, d)
```

```python
import functools

import numpy as np
import jax
import jax.numpy as jnp
from jax import lax
from jax.experimental import pallas as pl
from jax.experimental.pallas import tpu as pltpu

F32 = jnp.float32
BF16 = jnp.bfloat16

RMS_EPS = 1e-6
N_META = 16
MLSTM_HEADS = 8
SWA_HEADS = 32
SWA_KV_HEADS = 4
SWA_GROUP = SWA_HEADS // SWA_KV_HEADS
WINDOW = 128

V7X_LANES = 128
BF16_SUBLANES = 16
V7X_VMEM_BYTES = 64 * 1024 * 1024
VMEM_LIMIT = V7X_VMEM_BYTES * 3 // 4

FFN_TF = 512
MLSTM_CHUNK = 256
MASK_NEG = -1e30


def _row_tile(rows, cap):
    best = None
    for t in range(BF16_SUBLANES, cap + 1, BF16_SUBLANES):
        if rows % t == 0:
            best = t
    assert best is not None, (rows, cap)
    return best


def _params(*sem, vmem=VMEM_LIMIT):
    return pltpu.CompilerParams(dimension_semantics=sem, vmem_limit_bytes=vmem)


def _rms(x, g):
    return x * lax.rsqrt(jnp.mean(x * x, axis=-1, keepdims=True) + RMS_EPS) * g


def _ffn_kernel(h_ref, gpre_ref, gpost_ref, wg_ref, wu_ref, wo_ref, o_ref, xn_ref, *, tf, nf, dff):
    j = pl.program_id(1)

    @pl.when(j == 0)
    def _():
        xn_ref[...] = _rms(h_ref[...], gpre_ref[...]).astype(BF16)
        o_ref[...] = jnp.zeros_like(o_ref)

    xn = xn_ref[...]
    g = jnp.dot(xn, wg_ref[...].astype(BF16), preferred_element_type=F32)
    u = jnp.dot(xn, wu_ref[...].astype(BF16), preferred_element_type=F32)
    a = g * jax.nn.sigmoid(g) * u
    first_new = (nf - 1) * tf - (dff - tf)
    col = lax.broadcasted_iota(jnp.int32, a.shape, 1)
    a = jnp.where((j < nf - 1) | (col >= first_new), a, 0.0).astype(BF16)
    o_ref[...] += jnp.dot(a, wo_ref[...].astype(BF16), preferred_element_type=F32)

    @pl.when(j == nf - 1)
    def _():
        o_ref[...] = h_ref[...] + 0.5 * _rms(o_ref[...], gpost_ref[...])


def _ffn(h, g_pre, g_post, w_in, w_out, layer, half, rows, tm):
    d = h.shape[1]
    dff = w_out.shape[2]
    tf = FFN_TF
    nf = -(-dff // tf)
    assert nf >= 2 and dff % V7X_LANES == 0 and tf % V7X_LANES == 0
    off = lambda j, base=0: (jnp.minimum(j * (tf // V7X_LANES), (dff - tf) // V7X_LANES) + base // V7X_LANES) * V7X_LANES
    wbytes = w_in.dtype.itemsize
    vmem = 3 * tm * d * 4 + tm * d * 2 + 2 * 3 * d * tf * wbytes + 4 * tm * tf * 4
    vmem = min(vmem * 5 // 4, V7X_VMEM_BYTES * 7 // 8)
    return pl.pallas_call(
        functools.partial(_ffn_kernel, tf=tf, nf=nf, dff=dff),
        grid=(rows // tm, nf),
        in_specs=[
            pl.BlockSpec((tm, d), lambda i, j: (i, 0), pipeline_mode=pl.Buffered(1)),
            pl.BlockSpec((1, d), lambda i, j: (0, 0)),
            pl.BlockSpec((1, d), lambda i, j: (0, 0)),
            pl.BlockSpec((None, None, pl.Element(d), pl.Element(tf)), lambda i, j: (layer, half, 0, off(j))),
            pl.BlockSpec((None, None, pl.Element(d), pl.Element(tf)), lambda i, j: (layer, half, 0, off(j, dff))),
            pl.BlockSpec((None, None, pl.Element(tf), pl.Element(d)), lambda i, j: (layer, half, off(j), 0)),
        ],
        out_specs=pl.BlockSpec((tm, d), lambda i, j: (i, 0)),
        out_shape=jax.ShapeDtypeStruct((rows, d), F32),
        scratch_shapes=[pltpu.VMEM((tm, d), BF16)],
        compiler_params=_params("parallel", "arbitrary", vmem=vmem),
        name="ffn",
    )(h, g_pre, g_post, w_in, w_in, w_out)


def _norm_matmul_kernel(h_ref, g_ref, w_ref, o_ref, xn_ref):
    @pl.when(pl.program_id(1) == 0)
    def _():
        xn_ref[...] = _rms(h_ref[...], g_ref[...]).astype(BF16)

    o_ref[...] = jnp.dot(xn_ref[...], w_ref[...], preferred_element_type=F32).astype(o_ref.dtype)


def _norm_matmul(h, g, w, rows, tm, tn):
    d = h.shape[1]
    n = w.shape[1]
    return pl.pallas_call(
        _norm_matmul_kernel,
        grid=(rows // tm, n // tn),
        in_specs=[
            pl.BlockSpec((tm, d), lambda i, j: (i, 0)),
            pl.BlockSpec((1, d), lambda i, j: (0, 0)),
            pl.BlockSpec((d, tn), lambda i, j: (0, j)),
        ],
        out_specs=pl.BlockSpec((tm, tn), lambda i, j: (i, j)),
        out_shape=jax.ShapeDtypeStruct((rows, n), BF16),
        scratch_shapes=[pltpu.VMEM((tm, d), BF16)],
        compiler_params=_params("parallel", "arbitrary"),
        name="norm_matmul",
    )(h, g, w)


def _gates_kernel(h_ref, g_ref, w_ref, b_ref, o_ref, *, heads):
    xn = _rms(h_ref[...], g_ref[...]).astype(BF16)
    z = jnp.dot(xn, w_ref[...], preferred_element_type=F32) + b_ref[...]
    log_sig = jnp.minimum(z, 0.0) - jnp.log1p(jnp.exp(-jnp.abs(z)))
    col = lax.broadcasted_iota(jnp.int32, z.shape, 1)
    o_ref[...] = jnp.where(col >= heads, log_sig, z)


def _gates(h, g, w_gates, bias, rows, tm):
    d = h.shape[1]
    return pl.pallas_call(
        functools.partial(_gates_kernel, heads=MLSTM_HEADS),
        grid=(rows // tm,),
        in_specs=[
            pl.BlockSpec((tm, d), lambda i: (i, 0)),
            pl.BlockSpec((1, d), lambda i: (0, 0)),
            pl.BlockSpec((d, V7X_LANES), lambda i: (0, 0)),
            pl.BlockSpec((1, V7X_LANES), lambda i: (0, 0)),
        ],
        out_specs=pl.BlockSpec((tm, V7X_LANES), lambda i: (i, 0)),
        out_shape=jax.ShapeDtypeStruct((rows, V7X_LANES), F32),
        compiler_params=_params("parallel"),
        name="mlstm_gates",
    )(h, g, w_gates, bias)


def _mlstm_head(hd, tc, qkvo_ref, gc_ref, gr_ref, ng_ref, o_ref, c_ref, n_ref, m_ref, *, heads, dk, dv):
    qk_w = heads * dk
    v_w = heads * dv
    q = qkvo_ref[0:tc, hd * dk:(hd + 1) * dk]
    k = qkvo_ref[0:tc, qk_w + hd * dk:qk_w + (hd + 1) * dk]
    v = qkvo_ref[0:tc, 2 * qk_w + hd * dv:2 * qk_w + (hd + 1) * dv]
    og = qkvo_ref[0:tc, 2 * qk_w + v_w + hd * dv:2 * qk_w + v_w + (hd + 1) * dv]
    li_col = gc_ref[0:tc, hd:hd + 1]
    lf_col = gc_ref[0:tc, heads + hd:heads + hd + 1]
    li_row = gr_ref[hd:hd + 1, 0:tc]
    lf_row = gr_ref[heads + hd:heads + hd + 1, 0:tc]
    scale = dk ** -0.5

    t_idx = lax.broadcasted_iota(jnp.int32, (tc, tc), 0)
    s_idx = lax.broadcasted_iota(jnp.int32, (tc, tc), 1)
    causal = s_idx <= t_idx
    b_col = jnp.sum(jnp.where(causal, lf_row, 0.0), axis=1, keepdims=True)
    b_row = jnp.sum(jnp.where(t_idx <= s_idx, lf_col, 0.0), axis=0, keepdims=True)

    m_prev = m_ref[hd:hd + 1, 0:1]
    d_intra = jnp.where(causal, b_col - b_row + li_row, -jnp.inf)
    d_inter = b_col + m_prev
    m_t = jnp.maximum(d_inter, jnp.max(d_intra, axis=1, keepdims=True))
    w_intra = jnp.exp(d_intra - m_t)
    w_inter = jnp.exp(d_inter - m_t)

    s = lax.dot_general(q, k, (((1,), (1,)), ((), ())), preferred_element_type=F32) * scale * w_intra
    c_prev = c_ref[hd]
    n_prev = n_ref[hd:hd + 1, :]
    q_c = jnp.dot(q, c_prev.astype(BF16), preferred_element_type=F32) * scale
    num = w_inter * q_c + jnp.dot(s.astype(BF16), v, preferred_element_type=F32)
    q_n = jnp.sum(q.astype(F32) * n_prev, axis=1, keepdims=True) * scale
    den = w_inter * q_n + jnp.sum(s, axis=1, keepdims=True)
    hh = num / jnp.maximum(jnp.abs(den), jnp.exp(-m_t))
    hh = hh * lax.rsqrt(jnp.mean(hh * hh, axis=1, keepdims=True) + RMS_EPS)
    hh = hh * ng_ref[:, hd * dv:(hd + 1) * dv] * jax.nn.sigmoid(og.astype(F32))
    o_ref[0:tc, hd * dv:(hd + 1) * dv] = hh.astype(o_ref.dtype)

    b_last = b_col[tc - 1:tc, :]
    d_state = b_last - b_col + li_col
    m_new = jnp.maximum(b_last + m_prev, jnp.max(d_state, axis=0, keepdims=True))
    w_s = jnp.exp(d_state - m_new)
    decay = jnp.exp(b_last + m_prev - m_new)
    vw = (v.astype(F32) * w_s).astype(BF16)
    c_ref[hd] = decay * c_prev + lax.dot_general(k, vw, (((0,), (0,)), ((), ())), preferred_element_type=F32)
    n_ref[hd:hd + 1, :] = decay * n_prev + jnp.sum(k.astype(F32) * w_s, axis=0, keepdims=True)
    m_ref[hd:hd + 1, :] = jnp.broadcast_to(m_new, (1, m_ref.shape[1]))


def _mlstm_kernel(qkvo_ref, gc_ref, gr_ref, ng_ref, o_ref, c_ref, n_ref, m_ref, *, heads, dk, dv, t_meta, t_chunk):
    step = pl.program_id(0)
    head = functools.partial(_mlstm_head, qkvo_ref=qkvo_ref, gc_ref=gc_ref, gr_ref=gr_ref, ng_ref=ng_ref,
                             o_ref=o_ref, c_ref=c_ref, n_ref=n_ref, m_ref=m_ref, heads=heads, dk=dk, dv=dv)

    @pl.when(step == 0)
    def _():
        c_ref[...] = jnp.zeros_like(c_ref)
        n_ref[...] = jnp.zeros_like(n_ref)
        m_ref[...] = jnp.zeros_like(m_ref)
        o_ref[...] = jnp.zeros_like(o_ref)
        for hd in range(heads):
            head(hd, t_meta)

    @pl.when(step > 0)
    def _():
        for hd in range(heads):
            head(hd, t_chunk)


def _mlstm_recurrence(qkvo, gates, gates_t, norm_g, rows_real):
    heads = MLSTM_HEADS
    rows = qkvo.shape[0]
    v_w = norm_g.shape[1]
    dv = v_w // heads
    dk = (qkvo.shape[1] - 2 * v_w) // (2 * heads)
    t = MLSTM_CHUNK
    nc = rows_real // t
    assert rows_real % t == 0 and rows - rows_real == N_META and gates_t.shape[1] == (nc + 1) * t
    blk = lambda c: (c + nc) % (nc + 1)
    return pl.pallas_call(
        functools.partial(_mlstm_kernel, heads=heads, dk=dk, dv=dv, t_meta=N_META, t_chunk=t),
        grid=(nc + 1,),
        in_specs=[
            pl.BlockSpec((t, qkvo.shape[1]), lambda c: (blk(c), 0)),
            pl.BlockSpec((t, V7X_LANES), lambda c: (blk(c), 0)),
            pl.BlockSpec((2 * heads, t), lambda c: (0, blk(c))),
            pl.BlockSpec((1, v_w), lambda c: (0, 0)),
        ],
        out_specs=pl.BlockSpec((t, v_w), lambda c: (blk(c), 0)),
        out_shape=jax.ShapeDtypeStruct((rows, v_w), BF16),
        scratch_shapes=[pltpu.VMEM((heads, dk, dv), F32), pltpu.VMEM((heads, dk), F32),
                        pltpu.VMEM((heads, V7X_LANES), F32)],
        compiler_params=_params("arbitrary"),
        name="mlstm_recurrence",
    )(qkvo, gates, gates_t, norm_g)


def _proj_res_kernel(a_ref, w_ref, g_ref, h_ref, o_ref):
    y = jnp.dot(a_ref[...], w_ref[...], preferred_element_type=F32)
    o_ref[...] = h_ref[...] + _rms(y, g_ref[...])


def _proj_res(a, w, g, h, rows, tm):
    k = a.shape[1]
    d = w.shape[1]
    return pl.pallas_call(
        _proj_res_kernel,
        grid=(rows // tm,),
        in_specs=[
            pl.BlockSpec((tm, k), lambda i: (i, 0)),
            pl.BlockSpec((k, d), lambda i: (0, 0)),
            pl.BlockSpec((1, d), lambda i: (0, 0)),
            pl.BlockSpec((tm, d), lambda i: (i, 0)),
        ],
        out_specs=pl.BlockSpec((tm, d), lambda i: (i, 0)),
        out_shape=jax.ShapeDtypeStruct((rows, d), F32),
        compiler_params=_params("parallel"),
        name="proj_res",
    )(a, w, g, h)


def _swa_kernel(sink_ref, q_ref, kp_ref, kc_ref, km_ref, vp_ref, vc_ref, vm_ref, bias_ref, o_ref, *, kvh, grp, hd, n_keys):
    w = q_ref.shape[0]
    pad = n_keys - 2 * w - km_ref.shape[0]
    zpad = jnp.zeros((pad, hd), BF16)
    bias = bias_ref[0]
    for h in range(kvh):
        base = h * grp * hd
        qh = jnp.concatenate([q_ref[:, base + g * hd:base + (g + 1) * hd] for g in range(grp)], axis=0)
        cols = slice(h * hd, (h + 1) * hd)
        kk = jnp.concatenate([kp_ref[:, cols], kc_ref[:, cols], km_ref[:, cols], zpad], axis=0)
        vv = jnp.concatenate([vp_ref[:, cols], vc_ref[:, cols], vm_ref[:, cols], zpad], axis=0)
        s = lax.dot_general(qh, kk, (((1,), (1,)), ((), ())), preferred_element_type=F32) * (hd ** -0.5) + bias
        sink = jnp.concatenate([jnp.full((w, 1), sink_ref[h * grp + g], F32) for g in range(grp)], axis=0)
        m = jnp.maximum(jnp.max(s, axis=1, keepdims=True), sink)
        p = jnp.exp(s - m)
        denom = jnp.sum(p, axis=1, keepdims=True) + jnp.exp(sink - m)
        o = jnp.dot(p.astype(BF16), vv, preferred_element_type=F32) / denom
        for g in range(grp):
            o_ref[:, base + g * hd:base + (g + 1) * hd] = o[g * w:(g + 1) * w, :].astype(o_ref.dtype)


def _swa_bias(n_keys):
    qi = np.arange(WINDOW)[:, None]
    r = np.arange(n_keys)[None, :]
    band = (r > qi) & (r <= qi + WINDOW) & (r < 2 * WINDOW)
    meta = (r >= 2 * WINDOW) & (r < 2 * WINDOW + N_META)
    later = band | meta
    first = (band & (r >= WINDOW)) | meta
    both = np.stack([first, later]).astype(np.float32)
    both = np.where(both > 0, 0.0, MASK_NEG).astype(np.float32)
    return jnp.asarray(np.tile(both, (1, SWA_GROUP, 1)))


def _swa(q, kv, sinks, rows):
    d = q.shape[1]
    hd = d // SWA_HEADS
    kv_w = SWA_KV_HEADS * hd
    nb = rows // WINDOW
    n_keys = -(-(2 * WINDOW + N_META) // V7X_LANES) * V7X_LANES
    bias = _swa_bias(n_keys)
    meta_blk = rows // N_META
    prev = lambda n: jnp.maximum(n - 1, 0)
    return pl.pallas_call(
        functools.partial(_swa_kernel, kvh=SWA_KV_HEADS, grp=SWA_GROUP, hd=hd, n_keys=n_keys),
        grid=(nb,),
        in_specs=[
            pl.BlockSpec(memory_space=pltpu.SMEM),
            pl.BlockSpec((WINDOW, d), lambda n: (n, 0)),
            pl.BlockSpec((WINDOW, kv_w), lambda n: (prev(n), 0)),
            pl.BlockSpec((WINDOW, kv_w), lambda n: (n, 0)),
            pl.BlockSpec((N_META, kv_w), lambda n: (meta_blk, 0)),
            pl.BlockSpec((WINDOW, kv_w), lambda n: (prev(n), 1)),
            pl.BlockSpec((WINDOW, kv_w), lambda n: (n, 1)),
            pl.BlockSpec((N_META, kv_w), lambda n: (meta_blk, 1)),
            pl.BlockSpec((1,) + bias.shape[1:], lambda n: (jnp.minimum(n, 1), 0, 0)),
        ],
        out_specs=pl.BlockSpec((WINDOW, d), lambda n: (n, 0)),
        out_shape=jax.ShapeDtypeStruct((rows, d), BF16),
        compiler_params=_params("parallel"),
        name="swa_attention",
    )(sinks, q, kv, kv, kv, kv, kv, kv, bias)


def kernel(x, meta_tokens, norm_pre, norm_post, ffn_w_in, ffn_w_out, mlstm_w_in, mlstm_gate_bias, mlstm_norm_g,
           mlstm_w_out, kv_norm_g, w_kv, swa_w_q, swa_sinks, swa_w_o):
    batch, seq, d = x.shape
    assert batch == 1 and meta_tokens.shape[0] == N_META and seq % MLSTM_CHUNK == 0 and seq % WINDOW == 0
    depth = norm_pre.shape[0]
    n_a = mlstm_w_in.shape[0]
    heads = MLSTM_HEADS
    gain = lambda g: g.reshape(1, -1)
    w_ffn_in = ffn_w_in.astype(BF16)
    w_ffn_out = ffn_w_out.astype(BF16)
    ffn = lambda hh, l, k, rows, tm: _ffn(hh, gain(norm_pre[l, 2 * k]), gain(norm_post[l, 2 * k]),
                                         w_ffn_in, w_ffn_out, l, k, rows, tm)

    h = jnp.concatenate([x[0], meta_tokens.astype(x.dtype)], axis=0)
    rows = seq + N_META
    tm, tm_proj = _row_tile(rows, 512), _row_tile(rows, 1024)
    kv = None
    for l in range(depth):
        h = ffn(h, l, 0, rows, tm_proj)
        if l < n_a:
            n_main = mlstm_w_in.shape[2] - 2 * heads
            w_main = mlstm_w_in[l, :, :n_main].astype(BF16)
            w_gates = jnp.pad(mlstm_w_in[l, :, n_main:], ((0, 0), (0, V7X_LANES - 2 * heads))).astype(BF16)
            bias = jnp.pad(mlstm_gate_bias[l], (0, V7X_LANES - 2 * heads)).reshape(1, V7X_LANES)
            g_pre = gain(norm_pre[l, 1])
            qkvo = _norm_matmul(h, g_pre, w_main, rows, tm_proj, 1024)
            gates = _gates(h, g_pre, w_gates, bias, rows, tm_proj)
            n_chunks = seq // MLSTM_CHUNK + 1
            gates_t = jnp.pad(gates[:, :2 * heads].T, ((0, 0), (0, n_chunks * MLSTM_CHUNK - rows)))
            mixed = _mlstm_recurrence(qkvo, gates, gates_t, gain(mlstm_norm_g[l]), seq)
            h = _proj_res(mixed, mlstm_w_out[l].astype(BF16), gain(norm_post[l, 1]), h, rows, tm)
        else:
            j = l - n_a
            q = _norm_matmul(h, gain(norm_pre[l, 1]), swa_w_q[j].astype(BF16), rows, tm_proj, 1024)
            att = _swa(q, kv, swa_sinks[j], rows)
            h = _proj_res(att, swa_w_o[j].astype(BF16), gain(norm_post[l, 1]), h, rows, tm)
        if l == n_a - 1:
            h = ffn(h, l, 1, rows, tm_proj)
            kv =_norm_matmul(h, gain(kv_norm_g), w_kv.astype(BF16), rows, tm_proj, w_kv.shape[1])
            rows = seq
            tm, tm_proj = _row_tile(rows, 512), _row_tile(rows, 1024)
        else:
            h = ffn(h, l, 1, rows, tm_proj)
    return h.reshape(batch, seq, d)
```

```python
import functools

import numpy as np
import jax
import jax.numpy as jnp
from jax import lax
from jax.experimental import pallas as pl
from jax.experimental.pallas import tpu as pltpu

F32 = jnp.float32
BF16 = jnp.bfloat16

RMS_EPS = 1e-6
N_META = 16
MLSTM_HEADS = 8
SWA_HEADS = 32
SWA_KV_HEADS = 4
SWA_GROUP = SWA_HEADS // SWA_KV_HEADS
WINDOW = 128

V7X_LANES = 128
BF16_SUBLANES = 16
V7X_VMEM_BYTES = 64 * 1024 * 1024
VMEM_LIMIT = V7X_VMEM_BYTES * 3 // 4

FFN_TF = 512
FFN_ROW_CHUNK = 320
MLSTM_CHUNK = 256
MASK_NEG = -1e30


def _row_tile(rows, cap):
    best = None
    for t in range(BF16_SUBLANES, cap + 1, BF16_SUBLANES):
        if rows % t == 0:
            best = t
    assert best is not None, (rows, cap)
    return best


def _params(*sem, vmem=VMEM_LIMIT):
    return pltpu.CompilerParams(dimension_semantics=sem, vmem_limit_bytes=vmem)


def _rms(x, g):
    return x * lax.rsqrt(jnp.mean(x * x, axis=-1, keepdims=True) + RMS_EPS) * g


def _ffn_step(first, last, h_ref, gpre_ref, gpost_ref, wg_ref, wu_ref, wo_ref, o_ref, xn_ref, *, tf, nf, dff, rc):
    tm = h_ref.shape[0]
    chunks = [slice(r, r + rc) for r in range(0, tm, rc)]
    if first:
        for rows in chunks:
            xn_ref[rows, :] = _rms(h_ref[rows, :], gpre_ref[...]).astype(BF16)
    xn = xn_ref[...]
    g = jnp.dot(xn, wg_ref[...].astype(BF16), preferred_element_type=F32)
    u = jnp.dot(xn, wu_ref[...].astype(BF16), preferred_element_type=F32)
    a = g * jax.nn.sigmoid(g) * u
    wo = wo_ref[...].astype(BF16)
    if not last:
        y = jnp.dot(a.astype(BF16), wo, preferred_element_type=F32)
        if first:
            o_ref[...] = y
        else:
            o_ref[...] += y
        return
    first_new = (nf - 1) * tf - (dff - tf)
    col = lax.broadcasted_iota(jnp.int32, a.shape, 1)
    a = jnp.where(col >= first_new, a, 0.0).astype(BF16)
    for rows in chunks:
        y = o_ref[rows, :] + jnp.dot(a[rows, :], wo, preferred_element_type=F32)
        o_ref[rows, :] = h_ref[rows, :] + 0.5 * _rms(y, gpost_ref[...])


def _ffn_kernel(*refs, nf, **kw):
    j = pl.program_id(1)
    assert nf >= 3
    pl.when(j == 0)(functools.partial(_ffn_step, True, False, *refs, nf=nf, **kw))
    pl.when((j > 0) & (j < nf - 1))(functools.partial(_ffn_step, False, False, *refs, nf=nf, **kw))
    pl.when(j == nf - 1)(functools.partial(_ffn_step, False, True, *refs, nf=nf, **kw))


def _ffn(h, g_pre, g_post, w_in, w_out, layer, half, rows, tm):
    d = h.shape[1]
    dff = w_out.shape[2]
    tf = FFN_TF
    nf = -(-dff // tf)
    assert nf >= 2 and dff % V7X_LANES == 0 and tf % V7X_LANES == 0
    off = lambda j, base=0: (jnp.minimum(j * (tf // V7X_LANES), (dff - tf) // V7X_LANES) + base // V7X_LANES) * V7X_LANES
    wbytes = w_in.dtype.itemsize
    vmem = 3 * tm * d * 4 + tm * d * 2 + 2 * 3 * d * tf * wbytes + 4 * tm * tf * 4
    vmem = min(vmem * 5 // 4, V7X_VMEM_BYTES * 7 // 8)
    return pl.pallas_call(
        functools.partial(_ffn_kernel, tf=tf, nf=nf, dff=dff, rc=_row_tile(tm, FFN_ROW_CHUNK)),
        grid=(rows // tm, nf),
        in_specs=[
            pl.BlockSpec((tm, d), lambda i, j: (i, 0), pipeline_mode=pl.Buffered(1)),
            pl.BlockSpec((1, d), lambda i, j: (0, 0)),
            pl.BlockSpec((1, d), lambda i, j: (0, 0)),
            pl.BlockSpec((None, None, pl.Element(d), pl.Element(tf)), lambda i, j: (layer, half, 0, off(j))),
            pl.BlockSpec((None, None, pl.Element(d), pl.Element(tf)), lambda i, j: (layer, half, 0, off(j, dff))),
            pl.BlockSpec((None, None, pl.Element(tf), pl.Element(d)), lambda i, j: (layer, half, off(j), 0)),
        ],
        out_specs=pl.BlockSpec((tm, d), lambda i, j: (i, 0)),
        out_shape=jax.ShapeDtypeStruct((rows, d), F32),
        scratch_shapes=[pltpu.VMEM((tm, d), BF16)],
        compiler_params=_params("parallel", "arbitrary", vmem=vmem),
        name="ffn",
    )(h, g_pre, g_post, w_in, w_in, w_out)


def _norm_matmul_kernel(h_ref, g_ref, w_ref, o_ref, xn_ref):
    @pl.when(pl.program_id(1) == 0)
    def _():
        xn_ref[...] = _rms(h_ref[...], g_ref[...]).astype(BF16)

    o_ref[...] = jnp.dot(xn_ref[...], w_ref[...], preferred_element_type=F32).astype(o_ref.dtype)


def _norm_matmul(h, g, w, rows, tm, tn):
    d = h.shape[1]
    n = w.shape[1]
    return pl.pallas_call(
        _norm_matmul_kernel,
        grid=(rows // tm, n // tn),
        in_specs=[
            pl.BlockSpec((tm, d), lambda i, j: (i, 0)),
            pl.BlockSpec((1, d), lambda i, j: (0, 0)),
            pl.BlockSpec((d, tn), lambda i, j: (0, j)),
        ],
        out_specs=pl.BlockSpec((tm, tn), lambda i, j: (i, j)),
        out_shape=jax.ShapeDtypeStruct((rows, n), BF16),
        scratch_shapes=[pltpu.VMEM((tm, d), BF16)],
        compiler_params=_params("parallel", "arbitrary"),
        name="norm_matmul",
    )(h, g, w)


def _norm_matmul_t_kernel(h_ref, g_ref, wt_ref, o_ref, xn_ref, *, scale):
    @pl.when(pl.program_id(1) == 0)
    def _():
        xn_ref[...] = _rms(h_ref[...], g_ref[...]).astype(BF16)

    y = lax.dot_general(wt_ref[...], xn_ref[...], (((1,), (1,)), ((), ())), preferred_element_type=F32)
    o_ref[...] = (y * scale).astype(o_ref.dtype)


def _norm_matmul_t(h, g, w_t, rows, tm, tn, scale):
    d = h.shape[1]
    n = w_t.shape[0]
    return pl.pallas_call(
        functools.partial(_norm_matmul_t_kernel, scale=scale),
        grid=(rows // tm, n // tn),
        in_specs=[
            pl.BlockSpec((tm, d), lambda i, j: (i, 0)),
            pl.BlockSpec((1, d), lambda i, j: (0, 0)),
            pl.BlockSpec((tn, d), lambda i, j: (j, 0)),
        ],
        out_specs=pl.BlockSpec((tn, tm), lambda i, j: (j, i)),
        out_shape=jax.ShapeDtypeStruct((n, rows), BF16),
        scratch_shapes=[pltpu.VMEM((tm, d), BF16)],
        compiler_params=_params("parallel", "arbitrary"),
        name="norm_matmul_t",
    )(h, g, w_t)


def _gates_kernel(h_ref, g_ref, w_ref, b_ref, o_ref, *, heads):
    xn = _rms(h_ref[...], g_ref[...]).astype(BF16)
    z = jnp.dot(xn, w_ref[...], preferred_element_type=F32) + b_ref[...]
    log_sig = jnp.minimum(z, 0.0) - jnp.log1p(jnp.exp(-jnp.abs(z)))
    col = lax.broadcasted_iota(jnp.int32, z.shape, 1)
    o_ref[...] = jnp.where(col >= heads, log_sig, z)


def _gates(h, g, w_gates, bias, rows, tm):
    d = h.shape[1]
    return pl.pallas_call(
        functools.partial(_gates_kernel, heads=MLSTM_HEADS),
        grid=(rows // tm,),
        in_specs=[
            pl.BlockSpec((tm, d), lambda i: (i, 0)),
            pl.BlockSpec((1, d), lambda i: (0, 0)),
            pl.BlockSpec((d, V7X_LANES), lambda i: (0, 0)),
            pl.BlockSpec((1, V7X_LANES), lambda i: (0, 0)),
        ],
        out_specs=pl.BlockSpec((tm, V7X_LANES), lambda i: (i, 0)),
        out_shape=jax.ShapeDtypeStruct((rows, V7X_LANES), F32),
        compiler_params=_params("parallel"),
        name="mlstm_gates",
    )(h, g, w_gates, bias)


def _mlstm_head(hd, tc, qkvo_ref, gc_ref, gr_ref, ng_ref, o_ref, c_ref, n_ref, m_ref, *, heads, dk, dv):
    qk_w = heads * dk
    v_w = heads * dv
    q = qkvo_ref[0:tc, hd * dk:(hd + 1) * dk]
    k = qkvo_ref[0:tc, qk_w + hd * dk:qk_w + (hd + 1) * dk]
    v = qkvo_ref[0:tc, 2 * qk_w + hd * dv:2 * qk_w + (hd + 1) * dv]
    og = qkvo_ref[0:tc, 2 * qk_w + v_w + hd * dv:2 * qk_w + v_w + (hd + 1) * dv]
    li_col = gc_ref[0:tc, hd:hd + 1]
    lf_col = gc_ref[0:tc, heads + hd:heads + hd + 1]
    li_row = gr_ref[hd:hd + 1, 0:tc]
    lf_row = gr_ref[heads + hd:heads + hd + 1, 0:tc]
    scale = dk ** -0.5

    t_idx = lax.broadcasted_iota(jnp.int32, (tc, tc), 0)
    s_idx = lax.broadcasted_iota(jnp.int32, (tc, tc), 1)
    causal = s_idx <= t_idx
    b_col = jnp.sum(jnp.where(causal, lf_row, 0.0), axis=1, keepdims=True)
    b_row = jnp.sum(jnp.where(t_idx <= s_idx, lf_col, 0.0), axis=0, keepdims=True)

    m_prev = m_ref[hd:hd + 1, 0:1]
    d_intra = jnp.where(causal, b_col - b_row + li_row, -jnp.inf)
    d_inter = b_col + m_prev
    m_t = jnp.maximum(d_inter, jnp.max(d_intra, axis=1, keepdims=True))
    w_intra = jnp.exp(d_intra - m_t)
    w_inter = jnp.exp(d_inter - m_t)

    s = lax.dot_general(q, k, (((1,), (1,)), ((), ())), preferred_element_type=F32) * scale * w_intra
    c_prev = c_ref[hd]
    n_prev = n_ref[hd:hd + 1, :]
    q_c = jnp.dot(q, c_prev.astype(BF16), preferred_element_type=F32) * scale
    num = w_inter * q_c + jnp.dot(s.astype(BF16), v, preferred_element_type=F32)
    q_n = jnp.sum(q.astype(F32) * n_prev, axis=1, keepdims=True) * scale
    den = w_inter * q_n + jnp.sum(s, axis=1, keepdims=True)
    hh = num / jnp.maximum(jnp.abs(den), jnp.exp(-m_t))
    hh = hh * lax.rsqrt(jnp.mean(hh * hh, axis=1, keepdims=True) + RMS_EPS)
    hh = hh * ng_ref[:, hd * dv:(hd + 1) * dv] * jax.nn.sigmoid(og.astype(F32))
    o_ref[0:tc, hd * dv:(hd + 1) * dv] = hh.astype(o_ref.dtype)

    b_last = b_col[tc - 1:tc, :]
    d_state = b_last - b_col + li_col
    m_new = jnp.maximum(b_last + m_prev, jnp.max(d_state, axis=0, keepdims=True))
    w_s = jnp.exp(d_state - m_new)
    decay = jnp.exp(b_last + m_prev - m_new)
    vw = (v.astype(F32) * w_s).astype(BF16)
    c_ref[hd] = decay * c_prev + lax.dot_general(k, vw, (((0,), (0,)), ((), ())), preferred_element_type=F32)
    n_ref[hd:hd + 1, :] = decay * n_prev + jnp.sum(k.astype(F32) * w_s, axis=0, keepdims=True)
    m_ref[hd:hd + 1, :] = jnp.broadcast_to(m_new, (1, m_ref.shape[1]))


def _mlstm_kernel(qkvo_ref, gc_ref, gr_ref, ng_ref, o_ref, c_ref, n_ref, m_ref, *, heads, dk, dv, t_meta, t_chunk):
    step = pl.program_id(0)
    head = functools.partial(_mlstm_head, qkvo_ref=qkvo_ref, gc_ref=gc_ref, gr_ref=gr_ref, ng_ref=ng_ref,
                             o_ref=o_ref, c_ref=c_ref, n_ref=n_ref, m_ref=m_ref, heads=heads, dk=dk, dv=dv)

    @pl.when(step == 0)
    def _():
        c_ref[...] = jnp.zeros_like(c_ref)
        n_ref[...] = jnp.zeros_like(n_ref)
        m_ref[...] = jnp.zeros_like(m_ref)
        o_ref[...] = jnp.zeros_like(o_ref)
        for hd in range(heads):
            head(hd, t_meta)

    @pl.when(step > 0)
    def _():
        for hd in range(heads):
            head(hd, t_chunk)


def _mlstm_recurrence(qkvo, gates, gates_t, norm_g, rows_real):
    heads = MLSTM_HEADS
    rows = qkvo.shape[0]
    v_w = norm_g.shape[1]
    dv = v_w // heads
    dk = (qkvo.shape[1] - 2 * v_w) // (2 * heads)
    t = MLSTM_CHUNK
    nc = rows_real // t
    assert rows_real % t == 0 and rows - rows_real == N_META and gates_t.shape[1] == (nc + 1) * t
    blk = lambda c: (c + nc) % (nc + 1)
    return pl.pallas_call(
        functools.partial(_mlstm_kernel, heads=heads, dk=dk, dv=dv, t_meta=N_META, t_chunk=t),
        grid=(nc + 1,),
        in_specs=[
            pl.BlockSpec((t, qkvo.shape[1]), lambda c: (blk(c), 0)),
            pl.BlockSpec((t, V7X_LANES), lambda c: (blk(c), 0)),
            pl.BlockSpec((2 * heads, t), lambda c: (0, blk(c))),
            pl.BlockSpec((1, v_w), lambda c: (0, 0)),
        ],
        out_specs=pl.BlockSpec((t, v_w), lambda c: (blk(c), 0)),
        out_shape=jax.ShapeDtypeStruct((rows, v_w), BF16),
        scratch_shapes=[pltpu.VMEM((heads, dk, dv), F32), pltpu.VMEM((heads, dk), F32),
                        pltpu.VMEM((heads, V7X_LANES), F32)],
        compiler_params=_params("arbitrary"),
        name="mlstm_recurrence",
    )(qkvo, gates, gates_t, norm_g)


def _proj_res_kernel(a_ref, w_ref, g_ref, h_ref, o_ref, *, a_transposed):
    contract_a = 0 if a_transposed else 1
    y = lax.dot_general(a_ref[...], w_ref[...], (((contract_a,), (0,)), ((), ())), preferred_element_type=F32)
    o_ref[...] = h_ref[...] + _rms(y, g_ref[...])


def _proj_res(a, w, g, h, rows, tm, a_transposed=False):
    k, d = w.shape
    a_spec = pl.BlockSpec((k, tm), lambda i: (0, i)) if a_transposed else pl.BlockSpec((tm, k), lambda i: (i, 0))
    return pl.pallas_call(
        functools.partial(_proj_res_kernel, a_transposed=a_transposed),
        grid=(rows // tm,),
        in_specs=[
            a_spec,
            pl.BlockSpec((k, d), lambda i: (0, 0)),
            pl.BlockSpec((1, d), lambda i: (0, 0)),
            pl.BlockSpec((tm, d), lambda i: (i, 0)),
        ],
        out_specs=pl.BlockSpec((tm, d), lambda i: (i, 0)),
        out_shape=jax.ShapeDtypeStruct((rows, d), F32),
        compiler_params=_params("parallel"),
        name="proj_res",
    )(a, w, g, h)


def _swa_kernel(sink_ref, qt_ref, kp_ref, kc_ref, km_ref, vp_ref, vc_ref, vm_ref, bias_ref, o_ref, *, kvh, grp, hd, n_keys):
    w = qt_ref.shape[1]
    n_meta = km_ref.shape[0]
    sink_row_idx = 2 * w + n_meta
    zpad = jnp.zeros((n_keys - sink_row_idx, hd), BF16)
    bias = bias_ref[0]
    key_idx = lax.broadcasted_iota(jnp.int32, (n_keys - 2 * w, grp * w), 0) + 2 * w
    for h in range(kvh):
        base = h * grp * hd
        qt = jnp.concatenate([qt_ref[base + g * hd:base + (g + 1) * hd, :] for g in range(grp)], axis=1)
        cols = slice(h * hd, (h + 1) * hd)
        kk = jnp.concatenate([kp_ref[:, cols], kc_ref[:, cols], km_ref[:, cols], zpad], axis=0)
        vv = jnp.concatenate([vp_ref[:, cols], vc_ref[:, cols], vm_ref[:, cols], zpad], axis=0)
        s = jnp.dot(kk, qt, preferred_element_type=F32) + bias
        sink = jnp.concatenate([jnp.full((1, w), sink_ref[h * grp + g], F32) for g in range(grp)], axis=1)
        s = jnp.concatenate([s[:2 * w], jnp.where(key_idx == sink_row_idx, sink, s[2 * w:])], axis=0)
        m = jnp.max(s, axis=0, keepdims=True)
        p = jnp.exp(s - m)
        denom = jnp.sum(p, axis=0, keepdims=True)
        o = lax.dot_general(vv, p.astype(BF16), (((0,), (0,)), ((), ())), preferred_element_type=F32) / denom
        for g in range(grp):
            o_ref[base + g * hd:base + (g + 1) * hd, :] = o[:, g * w:(g + 1) * w].astype(o_ref.dtype)


def _swa_bias(n_keys):
    qi = np.arange(WINDOW)[None, :]
    r = np.arange(n_keys)[:, None]
    band = (r > qi) & (r <= qi + WINDOW) & (r < 2 * WINDOW)
    meta = (r >= 2 * WINDOW) & (r < 2 * WINDOW + N_META)
    later = band | meta
    first = (band & (r >= WINDOW)) | meta
    both = np.stack([first, later])
    both = np.where(both, 0.0, MASK_NEG).astype(np.float32)
    return jnp.asarray(np.tile(both, (1, 1, SWA_GROUP)))


def _swa(q_t, kv, sinks, rows):
    d = q_t.shape[0]
    hd = d // SWA_HEADS
    kv_w = SWA_KV_HEADS * hd
    nb = rows // WINDOW
    n_keys = -(-(2 * WINDOW + N_META + 1) // V7X_LANES) * V7X_LANES
    bias = _swa_bias(n_keys)
    meta_blk = rows // N_META
    prev = lambda n: jnp.maximum(n - 1, 0)
    return pl.pallas_call(
        functools.partial(_swa_kernel, kvh=SWA_KV_HEADS, grp=SWA_GROUP, hd=hd, n_keys=n_keys),
        grid=(nb,),
        in_specs=[
            pl.BlockSpec(memory_space=pltpu.SMEM),
            pl.BlockSpec((d, WINDOW), lambda n: (0, n)),
            pl.BlockSpec((WINDOW, kv_w), lambda n: (prev(n), 0)),
            pl.BlockSpec((WINDOW, kv_w), lambda n: (n, 0)),
            pl.BlockSpec((N_META, kv_w), lambda n: (meta_blk, 0)),
            pl.BlockSpec((WINDOW, kv_w), lambda n: (prev(n), 1)),
            pl.BlockSpec((WINDOW, kv_w), lambda n: (n, 1)),
            pl.BlockSpec((N_META, kv_w), lambda n: (meta_blk, 1)),
            pl.BlockSpec((1,) + bias.shape[1:], lambda n: (jnp.minimum(n, 1), 0, 0)),
        ],
        out_specs=pl.BlockSpec((d, WINDOW), lambda n: (0, n)),
        out_shape=jax.ShapeDtypeStruct((d, rows), BF16),
        compiler_params=_params("parallel"),
        name="swa_attention",
    )(sinks, q_t, kv, kv, kv, kv, kv, kv, bias)


def kernel(x, meta_tokens, norm_pre, norm_post, ffn_w_in, ffn_w_out, mlstm_w_in, mlstm_gate_bias, mlstm_norm_g,
           mlstm_w_out, kv_norm_g, w_kv, swa_w_q, swa_sinks, swa_w_o):
    batch, seq, d = x.shape
    assert batch == 1 and meta_tokens.shape[0] == N_META and seq % MLSTM_CHUNK == 0 and seq % WINDOW == 0
    depth = norm_pre.shape[0]
    n_a = mlstm_w_in.shape[0]
    heads = MLSTM_HEADS
    gain = lambda g: g.reshape(1, -1)
    w_ffn_in = ffn_w_in.astype(BF16)
    w_ffn_out = ffn_w_out.astype(BF16)
    ffn = lambda hh, l, k, rows, tm: _ffn(hh, gain(norm_pre[l, 2 * k]), gain(norm_post[l, 2 * k]),
                                         w_ffn_in, w_ffn_out, l, k, rows, tm)

    h = jnp.concatenate([x[0], meta_tokens.astype(x.dtype)], axis=0)
    rows = seq + N_META
    tm, tm_proj = _row_tile(rows, 512), _row_tile(rows, 1024)
    kv = None
    for l in range(depth):
        h = ffn(h, l, 0, rows, tm_proj)
        if l < n_a:
            n_main = mlstm_w_in.shape[2] - 2 * heads
            w_main = mlstm_w_in[l, :, :n_main].astype(BF16)
            w_gates = jnp.pad(mlstm_w_in[l, :, n_main:], ((0, 0), (0, V7X_LANES - 2 * heads))).astype(BF16)
            bias = jnp.pad(mlstm_gate_bias[l], (0, V7X_LANES - 2 * heads)).reshape(1, V7X_LANES)
            g_pre = gain(norm_pre[l, 1])
            qkvo = _norm_matmul(h, g_pre, w_main, rows, tm_proj, 1024)
            gates = _gates(h, g_pre, w_gates, bias, rows, tm_proj)
            n_chunks = seq // MLSTM_CHUNK + 1
            gates_t = jnp.pad(gates[:, :2 * heads].T, ((0, 0), (0, n_chunks * MLSTM_CHUNK - rows)))
            mixed = _mlstm_recurrence(qkvo, gates, gates_t, gain(mlstm_norm_g[l]), seq)
            h = _proj_res(mixed, mlstm_w_out[l].astype(BF16), gain(norm_post[l, 1]), h, rows, tm)
        else:
            j = l - n_a
            hd = d // SWA_HEADS
            q_t = _norm_matmul_t(h, gain(norm_pre[l, 1]), swa_w_q[j].T.astype(BF16), rows, tm_proj, 1024, hd ** -0.5)
            att_t = _swa(q_t, kv, swa_sinks[j], rows)
            h = _proj_res(att_t, swa_w_o[j].astype(BF16), gain(norm_post[l, 1]), h, rows, tm, a_transposed=True)
        if l == n_a - 1:
            h = ffn(h, l, 1, rows, tm_proj)
            kv =_norm_matmul(h, gain(kv_norm_g), w_kv.astype(BF16), rows, tm_proj, w_kv.shape[1])
            rows = seq
            tm, tm_proj = _row_tile(rows, 512), _row_tile(rows, 1024)
        else:
            h = ffn(h, l, 1, rows, tm_proj)
    return h.reshape(batch, seq, d)
```

```python
import functools

import numpy as np
import jax
import jax.numpy as jnp
from jax import lax
from jax.experimental import pallas as pl
from jax.experimental.pallas import tpu as pltpu

F32 = jnp.float32
BF16 = jnp.bfloat16

RMS_EPS = 1e-6
N_META = 16
MLSTM_HEADS = 8
SWA_HEADS = 32
SWA_KV_HEADS = 4
SWA_GROUP = SWA_HEADS // SWA_KV_HEADS
WINDOW = 128

V7X_LANES = 128
BF16_SUBLANES = 16
V7X_VMEM_BYTES = 64 * 1024 * 1024
VMEM_LIMIT = V7X_VMEM_BYTES * 3 // 4

FFN_TF = 512
FFN_ROW_CHUNK = 320
MLSTM_CHUNK = 256
MASK_NEG = -1e30


def _row_tile(rows, cap):
    best = None
    for t in range(BF16_SUBLANES, cap + 1, BF16_SUBLANES):
        if rows % t == 0:
            best = t
    assert best is not None, (rows, cap)
    return best


def _params(*sem, vmem=VMEM_LIMIT):
    return pltpu.CompilerParams(dimension_semantics=sem, vmem_limit_bytes=vmem)


def _rms(x, g):
    return x * lax.rsqrt(jnp.mean(x * x, axis=-1, keepdims=True) + RMS_EPS) * g


def _ffn_step(first, last, h_ref, gpre_ref, gpost_ref, wg_ref, wu_ref, wo_ref, o_ref, xn_ref, *, tf, nf, dff, rc):
    tm = h_ref.shape[0]
    chunks = [slice(r, r + rc) for r in range(0, tm, rc)]
    if first:
        for rows in chunks:
            xn_ref[rows, :] = _rms(h_ref[rows, :], gpre_ref[...]).astype(BF16)
    xn = xn_ref[...]
    g = jnp.dot(xn, wg_ref[...].astype(BF16), preferred_element_type=F32)
    u = jnp.dot(xn, wu_ref[...].astype(BF16), preferred_element_type=F32)
    a = g * jax.nn.sigmoid(g) * u
    wo = wo_ref[...].astype(BF16)
    if not last:
        y = jnp.dot(a.astype(BF16), wo, preferred_element_type=F32)
        if first:
            o_ref[...] = y
        else:
            o_ref[...] += y
        return
    first_new = (nf - 1) * tf - (dff - tf)
    col = lax.broadcasted_iota(jnp.int32, a.shape, 1)
    a = jnp.where(col >= first_new, a, 0.0).astype(BF16)
    for rows in chunks:
        y = o_ref[rows, :] + jnp.dot(a[rows, :], wo, preferred_element_type=F32)
        o_ref[rows, :] = h_ref[rows, :] + 0.5 * _rms(y, gpost_ref[...])


def _ffn_kernel(*refs, nf, **kw):
    j = pl.program_id(1)
    assert nf >= 3
    pl.when(j == 0)(functools.partial(_ffn_step, True, False, *refs, nf=nf, **kw))
    pl.when((j > 0) & (j < nf - 1))(functools.partial(_ffn_step, False, False, *refs, nf=nf, **kw))
    pl.when(j == nf - 1)(functools.partial(_ffn_step, False, True, *refs, nf=nf, **kw))


def _ffn(h, g_pre, g_post, w_in, w_out, layer, half, rows, tm):
    d = h.shape[1]
    dff = w_out.shape[2]
    tf = FFN_TF
    nf = -(-dff // tf)
    assert nf >= 2 and dff % V7X_LANES == 0 and tf % V7X_LANES == 0
    off = lambda j, base=0: (jnp.minimum(j * (tf // V7X_LANES), (dff - tf) // V7X_LANES) + base // V7X_LANES) * V7X_LANES
    wbytes = w_in.dtype.itemsize
    fixed = 2 * tm * d * 4 + tm * d * 2 + 2 * 3 * d * tf * wbytes + 4 * tm * tf * 4
    h_bufs = 2 if fixed + 2 * tm * d * 4 <= V7X_VMEM_BYTES * 5 // 8 else 1
    vmem = min((fixed + h_bufs * tm * d * 4) * 5 // 4, V7X_VMEM_BYTES * 7 // 8)
    return pl.pallas_call(
        functools.partial(_ffn_kernel, tf=tf, nf=nf, dff=dff, rc=_row_tile(tm, FFN_ROW_CHUNK)),
        grid=(rows // tm, nf),
        in_specs=[
            pl.BlockSpec((tm, d), lambda i, j: (i, 0), pipeline_mode=pl.Buffered(h_bufs)),
            pl.BlockSpec((1, d), lambda i, j: (0, 0)),
            pl.BlockSpec((1, d), lambda i, j: (0, 0)),
            pl.BlockSpec((None, None, pl.Element(d), pl.Element(tf)), lambda i, j: (layer, half, 0, off(j))),
            pl.BlockSpec((None, None, pl.Element(d), pl.Element(tf)), lambda i, j: (layer, half, 0, off(j, dff))),
            pl.BlockSpec((None, None, pl.Element(tf), pl.Element(d)), lambda i, j: (layer, half, off(j), 0)),
        ],
        out_specs=pl.BlockSpec((tm, d), lambda i, j: (i, 0)),
        out_shape=jax.ShapeDtypeStruct((rows, d), F32),
        scratch_shapes=[pltpu.VMEM((tm, d), BF16)],
        compiler_params=_params("parallel", "arbitrary", vmem=vmem),
        name="ffn",
    )(h, g_pre, g_post, w_in, w_in, w_out)


def _norm_matmul_step(first, h_ref, g_ref, w_ref, o_ref, xn_ref, *, transposed, scale, rc):
    if first:
        for r in range(0, h_ref.shape[0], rc):
            xn_ref[r:r + rc, :] = _rms(h_ref[r:r + rc, :], g_ref[...]).astype(BF16)
    if transposed:
        y = lax.dot_general(w_ref[...], xn_ref[...], (((1,), (1,)), ((), ())), preferred_element_type=F32)
    else:
        y = jnp.dot(xn_ref[...], w_ref[...], preferred_element_type=F32)
    o_ref[...] = (y if scale == 1.0 else y * scale).astype(o_ref.dtype)


def _norm_matmul_kernel(*refs, **kw):
    j = pl.program_id(1)
    pl.when(j == 0)(functools.partial(_norm_matmul_step, True, *refs, **kw))
    pl.when(j > 0)(functools.partial(_norm_matmul_step, False, *refs, **kw))


def _norm_matmul(h, g, w, rows, tm, tn, transposed=False, scale=1.0):
    d = h.shape[1]
    n = w.shape[0] if transposed else w.shape[1]
    if transposed:
        w_spec = pl.BlockSpec((tn, d), lambda i, j: (j, 0))
        o_spec = pl.BlockSpec((tn, tm), lambda i, j: (j, i))
        o_shape = (n, rows)
    else:
        w_spec = pl.BlockSpec((d, tn), lambda i, j: (0, j))
        o_spec = pl.BlockSpec((tm, tn), lambda i, j: (i, j))
        o_shape = (rows, n)
    return pl.pallas_call(
        functools.partial(_norm_matmul_kernel, transposed=transposed, scale=scale, rc=_row_tile(tm, FFN_ROW_CHUNK)),
        grid=(rows // tm, n // tn),
        in_specs=[
            pl.BlockSpec((tm, d), lambda i, j: (i, 0)),
            pl.BlockSpec((1, d), lambda i, j: (0, 0)),
            w_spec,
        ],
        out_specs=o_spec,
        out_shape=jax.ShapeDtypeStruct(o_shape, BF16),
        scratch_shapes=[pltpu.VMEM((tm, d), BF16)],
        compiler_params=_params("parallel", "arbitrary"),
        name="norm_matmul",
    )(h, g, w)


def _gates_kernel(h_ref, g_ref, w_ref, b_ref, o_ref, *, heads):
    xn = _rms(h_ref[...], g_ref[...]).astype(BF16)
    z = jnp.dot(xn, w_ref[...], preferred_element_type=F32) + b_ref[...]
    log_sig = jnp.minimum(z, 0.0) - jnp.log1p(jnp.exp(-jnp.abs(z)))
    col = lax.broadcasted_iota(jnp.int32, z.shape, 1)
    o_ref[...] = jnp.where(col >= heads, log_sig, z)


def _gates(h, g, w_gates, bias, rows, tm):
    d = h.shape[1]
    return pl.pallas_call(
        functools.partial(_gates_kernel, heads=MLSTM_HEADS),
        grid=(rows // tm,),
        in_specs=[
            pl.BlockSpec((tm, d), lambda i: (i, 0)),
            pl.BlockSpec((1, d), lambda i: (0, 0)),
            pl.BlockSpec((d, V7X_LANES), lambda i: (0, 0)),
            pl.BlockSpec((1, V7X_LANES), lambda i: (0, 0)),
        ],
        out_specs=pl.BlockSpec((tm, V7X_LANES), lambda i: (i, 0)),
        out_shape=jax.ShapeDtypeStruct((rows, V7X_LANES), F32),
        compiler_params=_params("parallel"),
        name="mlstm_gates",
    )(h, g, w_gates, bias)


def _mlstm_head(hd, tc, qkvo_ref, gc_ref, gr_ref, ng_ref, o_ref, c_ref, n_ref, m_ref, *, heads, dk, dv):
    qk_w = heads * dk
    v_w = heads * dv
    q = qkvo_ref[0:tc, hd * dk:(hd + 1) * dk]
    k = qkvo_ref[0:tc, qk_w + hd * dk:qk_w + (hd + 1) * dk]
    v = qkvo_ref[0:tc, 2 * qk_w + hd * dv:2 * qk_w + (hd + 1) * dv]
    og = qkvo_ref[0:tc, 2 * qk_w + v_w + hd * dv:2 * qk_w + v_w + (hd + 1) * dv]
    li_col = gc_ref[0:tc, hd:hd + 1]
    lf_col = gc_ref[0:tc, heads + hd:heads + hd + 1]
    li_row = gr_ref[hd:hd + 1, 0:tc]
    lf_row = gr_ref[heads + hd:heads + hd + 1, 0:tc]
    scale = dk ** -0.5

    t_idx = lax.broadcasted_iota(jnp.int32, (tc, tc), 0)
    s_idx = lax.broadcasted_iota(jnp.int32, (tc, tc), 1)
    causal = s_idx <= t_idx
    b_col = jnp.sum(jnp.where(causal, lf_row, 0.0), axis=1, keepdims=True)
    b_row = jnp.sum(jnp.where(t_idx <= s_idx, lf_col, 0.0), axis=0, keepdims=True)

    m_prev = m_ref[hd:hd + 1, 0:1]
    d_intra = jnp.where(causal, b_col - b_row + li_row, -jnp.inf)
    d_inter = b_col + m_prev
    m_t = jnp.maximum(d_inter, jnp.max(d_intra, axis=1, keepdims=True))
    w_intra = jnp.exp(d_intra - m_t)
    w_inter = jnp.exp(d_inter - m_t)

    s = lax.dot_general(q, k, (((1,), (1,)), ((), ())), preferred_element_type=F32) * scale * w_intra
    c_prev = c_ref[hd]
    n_prev = n_ref[hd:hd + 1, :]
    q_c = jnp.dot(q, c_prev.astype(BF16), preferred_element_type=F32) * scale
    num = w_inter * q_c + jnp.dot(s.astype(BF16), v, preferred_element_type=F32)
    q_n = jnp.sum(q.astype(F32) * n_prev, axis=1, keepdims=True) * scale
    den = w_inter * q_n + jnp.sum(s, axis=1, keepdims=True)
    hh = num / jnp.maximum(jnp.abs(den), jnp.exp(-m_t))
    hh = hh * lax.rsqrt(jnp.mean(hh * hh, axis=1, keepdims=True) + RMS_EPS)
    hh = hh * ng_ref[:, hd * dv:(hd + 1) * dv] * jax.nn.sigmoid(og.astype(F32))
    o_ref[0:tc, hd * dv:(hd + 1) * dv] = hh.astype(o_ref.dtype)

    b_last = b_col[tc - 1:tc, :]
    d_state = b_last - b_col + li_col
    m_new = jnp.maximum(b_last + m_prev, jnp.max(d_state, axis=0, keepdims=True))
    w_s = jnp.exp(d_state - m_new)
    decay = jnp.exp(b_last + m_prev - m_new)
    vw = (v.astype(F32) * w_s).astype(BF16)
    c_ref[hd] = decay * c_prev + lax.dot_general(k, vw, (((0,), (0,)), ((), ())), preferred_element_type=F32)
    n_ref[hd:hd + 1, :] = decay * n_prev + jnp.sum(k.astype(F32) * w_s, axis=0, keepdims=True)
    m_ref[hd:hd + 1, :] = jnp.broadcast_to(m_new, (1, m_ref.shape[1]))


def _mlstm_kernel(qkvo_ref, gc_ref, gr_ref, ng_ref, o_ref, c_ref, n_ref, m_ref, *, heads, dk, dv, t_meta, t_chunk):
    step = pl.program_id(0)
    head = functools.partial(_mlstm_head, qkvo_ref=qkvo_ref, gc_ref=gc_ref, gr_ref=gr_ref, ng_ref=ng_ref,
                             o_ref=o_ref, c_ref=c_ref, n_ref=n_ref, m_ref=m_ref, heads=heads, dk=dk, dv=dv)

    @pl.when(step == 0)
    def _():
        c_ref[...] = jnp.zeros_like(c_ref)
        n_ref[...] = jnp.zeros_like(n_ref)
        m_ref[...] = jnp.zeros_like(m_ref)
        o_ref[...] = jnp.zeros_like(o_ref)
        for hd in range(heads):
            head(hd, t_meta)

    @pl.when(step > 0)
    def _():
        for hd in range(heads):
            head(hd, t_chunk)


def _mlstm_recurrence(qkvo, gates, gates_t, norm_g, rows_real):
    heads = MLSTM_HEADS
    rows = qkvo.shape[0]
    v_w = norm_g.shape[1]
    dv = v_w // heads
    dk = (qkvo.shape[1] - 2 * v_w) // (2 * heads)
    t = MLSTM_CHUNK
    nc = rows_real // t
    assert rows_real % t == 0 and rows - rows_real == N_META and gates_t.shape[1] == (nc + 1) * t
    blk = lambda c: (c + nc) % (nc + 1)
    return pl.pallas_call(
        functools.partial(_mlstm_kernel, heads=heads, dk=dk, dv=dv, t_meta=N_META, t_chunk=t),
        grid=(nc + 1,),
        in_specs=[
            pl.BlockSpec((t, qkvo.shape[1]), lambda c: (blk(c), 0)),
            pl.BlockSpec((t, V7X_LANES), lambda c: (blk(c), 0)),
            pl.BlockSpec((2 * heads, t), lambda c: (0, blk(c))),
            pl.BlockSpec((1, v_w), lambda c: (0, 0)),
        ],
        out_specs=pl.BlockSpec((t, v_w), lambda c: (blk(c), 0)),
        out_shape=jax.ShapeDtypeStruct((rows, v_w), BF16),
        scratch_shapes=[pltpu.VMEM((heads, dk, dv), F32), pltpu.VMEM((heads, dk), F32),
                        pltpu.VMEM((heads, V7X_LANES), F32)],
        compiler_params=_params("arbitrary"),
        name="mlstm_recurrence",
    )(qkvo, gates, gates_t, norm_g)


def _proj_res_kernel(a_ref, w_ref, g_ref, h_ref, o_ref, *, a_transposed):
    contract_a = 0 if a_transposed else 1
    y = lax.dot_general(a_ref[...], w_ref[...], (((contract_a,), (0,)), ((), ())), preferred_element_type=F32)
    o_ref[...] = h_ref[...] + _rms(y, g_ref[...])


def _proj_res(a, w, g, h, rows, tm, a_transposed=False):
    k, d = w.shape
    a_spec = pl.BlockSpec((k, tm), lambda i: (0, i)) if a_transposed else pl.BlockSpec((tm, k), lambda i: (i, 0))
    return pl.pallas_call(
        functools.partial(_proj_res_kernel, a_transposed=a_transposed),
        grid=(rows // tm,),
        in_specs=[
            a_spec,
            pl.BlockSpec((k, d), lambda i: (0, 0)),
            pl.BlockSpec((1, d), lambda i: (0, 0)),
            pl.BlockSpec((tm, d), lambda i: (i, 0)),
        ],
        out_specs=pl.BlockSpec((tm, d), lambda i: (i, 0)),
        out_shape=jax.ShapeDtypeStruct((rows, d), F32),
        compiler_params=_params("parallel"),
        name="proj_res",
    )(a, w, g, h)


def _swa_kernel(sink_ref, qt_ref, kp_ref, kc_ref, km_ref, vp_ref, vc_ref, vm_ref, bias_ref, o_ref, *, kvh, grp, hd, n_keys):
    w = qt_ref.shape[1]
    n_meta = km_ref.shape[0]
    sink_row_idx = 2 * w + n_meta
    zpad = jnp.zeros((n_keys - sink_row_idx, hd), BF16)
    bias = bias_ref[0]
    key_idx = lax.broadcasted_iota(jnp.int32, (n_keys - 2 * w, grp * w), 0) + 2 * w
    for h in range(kvh):
        base = h * grp * hd
        qt = jnp.concatenate([qt_ref[base + g * hd:base + (g + 1) * hd, :] for g in range(grp)], axis=1)
        cols = slice(h * hd, (h + 1) * hd)
        kk = jnp.concatenate([kp_ref[:, cols], kc_ref[:, cols], km_ref[:, cols], zpad], axis=0)
        vv = jnp.concatenate([vp_ref[:, cols], vc_ref[:, cols], vm_ref[:, cols], zpad], axis=0)
        s = jnp.dot(kk, qt, preferred_element_type=F32) + bias
        sink = jnp.concatenate([jnp.full((1, w), sink_ref[h * grp + g], F32) for g in range(grp)], axis=1)
        s = jnp.concatenate([s[:2 * w], jnp.where(key_idx == sink_row_idx, sink, s[2 * w:])], axis=0)
        m = jnp.max(s, axis=0, keepdims=True)
        p = jnp.exp(s - m)
        denom = jnp.sum(p, axis=0, keepdims=True)
        o = lax.dot_general(vv, p.astype(BF16), (((0,), (0,)), ((), ())), preferred_element_type=F32) / denom
        for g in range(grp):
            o_ref[base + g * hd:base + (g + 1) * hd, :] = o[:, g * w:(g + 1) * w].astype(o_ref.dtype)


def _swa_bias(n_keys):
    qi = np.arange(WINDOW)[None, :]
    r = np.arange(n_keys)[:, None]
    band = (r > qi) & (r <= qi + WINDOW) & (r < 2 * WINDOW)
    meta = (r >= 2 * WINDOW) & (r < 2 * WINDOW + N_META)
    later = band | meta
    first = (band & (r >= WINDOW)) | meta
    both = np.stack([first, later])
    both = np.where(both, 0.0, MASK_NEG).astype(np.float32)
    return jnp.asarray(np.tile(both, (1, 1, SWA_GROUP)))


def _swa(q_t, kv, sinks, rows):
    d = q_t.shape[0]
    hd = d // SWA_HEADS
    kv_w = SWA_KV_HEADS * hd
    nb = rows // WINDOW
    n_keys = -(-(2 * WINDOW + N_META + 1) // V7X_LANES) * V7X_LANES
    bias = _swa_bias(n_keys)
    meta_blk = rows // N_META
    prev = lambda n: jnp.maximum(n - 1, 0)
    return pl.pallas_call(
        functools.partial(_swa_kernel, kvh=SWA_KV_HEADS, grp=SWA_GROUP, hd=hd, n_keys=n_keys),
        grid=(nb,),
        in_specs=[
            pl.BlockSpec(memory_space=pltpu.SMEM),
            pl.BlockSpec((d, WINDOW), lambda n: (0, n)),
            pl.BlockSpec((WINDOW, kv_w), lambda n: (prev(n), 0)),
            pl.BlockSpec((WINDOW, kv_w), lambda n: (n, 0)),
            pl.BlockSpec((N_META, kv_w), lambda n: (meta_blk, 0)),
            pl.BlockSpec((WINDOW, kv_w), lambda n: (prev(n), 1)),
            pl.BlockSpec((WINDOW, kv_w), lambda n: (n, 1)),
            pl.BlockSpec((N_META, kv_w), lambda n: (meta_blk, 1)),
            pl.BlockSpec((1,) + bias.shape[1:], lambda n: (jnp.minimum(n, 1), 0, 0)),
        ],
        out_specs=pl.BlockSpec((d, WINDOW), lambda n: (0, n)),
        out_shape=jax.ShapeDtypeStruct((d, rows), BF16),
        compiler_params=_params("parallel"),
        name="swa_attention",
    )(sinks, q_t, kv, kv, kv, kv, kv, kv, bias)


def kernel(x, meta_tokens, norm_pre, norm_post, ffn_w_in, ffn_w_out, mlstm_w_in, mlstm_gate_bias, mlstm_norm_g,
           mlstm_w_out, kv_norm_g, w_kv, swa_w_q, swa_sinks, swa_w_o):
    batch, seq, d = x.shape
    assert batch == 1 and meta_tokens.shape[0] == N_META and seq % MLSTM_CHUNK == 0 and seq % WINDOW == 0
    depth = norm_pre.shape[0]
    n_a = mlstm_w_in.shape[0]
    heads = MLSTM_HEADS
    gain = lambda g: g.reshape(1, -1)
    w_ffn_in = ffn_w_in.astype(BF16)
    w_ffn_out = ffn_w_out.astype(BF16)
    ffn = lambda hh, l, k, rows, tm: _ffn(hh, gain(norm_pre[l, 2 * k]), gain(norm_post[l, 2 * k]),
                                         w_ffn_in, w_ffn_out, l, k, rows, tm)

    h = jnp.concatenate([x[0], meta_tokens.astype(x.dtype)], axis=0)
    rows = seq + N_META
    tm, tm_proj = _row_tile(rows, 512), _row_tile(rows, 1024)
    kv = None
    for l in range(depth):
        h = ffn(h, l, 0, rows, tm)
        if l < n_a:
            n_main = mlstm_w_in.shape[2] - 2 * heads
            w_main = mlstm_w_in[l, :, :n_main].astype(BF16)
            w_gates = jnp.pad(mlstm_w_in[l, :, n_main:], ((0, 0), (0, V7X_LANES - 2 * heads))).astype(BF16)
            bias = jnp.pad(mlstm_gate_bias[l], (0, V7X_LANES - 2 * heads)).reshape(1, V7X_LANES)
            g_pre = gain(norm_pre[l, 1])
            qkvo = _norm_matmul(h, g_pre, w_main, rows, tm_proj, 1024)
            gates = _gates(h, g_pre, w_gates, bias, rows, tm_proj)
            n_chunks = seq // MLSTM_CHUNK + 1
            gates_t = jnp.pad(gates[:, :2 * heads].T, ((0, 0), (0, n_chunks * MLSTM_CHUNK - rows)))
            mixed = _mlstm_recurrence(qkvo, gates, gates_t, gain(mlstm_norm_g[l]), seq)
            h = _proj_res(mixed, mlstm_w_out[l].astype(BF16), gain(norm_post[l, 1]), h, rows, tm)
        else:
            j = l - n_a
            hd = d // SWA_HEADS
            q_t = _norm_matmul(h, gain(norm_pre[l, 1]), swa_w_q[j].T.astype(BF16), rows, tm_proj, 1024,
                               transposed=True, scale=hd ** -0.5)
            att_t = _swa(q_t, kv, swa_sinks[j], rows)
            h = _proj_res(att_t, swa_w_o[j].astype(BF16), gain(norm_post[l, 1]), h, rows, tm, a_transposed=True)
        if l == n_a - 1:
            h = ffn(h, l, 1, rows, tm)
            kv =_norm_matmul(h, gain(kv_norm_g), w_kv.astype(BF16), rows, tm_proj, w_kv.shape[1])
            rows = seq
            tm, tm_proj = _row_tile(rows, 512), _row_tile(rows, 1024)
        else:
            h = ffn(h, l, 1, rows, tm)
    return h.reshape(batch, seq, d)
```

```python
import functools

import numpy as np
import jax
import jax.numpy as jnp
from jax import lax
from jax.experimental import pallas as pl
from jax.experimental.pallas import tpu as pltpu

F32 = jnp.float32
BF16 = jnp.bfloat16

RMS_EPS = 1e-6
N_META = 16
MLSTM_HEADS = 8
SWA_HEADS = 32
SWA_KV_HEADS = 4
SWA_GROUP = SWA_HEADS // SWA_KV_HEADS
WINDOW = 128

V7X_LANES = 128
BF16_SUBLANES = 16
V7X_VMEM_BYTES = 64 * 1024 * 1024
VMEM_LIMIT = V7X_VMEM_BYTES * 3 // 4

FFN_TF = 256
FFN_TM_CAP = 1024
FFN_ROW_CHUNK = 320
MLSTM_CHUNK = 256
MASK_NEG = -1e30


def _row_tile(rows, cap):
    best = None
    for t in range(BF16_SUBLANES, cap + 1, BF16_SUBLANES):
        if rows % t == 0:
            best = t
    assert best is not None, (rows, cap)
    return best


def _params(*sem, vmem=VMEM_LIMIT):
    return pltpu.CompilerParams(dimension_semantics=sem, vmem_limit_bytes=vmem)


def _rms(x, g):
    return x * lax.rsqrt(jnp.mean(x * x, axis=-1, keepdims=True) + RMS_EPS) * g


def _ffn_step(first, last, h_ref, gpre_ref, gpost_ref, wg_ref, wu_ref, wo_ref, o_ref, xn_ref, *, tf, nf, dff, rc):
    tm = h_ref.shape[0]
    chunks = [slice(r, r + rc) for r in range(0, tm, rc)]
    if first:
        for rows in chunks:
            xn_ref[rows, :] = _rms(h_ref[rows, :], gpre_ref[...]).astype(BF16)
    xn = xn_ref[...]
    g = jnp.dot(xn, wg_ref[...].astype(BF16), preferred_element_type=F32)
    u = jnp.dot(xn, wu_ref[...].astype(BF16), preferred_element_type=F32)
    a = g * jax.nn.sigmoid(g) * u
    wo = wo_ref[...].astype(BF16)
    if not last:
        y = jnp.dot(a.astype(BF16), wo, preferred_element_type=F32)
        if first:
            o_ref[...] = y
        else:
            o_ref[...] += y
        return
    first_new = (nf - 1) * tf - (dff - tf)
    col = lax.broadcasted_iota(jnp.int32, a.shape, 1)
    a = jnp.where(col >= first_new, a, 0.0).astype(BF16)
    for rows in chunks:
        y = o_ref[rows, :] + jnp.dot(a[rows, :], wo, preferred_element_type=F32)
        o_ref[rows, :] = h_ref[rows, :] + 0.5 * _rms(y, gpost_ref[...])


def _ffn_kernel(*refs, nf, **kw):
    j = pl.program_id(1)
    assert nf >= 3
    pl.when(j == 0)(functools.partial(_ffn_step, True, False, *refs, nf=nf, **kw))
    pl.when((j > 0) & (j < nf - 1))(functools.partial(_ffn_step, False, False, *refs, nf=nf, **kw))
    pl.when(j == nf - 1)(functools.partial(_ffn_step, False, True, *refs, nf=nf, **kw))


def _ffn(h, g_pre, g_post, w_in, w_out, layer, half, rows, tm):
    d = h.shape[1]
    dff = w_out.shape[2]
    tf = FFN_TF
    nf = -(-dff // tf)
    assert nf >= 2 and dff % V7X_LANES == 0 and tf % V7X_LANES == 0
    off = lambda j, base=0: (jnp.minimum(j * (tf // V7X_LANES), (dff - tf) // V7X_LANES) + base // V7X_LANES) * V7X_LANES
    wbytes = w_in.dtype.itemsize
    cast_tmp = 3 * d * tf * 2 if w_in.dtype != BF16 else 0
    fixed = 2 * tm * d * 4 + tm * d * 2 + 2 * 3 * d * tf * wbytes + cast_tmp + 4 * tm * tf * 4
    h_bufs = 2 if fixed + 2 * tm * d * 4 <= V7X_VMEM_BYTES * 5 // 8 else 1
    vmem = min((fixed + h_bufs * tm * d * 4) * 5 // 4, V7X_VMEM_BYTES * 7 // 8)
    return pl.pallas_call(
        functools.partial(_ffn_kernel, tf=tf, nf=nf, dff=dff, rc=_row_tile(tm, FFN_ROW_CHUNK)),
        grid=(rows // tm, nf),
        in_specs=[
            pl.BlockSpec((tm, d), lambda i, j: (i, 0), pipeline_mode=pl.Buffered(h_bufs)),
            pl.BlockSpec((1, d), lambda i, j: (0, 0)),
            pl.BlockSpec((1, d), lambda i, j: (0, 0)),
            pl.BlockSpec((None, None, pl.Element(d), pl.Element(tf)), lambda i, j: (layer, half, 0, off(j))),
            pl.BlockSpec((None, None, pl.Element(d), pl.Element(tf)), lambda i, j: (layer, half, 0, off(j, dff))),
            pl.BlockSpec((None, None, pl.Element(tf), pl.Element(d)), lambda i, j: (layer, half, off(j), 0)),
        ],
        out_specs=pl.BlockSpec((tm, d), lambda i, j: (i, 0)),
        out_shape=jax.ShapeDtypeStruct((rows, d), F32),
        scratch_shapes=[pltpu.VMEM((tm, d), BF16)],
        compiler_params=_params("parallel", "arbitrary", vmem=vmem),
        name="ffn",
    )(h, g_pre, g_post, w_in, w_in, w_out)


def _norm_matmul_step(first, h_ref, g_ref, w_ref, o_ref, xn_ref, *, transposed, scale, rc):
    if first:
        for r in range(0, h_ref.shape[0], rc):
            xn_ref[r:r + rc, :] = _rms(h_ref[r:r + rc, :], g_ref[...]).astype(BF16)
    if transposed:
        y = lax.dot_general(w_ref[...], xn_ref[...], (((1,), (1,)), ((), ())), preferred_element_type=F32)
    else:
        y = jnp.dot(xn_ref[...], w_ref[...], preferred_element_type=F32)
    o_ref[...] = (y if scale == 1.0 else y * scale).astype(o_ref.dtype)


def _norm_matmul_kernel(*refs, **kw):
    j = pl.program_id(1)
    pl.when(j == 0)(functools.partial(_norm_matmul_step, True, *refs, **kw))
    pl.when(j > 0)(functools.partial(_norm_matmul_step, False, *refs, **kw))


def _norm_matmul(h, g, w, rows, tm, tn, transposed=False, scale=1.0):
    d = h.shape[1]
    n = w.shape[0] if transposed else w.shape[1]
    if transposed:
        w_spec = pl.BlockSpec((tn, d), lambda i, j: (j, 0))
        o_spec = pl.BlockSpec((tn, tm), lambda i, j: (j, i))
        o_shape = (n, rows)
    else:
        w_spec = pl.BlockSpec((d, tn), lambda i, j: (0, j))
        o_spec = pl.BlockSpec((tm, tn), lambda i, j: (i, j))
        o_shape = (rows, n)
    return pl.pallas_call(
        functools.partial(_norm_matmul_kernel, transposed=transposed, scale=scale, rc=_row_tile(tm, FFN_ROW_CHUNK)),
        grid=(rows // tm, n // tn),
        in_specs=[
            pl.BlockSpec((tm, d), lambda i, j: (i, 0)),
            pl.BlockSpec((1, d), lambda i, j: (0, 0)),
            w_spec,
        ],
        out_specs=o_spec,
        out_shape=jax.ShapeDtypeStruct(o_shape, BF16),
        scratch_shapes=[pltpu.VMEM((tm, d), BF16)],
        compiler_params=_params("parallel", "arbitrary"),
        name="norm_matmul",
    )(h, g, w)


def _gates_kernel(h_ref, g_ref, w_ref, b_ref, o_ref, *, heads):
    xn = _rms(h_ref[...], g_ref[...]).astype(BF16)
    z = jnp.dot(xn, w_ref[...], preferred_element_type=F32) + b_ref[...]
    log_sig = jnp.minimum(z, 0.0) - jnp.log1p(jnp.exp(-jnp.abs(z)))
    col = lax.broadcasted_iota(jnp.int32, z.shape, 1)
    o_ref[...] = jnp.where(col >= heads, log_sig, z)


def _gates(h, g, w_gates, bias, rows, tm):
    d = h.shape[1]
    return pl.pallas_call(
        functools.partial(_gates_kernel, heads=MLSTM_HEADS),
        grid=(rows // tm,),
        in_specs=[
            pl.BlockSpec((tm, d), lambda i: (i, 0)),
            pl.BlockSpec((1, d), lambda i: (0, 0)),
            pl.BlockSpec((d, V7X_LANES), lambda i: (0, 0)),
            pl.BlockSpec((1, V7X_LANES), lambda i: (0, 0)),
        ],
        out_specs=pl.BlockSpec((tm, V7X_LANES), lambda i: (i, 0)),
        out_shape=jax.ShapeDtypeStruct((rows, V7X_LANES), F32),
        compiler_params=_params("parallel"),
        name="mlstm_gates",
    )(h, g, w_gates, bias)


def _mlstm_head(hd, tc, qkvo_ref, gc_ref, gr_ref, ng_ref, o_ref, c_ref, n_ref, m_ref, *, heads, dk, dv):
    qk_w = heads * dk
    v_w = heads * dv
    q = qkvo_ref[0:tc, hd * dk:(hd + 1) * dk]
    k = qkvo_ref[0:tc, qk_w + hd * dk:qk_w + (hd + 1) * dk]
    v = qkvo_ref[0:tc, 2 * qk_w + hd * dv:2 * qk_w + (hd + 1) * dv]
    og = qkvo_ref[0:tc, 2 * qk_w + v_w + hd * dv:2 * qk_w + v_w + (hd + 1) * dv]
    li_col = gc_ref[0:tc, hd:hd + 1]
    lf_col = gc_ref[0:tc, heads + hd:heads + hd + 1]
    li_row = gr_ref[hd:hd + 1, 0:tc]
    lf_row = gr_ref[heads + hd:heads + hd + 1, 0:tc]
    scale = dk ** -0.5

    t_idx = lax.broadcasted_iota(jnp.int32, (tc, tc), 0)
    s_idx = lax.broadcasted_iota(jnp.int32, (tc, tc), 1)
    causal = s_idx <= t_idx
    b_col = jnp.sum(jnp.where(causal, lf_row, 0.0), axis=1, keepdims=True)
    b_row = jnp.sum(jnp.where(t_idx <= s_idx, lf_col, 0.0), axis=0, keepdims=True)

    m_prev = m_ref[hd:hd + 1, 0:1]
    d_intra = jnp.where(causal, b_col - b_row + li_row, -jnp.inf)
    d_inter = b_col + m_prev
    m_t = jnp.maximum(d_inter, jnp.max(d_intra, axis=1, keepdims=True))
    w_intra = jnp.exp(d_intra - m_t)
    w_inter = jnp.exp(d_inter - m_t)

    s = lax.dot_general(q, k, (((1,), (1,)), ((), ())), preferred_element_type=F32) * scale * w_intra
    c_prev = c_ref[hd]
    n_prev = n_ref[hd:hd + 1, :]
    q_c = jnp.dot(q, c_prev.astype(BF16), preferred_element_type=F32) * scale
    num = w_inter * q_c + jnp.dot(s.astype(BF16), v, preferred_element_type=F32)
    q_n = jnp.sum(q.astype(F32) * n_prev, axis=1, keepdims=True) * scale
    den = w_inter * q_n + jnp.sum(s, axis=1, keepdims=True)
    hh = num / jnp.maximum(jnp.abs(den), jnp.exp(-m_t))
    hh = hh * lax.rsqrt(jnp.mean(hh * hh, axis=1, keepdims=True) + RMS_EPS)
    hh = hh * ng_ref[:, hd * dv:(hd + 1) * dv] * jax.nn.sigmoid(og.astype(F32))
    o_ref[0:tc, hd * dv:(hd + 1) * dv] = hh.astype(o_ref.dtype)

    b_last = b_col[tc - 1:tc, :]
    d_state = b_last - b_col + li_col
    m_new = jnp.maximum(b_last + m_prev, jnp.max(d_state, axis=0, keepdims=True))
    w_s = jnp.exp(d_state - m_new)
    decay = jnp.exp(b_last + m_prev - m_new)
    vw = (v.astype(F32) * w_s).astype(BF16)
    c_ref[hd] = decay * c_prev + lax.dot_general(k, vw, (((0,), (0,)), ((), ())), preferred_element_type=F32)
    n_ref[hd:hd + 1, :] = decay * n_prev + jnp.sum(k.astype(F32) * w_s, axis=0, keepdims=True)
    m_ref[hd:hd + 1, :] = jnp.broadcast_to(m_new, (1, m_ref.shape[1]))


def _mlstm_kernel(qkvo_ref, gc_ref, gr_ref, ng_ref, o_ref, c_ref, n_ref, m_ref, *, heads, dk, dv, t_meta, t_chunk):
    step = pl.program_id(0)
    head = functools.partial(_mlstm_head, qkvo_ref=qkvo_ref, gc_ref=gc_ref, gr_ref=gr_ref, ng_ref=ng_ref,
                             o_ref=o_ref, c_ref=c_ref, n_ref=n_ref, m_ref=m_ref, heads=heads, dk=dk, dv=dv)

    @pl.when(step == 0)
    def _():
        c_ref[...] = jnp.zeros_like(c_ref)
        n_ref[...] = jnp.zeros_like(n_ref)
        m_ref[...] = jnp.zeros_like(m_ref)
        o_ref[...] = jnp.zeros_like(o_ref)
        for hd in range(heads):
            head(hd, t_meta)

    @pl.when(step > 0)
    def _():
        for hd in range(heads):
            head(hd, t_chunk)


def _mlstm_recurrence(qkvo, gates, gates_t, norm_g, rows_real):
    heads = MLSTM_HEADS
    rows = qkvo.shape[0]
    v_w = norm_g.shape[1]
    dv = v_w // heads
    dk = (qkvo.shape[1] - 2 * v_w) // (2 * heads)
    t = MLSTM_CHUNK
    nc = rows_real // t
    assert rows_real % t == 0 and rows - rows_real == N_META and gates_t.shape[1] == (nc + 1) * t
    blk = lambda c: (c + nc) % (nc + 1)
    return pl.pallas_call(
        functools.partial(_mlstm_kernel, heads=heads, dk=dk, dv=dv, t_meta=N_META, t_chunk=t),
        grid=(nc + 1,),
        in_specs=[
            pl.BlockSpec((t, qkvo.shape[1]), lambda c: (blk(c), 0)),
            pl.BlockSpec((t, V7X_LANES), lambda c: (blk(c), 0)),
            pl.BlockSpec((2 * heads, t), lambda c: (0, blk(c))),
            pl.BlockSpec((1, v_w), lambda c: (0, 0)),
        ],
        out_specs=pl.BlockSpec((t, v_w), lambda c: (blk(c), 0)),
        out_shape=jax.ShapeDtypeStruct((rows, v_w), BF16),
        scratch_shapes=[pltpu.VMEM((heads, dk, dv), F32), pltpu.VMEM((heads, dk), F32),
                        pltpu.VMEM((heads, V7X_LANES), F32)],
        compiler_params=_params("arbitrary"),
        name="mlstm_recurrence",
    )(qkvo, gates, gates_t, norm_g)


def _proj_res_kernel(a_ref, w_ref, g_ref, h_ref, o_ref, *, a_transposed):
    contract_a = 0 if a_transposed else 1
    y = lax.dot_general(a_ref[...], w_ref[...], (((contract_a,), (0,)), ((), ())), preferred_element_type=F32)
    o_ref[...] = h_ref[...] + _rms(y, g_ref[...])


def _proj_res(a, w, g, h, rows, tm, a_transposed=False):
    k, d = w.shape
    a_spec = pl.BlockSpec((k, tm), lambda i: (0, i)) if a_transposed else pl.BlockSpec((tm, k), lambda i: (i, 0))
    return pl.pallas_call(
        functools.partial(_proj_res_kernel, a_transposed=a_transposed),
        grid=(rows // tm,),
        in_specs=[
            a_spec,
            pl.BlockSpec((k, d), lambda i: (0, 0)),
            pl.BlockSpec((1, d), lambda i: (0, 0)),
            pl.BlockSpec((tm, d), lambda i: (i, 0)),
        ],
        out_specs=pl.BlockSpec((tm, d), lambda i: (i, 0)),
        out_shape=jax.ShapeDtypeStruct((rows, d), F32),
        compiler_params=_params("parallel"),
        name="proj_res",
    )(a, w, g, h)


def _swa_kernel(sink_ref, qt_ref, kp_ref, kc_ref, km_ref, vp_ref, vc_ref, vm_ref, bias_ref, o_ref, *, kvh, grp, hd, n_keys):
    w = qt_ref.shape[1]
    n_meta = km_ref.shape[0]
    sink_row_idx = 2 * w + n_meta
    zpad = jnp.zeros((n_keys - sink_row_idx, hd), BF16)
    bias = bias_ref[0]
    key_idx = lax.broadcasted_iota(jnp.int32, (n_keys - 2 * w, grp * w), 0) + 2 * w
    for h in range(kvh):
        base = h * grp * hd
        qt = jnp.concatenate([qt_ref[base + g * hd:base + (g + 1) * hd, :] for g in range(grp)], axis=1)
        cols = slice(h * hd, (h + 1) * hd)
        kk = jnp.concatenate([kp_ref[:, cols], kc_ref[:, cols], km_ref[:, cols], zpad], axis=0)
        vv = jnp.concatenate([vp_ref[:, cols], vc_ref[:, cols], vm_ref[:, cols], zpad], axis=0)
        s = jnp.dot(kk, qt, preferred_element_type=F32) + bias
        sink = jnp.concatenate([jnp.full((1, w), sink_ref[h * grp + g], F32) for g in range(grp)], axis=1)
        s = jnp.concatenate([s[:2 * w], jnp.where(key_idx == sink_row_idx, sink, s[2 * w:])], axis=0)
        m = jnp.max(s, axis=0, keepdims=True)
        p = jnp.exp(s - m)
        denom = jnp.sum(p, axis=0, keepdims=True)
        o = lax.dot_general(vv, p.astype(BF16), (((0,), (0,)), ((), ())), preferred_element_type=F32) / denom
        for g in range(grp):
            o_ref[base + g * hd:base + (g + 1) * hd, :] = o[:, g * w:(g + 1) * w].astype(o_ref.dtype)


def _swa_bias(n_keys):
    qi = np.arange(WINDOW)[None, :]
    r = np.arange(n_keys)[:, None]
    band = (r > qi) & (r <= qi + WINDOW) & (r < 2 * WINDOW)
    meta = (r >= 2 * WINDOW) & (r < 2 * WINDOW + N_META)
    later = band | meta
    first = (band & (r >= WINDOW)) | meta
    both = np.stack([first, later])
    both = np.where(both, 0.0, MASK_NEG).astype(np.float32)
    return jnp.asarray(np.tile(both, (1, 1, SWA_GROUP)))


def _swa(q_t, kv, sinks, rows):
    d = q_t.shape[0]
    hd = d // SWA_HEADS
    kv_w = SWA_KV_HEADS * hd
    nb = rows // WINDOW
    n_keys = -(-(2 * WINDOW + N_META + 1) // V7X_LANES) * V7X_LANES
    bias = _swa_bias(n_keys)
    meta_blk = rows // N_META
    prev = lambda n: jnp.maximum(n - 1, 0)
    return pl.pallas_call(
        functools.partial(_swa_kernel, kvh=SWA_KV_HEADS, grp=SWA_GROUP, hd=hd, n_keys=n_keys),
        grid=(nb,),
        in_specs=[
            pl.BlockSpec(memory_space=pltpu.SMEM),
            pl.BlockSpec((d, WINDOW), lambda n: (0, n)),
            pl.BlockSpec((WINDOW, kv_w), lambda n: (prev(n), 0)),
            pl.BlockSpec((WINDOW, kv_w), lambda n: (n, 0)),
            pl.BlockSpec((N_META, kv_w), lambda n: (meta_blk, 0)),
            pl.BlockSpec((WINDOW, kv_w), lambda n: (prev(n), 1)),
            pl.BlockSpec((WINDOW, kv_w), lambda n: (n, 1)),
            pl.BlockSpec((N_META, kv_w), lambda n: (meta_blk, 1)),
            pl.BlockSpec((1,) + bias.shape[1:], lambda n: (jnp.minimum(n, 1), 0, 0)),
        ],
        out_specs=pl.BlockSpec((d, WINDOW), lambda n: (0, n)),
        out_shape=jax.ShapeDtypeStruct((d, rows), BF16),
        compiler_params=_params("parallel"),
        name="swa_attention",
    )(sinks, q_t, kv, kv, kv, kv, kv, kv, bias)


def kernel(x, meta_tokens, norm_pre, norm_post, ffn_w_in, ffn_w_out, mlstm_w_in, mlstm_gate_bias, mlstm_norm_g,
           mlstm_w_out, kv_norm_g, w_kv, swa_w_q, swa_sinks, swa_w_o):
    batch, seq, d = x.shape
    assert batch == 1 and meta_tokens.shape[0] == N_META and seq % MLSTM_CHUNK == 0 and seq % WINDOW == 0
    depth = norm_pre.shape[0]
    n_a = mlstm_w_in.shape[0]
    heads = MLSTM_HEADS
    gain = lambda g: g.reshape(1, -1)
    ffn = lambda hh, l, k, rows: _ffn(hh, gain(norm_pre[l, 2 * k]), gain(norm_post[l, 2 * k]),
                                     ffn_w_in, ffn_w_out, l, k, rows, _row_tile(rows, FFN_TM_CAP))

    h = jnp.concatenate([x[0], meta_tokens.astype(x.dtype)], axis=0)
    rows = seq + N_META
    tm, tm_proj = _row_tile(rows, 512), _row_tile(rows, 1024)
    kv = None
    for l in range(depth):
        h = ffn(h, l, 0, rows)
        if l < n_a:
            n_main = mlstm_w_in.shape[2] - 2 * heads
            w_main = mlstm_w_in[l, :, :n_main].astype(BF16)
            w_gates = jnp.pad(mlstm_w_in[l, :, n_main:], ((0, 0), (0, V7X_LANES - 2 * heads))).astype(BF16)
            bias = jnp.pad(mlstm_gate_bias[l], (0, V7X_LANES - 2 * heads)).reshape(1, V7X_LANES)
            g_pre = gain(norm_pre[l, 1])
            qkvo = _norm_matmul(h, g_pre, w_main, rows, tm_proj, 1024)
            gates = _gates(h, g_pre, w_gates, bias, rows, tm_proj)
            n_chunks = seq // MLSTM_CHUNK + 1
            gates_t = jnp.pad(gates[:, :2 * heads].T, ((0, 0), (0, n_chunks * MLSTM_CHUNK - rows)))
            mixed = _mlstm_recurrence(qkvo, gates, gates_t, gain(mlstm_norm_g[l]), seq)
            h = _proj_res(mixed, mlstm_w_out[l].astype(BF16), gain(norm_post[l, 1]), h, rows, tm)
        else:
            j = l - n_a
            hd = d // SWA_HEADS
            q_t = _norm_matmul(h, gain(norm_pre[l, 1]), swa_w_q[j].T.astype(BF16), rows, tm_proj, 1024,
                               transposed=True, scale=hd ** -0.5)
            att_t = _swa(q_t, kv, swa_sinks[j], rows)
            h = _proj_res(att_t, swa_w_o[j].astype(BF16), gain(norm_post[l, 1]), h, rows, tm, a_transposed=True)
        if l == n_a - 1:
            h = ffn(h, l, 1, rows)
            kv =_norm_matmul(h, gain(kv_norm_g), w_kv.astype(BF16), rows, tm_proj, w_kv.shape[1])
            rows = seq
            tm, tm_proj = _row_tile(rows, 512), _row_tile(rows, 1024)
        else:
            h = ffn(h, l, 1, rows)
    return h.reshape(batch, seq, d)
```

```python
import functools

import numpy as np
import jax
import jax.numpy as jnp
from jax import lax
from jax.experimental import pallas as pl
from jax.experimental.pallas import tpu as pltpu

F32 = jnp.float32
BF16 = jnp.bfloat16

RMS_EPS = 1e-6
N_META = 16
MLSTM_HEADS = 8
SWA_HEADS = 32
SWA_KV_HEADS = 4
SWA_GROUP = SWA_HEADS // SWA_KV_HEADS
WINDOW = 128

V7X_LANES = 128
BF16_SUBLANES = 16
V7X_VMEM_BYTES = 64 * 1024 * 1024
VMEM_LIMIT = V7X_VMEM_BYTES * 3 // 4

FFN_TF = 512
FFN_TM_CAP = 1024
FFN_ROW_CHUNK = 320
MLSTM_CHUNK = 256
MASK_NEG = -1e30


def _row_tile(rows, cap):
    best = None
    for t in range(BF16_SUBLANES, cap + 1, BF16_SUBLANES):
        if rows % t == 0:
            best = t
    assert best is not None, (rows, cap)
    return best


def _params(*sem, vmem=VMEM_LIMIT):
    return pltpu.CompilerParams(dimension_semantics=sem, vmem_limit_bytes=vmem)


def _rms(x, g):
    return x * lax.rsqrt(jnp.mean(x * x, axis=-1, keepdims=True) + RMS_EPS) * g


def _ffn_step(first, last, h_ref, gpre_ref, gpost_ref, wg_ref, wu_ref, wo_ref, o_ref, xn_ref, *, tf, nf, dff, rc):
    tm = h_ref.shape[0]
    chunks = [slice(r, r + rc) for r in range(0, tm, rc)]
    if first:
        for rows in chunks:
            xn_ref[rows, :] = _rms(h_ref[rows, :], gpre_ref[...]).astype(BF16)
    xn = xn_ref[...]
    g = jnp.dot(xn, wg_ref[...].astype(BF16), preferred_element_type=F32)
    u = jnp.dot(xn, wu_ref[...].astype(BF16), preferred_element_type=F32)
    a = g * jax.nn.sigmoid(g) * u
    wo = wo_ref[...].astype(BF16)
    if not last:
        y = jnp.dot(a.astype(BF16), wo, preferred_element_type=F32)
        if first:
            o_ref[...] = y
        else:
            o_ref[...] += y
        return
    first_new = (nf - 1) * tf - (dff - tf)
    col = lax.broadcasted_iota(jnp.int32, a.shape, 1)
    a = jnp.where(col >= first_new, a, 0.0).astype(BF16)
    for rows in chunks:
        y = o_ref[rows, :] + jnp.dot(a[rows, :], wo, preferred_element_type=F32)
        o_ref[rows, :] = h_ref[rows, :] + 0.5 * _rms(y, gpost_ref[...])


def _ffn_kernel(*refs, nf, **kw):
    j = pl.program_id(1)
    assert nf >= 3
    pl.when(j == 0)(functools.partial(_ffn_step, True, False, *refs, nf=nf, **kw))
    pl.when((j > 0) & (j < nf - 1))(functools.partial(_ffn_step, False, False, *refs, nf=nf, **kw))
    pl.when(j == nf - 1)(functools.partial(_ffn_step, False, True, *refs, nf=nf, **kw))


def _ffn(h, g_pre, g_post, w_in, w_out, layer, half, rows, tm):
    d = h.shape[1]
    dff = w_out.shape[2]
    tf = FFN_TF
    nf = -(-dff // tf)
    assert nf >= 2 and dff % V7X_LANES == 0 and tf % V7X_LANES == 0
    off = lambda j, base=0: (jnp.minimum(j * (tf // V7X_LANES), (dff - tf) // V7X_LANES) + base // V7X_LANES) * V7X_LANES
    wbytes = w_in.dtype.itemsize
    cast_tmp = 3 * d * tf * 2 if w_in.dtype != BF16 else 0
    fixed = 2 * tm * d * 4 + tm * d * 2 + 2 * 3 * d * tf * wbytes + cast_tmp + 4 * tm * tf * 4
    h_bufs = 2 if fixed + 2 * tm * d * 4 <= V7X_VMEM_BYTES * 5 // 8 else 1
    vmem = min((fixed + h_bufs * tm * d * 4) * 5 // 4, V7X_VMEM_BYTES * 7 // 8)
    return pl.pallas_call(
        functools.partial(_ffn_kernel, tf=tf, nf=nf, dff=dff, rc=_row_tile(tm, FFN_ROW_CHUNK)),
        grid=(rows // tm, nf),
        in_specs=[
            pl.BlockSpec((tm, d), lambda i, j: (i, 0), pipeline_mode=pl.Buffered(h_bufs)),
            pl.BlockSpec((1, d), lambda i, j: (0, 0)),
            pl.BlockSpec((1, d), lambda i, j: (0, 0)),
        ] + ([
            pl.BlockSpec((None, None, d, tf), lambda i, j: (layer, half, 0, j)),
            pl.BlockSpec((None, None, d, tf), lambda i, j: (layer, half, 0, nf + j)),
            pl.BlockSpec((None, None, tf, d), lambda i, j: (layer, half, j, 0)),
        ] if dff % tf == 0 else [
            pl.BlockSpec((None, None, pl.Element(d), pl.Element(tf)), lambda i, j: (layer, half, 0, off(j))),
            pl.BlockSpec((None, None, pl.Element(d), pl.Element(tf)), lambda i, j: (layer, half, 0, off(j, dff))),
            pl.BlockSpec((None, None, pl.Element(tf), pl.Element(d)), lambda i, j: (layer, half, off(j), 0)),
        ]),
        out_specs=pl.BlockSpec((tm, d), lambda i, j: (i, 0)),
        out_shape=jax.ShapeDtypeStruct((rows, d), F32),
        scratch_shapes=[pltpu.VMEM((tm, d), BF16)],
        compiler_params=_params("parallel", "arbitrary", vmem=vmem),
        name="ffn",
    )(h, g_pre, g_post, w_in, w_in, w_out)


def _norm_matmul_step(first, h_ref, g_ref, w_ref, o_ref, xn_ref, *, transposed, scale, rc):
    if first:
        for r in range(0, h_ref.shape[0], rc):
            xn_ref[r:r + rc, :] = _rms(h_ref[r:r + rc, :], g_ref[...]).astype(BF16)
    if transposed:
        y = lax.dot_general(w_ref[...], xn_ref[...], (((1,), (1,)), ((), ())), preferred_element_type=F32)
    else:
        y = jnp.dot(xn_ref[...], w_ref[...], preferred_element_type=F32)
    o_ref[...] = (y if scale == 1.0 else y * scale).astype(o_ref.dtype)


def _norm_matmul_kernel(*refs, **kw):
    j = pl.program_id(1)
    pl.when(j == 0)(functools.partial(_norm_matmul_step, True, *refs, **kw))
    pl.when(j > 0)(functools.partial(_norm_matmul_step, False, *refs, **kw))


def _norm_matmul(h, g, w, rows, tm, tn, transposed=False, scale=1.0):
    d = h.shape[1]
    n = w.shape[0] if transposed else w.shape[1]
    if transposed:
        w_spec = pl.BlockSpec((tn, d), lambda i, j: (j, 0))
        o_spec = pl.BlockSpec((tn, tm), lambda i, j: (j, i))
        o_shape = (n, rows)
    else:
        w_spec = pl.BlockSpec((d, tn), lambda i, j: (0, j))
        o_spec = pl.BlockSpec((tm, tn), lambda i, j: (i, j))
        o_shape = (rows, n)
    return pl.pallas_call(
        functools.partial(_norm_matmul_kernel, transposed=transposed, scale=scale, rc=_row_tile(tm, FFN_ROW_CHUNK)),
        grid=(rows // tm, n // tn),
        in_specs=[
            pl.BlockSpec((tm, d), lambda i, j: (i, 0)),
            pl.BlockSpec((1, d), lambda i, j: (0, 0)),
            w_spec,
        ],
        out_specs=o_spec,
        out_shape=jax.ShapeDtypeStruct(o_shape, BF16),
        scratch_shapes=[pltpu.VMEM((tm, d), BF16)],
        compiler_params=_params("parallel", "arbitrary"),
        name="norm_matmul",
    )(h, g, w)


def _gates_kernel(h_ref, g_ref, w_ref, b_ref, o_ref, *, heads):
    xn = _rms(h_ref[...], g_ref[...]).astype(BF16)
    z = jnp.dot(xn, w_ref[...], preferred_element_type=F32) + b_ref[...]
    log_sig = jnp.minimum(z, 0.0) - jnp.log1p(jnp.exp(-jnp.abs(z)))
    col = lax.broadcasted_iota(jnp.int32, z.shape, 1)
    o_ref[...] = jnp.where(col >= heads, log_sig, z)


def _gates(h, g, w_gates, bias, rows, tm):
    d = h.shape[1]
    return pl.pallas_call(
        functools.partial(_gates_kernel, heads=MLSTM_HEADS),
        grid=(rows // tm,),
        in_specs=[
            pl.BlockSpec((tm, d), lambda i: (i, 0)),
            pl.BlockSpec((1, d), lambda i: (0, 0)),
            pl.BlockSpec((d, V7X_LANES), lambda i: (0, 0)),
            pl.BlockSpec((1, V7X_LANES), lambda i: (0, 0)),
        ],
        out_specs=pl.BlockSpec((tm, V7X_LANES), lambda i: (i, 0)),
        out_shape=jax.ShapeDtypeStruct((rows, V7X_LANES), F32),
        compiler_params=_params("parallel"),
        name="mlstm_gates",
    )(h, g, w_gates, bias)


def _mlstm_head(hd, tc, qkvo_ref, gc_ref, gr_ref, ng_ref, o_ref, c_ref, n_ref, m_ref, *, heads, dk, dv):
    qk_w = heads * dk
    v_w = heads * dv
    q = qkvo_ref[0:tc, hd * dk:(hd + 1) * dk]
    k = qkvo_ref[0:tc, qk_w + hd * dk:qk_w + (hd + 1) * dk]
    v = qkvo_ref[0:tc, 2 * qk_w + hd * dv:2 * qk_w + (hd + 1) * dv]
    og = qkvo_ref[0:tc, 2 * qk_w + v_w + hd * dv:2 * qk_w + v_w + (hd + 1) * dv]
    li_col = gc_ref[0:tc, hd:hd + 1]
    lf_col = gc_ref[0:tc, heads + hd:heads + hd + 1]
    li_row = gr_ref[hd:hd + 1, 0:tc]
    lf_row = gr_ref[heads + hd:heads + hd + 1, 0:tc]
    scale = dk ** -0.5

    t_idx = lax.broadcasted_iota(jnp.int32, (tc, tc), 0)
    s_idx = lax.broadcasted_iota(jnp.int32, (tc, tc), 1)
    causal = s_idx <= t_idx
    b_col = jnp.sum(jnp.where(causal, lf_row, 0.0), axis=1, keepdims=True)
    b_row = jnp.sum(jnp.where(t_idx <= s_idx, lf_col, 0.0), axis=0, keepdims=True)

    m_prev = m_ref[hd:hd + 1, 0:1]
    d_intra = jnp.where(causal, b_col - b_row + li_row, -jnp.inf)
    d_inter = b_col + m_prev
    m_t = jnp.maximum(d_inter, jnp.max(d_intra, axis=1, keepdims=True))
    w_intra = jnp.exp(d_intra - m_t)
    w_inter = jnp.exp(d_inter - m_t)

    s = lax.dot_general(q, k, (((1,), (1,)), ((), ())), preferred_element_type=F32) * scale * w_intra
    c_prev = c_ref[hd]
    n_prev = n_ref[hd:hd + 1, :]
    q_c = jnp.dot(q, c_prev.astype(BF16), preferred_element_type=F32) * scale
    num = w_inter * q_c + jnp.dot(s.astype(BF16), v, preferred_element_type=F32)
    q_n = jnp.sum(q.astype(F32) * n_prev, axis=1, keepdims=True) * scale
    den = w_inter * q_n + jnp.sum(s, axis=1, keepdims=True)
    hh = num / jnp.maximum(jnp.abs(den), jnp.exp(-m_t))
    hh = hh * lax.rsqrt(jnp.mean(hh * hh, axis=1, keepdims=True) + RMS_EPS)
    hh = hh * ng_ref[:, hd * dv:(hd + 1) * dv] * jax.nn.sigmoid(og.astype(F32))
    o_ref[0:tc, hd * dv:(hd + 1) * dv] = hh.astype(o_ref.dtype)

    b_last = b_col[tc - 1:tc, :]
    d_state = b_last - b_col + li_col
    m_new = jnp.maximum(b_last + m_prev, jnp.max(d_state, axis=0, keepdims=True))
    w_s = jnp.exp(d_state - m_new)
    decay = jnp.exp(b_last + m_prev - m_new)
    vw = (v.astype(F32) * w_s).astype(BF16)
    c_ref[hd] = decay * c_prev + lax.dot_general(k, vw, (((0,), (0,)), ((), ())), preferred_element_type=F32)
    n_ref[hd:hd + 1, :] = decay * n_prev + jnp.sum(k.astype(F32) * w_s, axis=0, keepdims=True)
    m_ref[hd:hd + 1, :] = jnp.broadcast_to(m_new, (1, m_ref.shape[1]))


def _mlstm_kernel(qkvo_ref, gc_ref, gr_ref, ng_ref, o_ref, c_ref, n_ref, m_ref, *, heads, dk, dv, t_meta, t_chunk):
    step = pl.program_id(0)
    head = functools.partial(_mlstm_head, qkvo_ref=qkvo_ref, gc_ref=gc_ref, gr_ref=gr_ref, ng_ref=ng_ref,
                             o_ref=o_ref, c_ref=c_ref, n_ref=n_ref, m_ref=m_ref, heads=heads, dk=dk, dv=dv)

    @pl.when(step == 0)
    def _():
        c_ref[...] = jnp.zeros_like(c_ref)
        n_ref[...] = jnp.zeros_like(n_ref)
        m_ref[...] = jnp.zeros_like(m_ref)
        o_ref[...] = jnp.zeros_like(o_ref)
        for hd in range(heads):
            head(hd, t_meta)

    @pl.when(step > 0)
    def _():
        for hd in range(heads):
            head(hd, t_chunk)


def _mlstm_recurrence(qkvo, gates, gates_t, norm_g, rows_real):
    heads = MLSTM_HEADS
    rows = qkvo.shape[0]
    v_w = norm_g.shape[1]
    dv = v_w // heads
    dk = (qkvo.shape[1] - 2 * v_w) // (2 * heads)
    t = MLSTM_CHUNK
    nc = rows_real // t
    assert rows_real % t == 0 and rows - rows_real == N_META and gates_t.shape[1] == (nc + 1) * t
    blk = lambda c: (c + nc) % (nc + 1)
    return pl.pallas_call(
        functools.partial(_mlstm_kernel, heads=heads, dk=dk, dv=dv, t_meta=N_META, t_chunk=t),
        grid=(nc + 1,),
        in_specs=[
            pl.BlockSpec((t, qkvo.shape[1]), lambda c: (blk(c), 0)),
            pl.BlockSpec((t, V7X_LANES), lambda c: (blk(c), 0)),
            pl.BlockSpec((2 * heads, t), lambda c: (0, blk(c))),
            pl.BlockSpec((1, v_w), lambda c: (0, 0)),
        ],
        out_specs=pl.BlockSpec((t, v_w), lambda c: (blk(c), 0)),
        out_shape=jax.ShapeDtypeStruct((rows, v_w), BF16),
        scratch_shapes=[pltpu.VMEM((heads, dk, dv), F32), pltpu.VMEM((heads, dk), F32),
                        pltpu.VMEM((heads, V7X_LANES), F32)],
        compiler_params=_params("arbitrary"),
        name="mlstm_recurrence",
    )(qkvo, gates, gates_t, norm_g)


def _proj_res_kernel(a_ref, w_ref, g_ref, h_ref, o_ref, *, a_transposed):
    contract_a = 0 if a_transposed else 1
    y = lax.dot_general(a_ref[...], w_ref[...], (((contract_a,), (0,)), ((), ())), preferred_element_type=F32)
    o_ref[...] = h_ref[...] + _rms(y, g_ref[...])


def _proj_res(a, w, g, h, rows, tm, a_transposed=False):
    k, d = w.shape
    a_spec = pl.BlockSpec((k, tm), lambda i: (0, i)) if a_transposed else pl.BlockSpec((tm, k), lambda i: (i, 0))
    return pl.pallas_call(
        functools.partial(_proj_res_kernel, a_transposed=a_transposed),
        grid=(rows // tm,),
        in_specs=[
            a_spec,
            pl.BlockSpec((k, d), lambda i: (0, 0)),
            pl.BlockSpec((1, d), lambda i: (0, 0)),
            pl.BlockSpec((tm, d), lambda i: (i, 0)),
        ],
        out_specs=pl.BlockSpec((tm, d), lambda i: (i, 0)),
        out_shape=jax.ShapeDtypeStruct((rows, d), F32),
        compiler_params=_params("parallel"),
        name="proj_res",
    )(a, w, g, h)


def _swa_kernel(sink_ref, qt_ref, kp_ref, kc_ref, km_ref, vp_ref, vc_ref, vm_ref, bias_ref, o_ref, *, kvh, grp, hd, n_keys):
    w = qt_ref.shape[1]
    n_meta = km_ref.shape[0]
    sink_row_idx = 2 * w + n_meta
    zpad = jnp.zeros((n_keys - sink_row_idx, hd), BF16)
    bias = bias_ref[0]
    key_idx = lax.broadcasted_iota(jnp.int32, (n_keys - 2 * w, grp * w), 0) + 2 * w
    for h in range(kvh):
        base = h * grp * hd
        qt = jnp.concatenate([qt_ref[base + g * hd:base + (g + 1) * hd, :] for g in range(grp)], axis=1)
        cols = slice(h * hd, (h + 1) * hd)
        kk = jnp.concatenate([kp_ref[:, cols], kc_ref[:, cols], km_ref[:, cols], zpad], axis=0)
        vv = jnp.concatenate([vp_ref[:, cols], vc_ref[:, cols], vm_ref[:, cols], zpad], axis=0)
        s = jnp.dot(kk, qt, preferred_element_type=F32) + bias
        sink = jnp.concatenate([jnp.full((1, w), sink_ref[h * grp + g], F32) for g in range(grp)], axis=1)
        s = jnp.concatenate([s[:2 * w], jnp.where(key_idx == sink_row_idx, sink, s[2 * w:])], axis=0)
        m = jnp.max(s, axis=0, keepdims=True)
        p = jnp.exp(s - m)
        denom = jnp.sum(p, axis=0, keepdims=True)
        o = lax.dot_general(vv, p.astype(BF16), (((0,), (0,)), ((), ())), preferred_element_type=F32) / denom
        for g in range(grp):
            o_ref[base + g * hd:base + (g + 1) * hd, :] = o[:, g * w:(g + 1) * w].astype(o_ref.dtype)


def _swa_bias(n_keys):
    qi = np.arange(WINDOW)[None, :]
    r = np.arange(n_keys)[:, None]
    band = (r > qi) & (r <= qi + WINDOW) & (r < 2 * WINDOW)
    meta = (r >= 2 * WINDOW) & (r < 2 * WINDOW + N_META)
    later = band | meta
    first = (band & (r >= WINDOW)) | meta
    both = np.stack([first, later])
    both = np.where(both, 0.0, MASK_NEG).astype(np.float32)
    return jnp.asarray(np.tile(both, (1, 1, SWA_GROUP)))


def _swa(q_t, kv, sinks, rows):
    d = q_t.shape[0]
    hd = d // SWA_HEADS
    kv_w = SWA_KV_HEADS * hd
    nb = rows // WINDOW
    n_keys = -(-(2 * WINDOW + N_META + 1) // V7X_LANES) * V7X_LANES
    bias = _swa_bias(n_keys)
    meta_blk = rows // N_META
    prev = lambda n: jnp.maximum(n - 1, 0)
    return pl.pallas_call(
        functools.partial(_swa_kernel, kvh=SWA_KV_HEADS, grp=SWA_GROUP, hd=hd, n_keys=n_keys),
        grid=(nb,),
        in_specs=[
            pl.BlockSpec(memory_space=pltpu.SMEM),
            pl.BlockSpec((d, WINDOW), lambda n: (0, n)),
            pl.BlockSpec((WINDOW, kv_w), lambda n: (prev(n), 0)),
            pl.BlockSpec((WINDOW, kv_w), lambda n: (n, 0)),
            pl.BlockSpec((N_META, kv_w), lambda n: (meta_blk, 0)),
            pl.BlockSpec((WINDOW, kv_w), lambda n: (prev(n), 1)),
            pl.BlockSpec((WINDOW, kv_w), lambda n: (n, 1)),
            pl.BlockSpec((N_META, kv_w), lambda n: (meta_blk, 1)),
            pl.BlockSpec((1,) + bias.shape[1:], lambda n: (jnp.minimum(n, 1), 0, 0)),
        ],
        out_specs=pl.BlockSpec((d, WINDOW), lambda n: (0, n)),
        out_shape=jax.ShapeDtypeStruct((d, rows), BF16),
        compiler_params=_params("parallel"),
        name="swa_attention",
    )(sinks, q_t, kv, kv, kv, kv, kv, kv, bias)


def kernel(x, meta_tokens, norm_pre, norm_post, ffn_w_in, ffn_w_out, mlstm_w_in, mlstm_gate_bias, mlstm_norm_g,
           mlstm_w_out, kv_norm_g, w_kv, swa_w_q, swa_sinks, swa_w_o):
    batch, seq, d = x.shape
    assert batch == 1 and meta_tokens.shape[0] == N_META and seq % MLSTM_CHUNK == 0 and seq % WINDOW == 0
    depth = norm_pre.shape[0]
    n_a = mlstm_w_in.shape[0]
    heads = MLSTM_HEADS
    gain = lambda g: g.reshape(1, -1)
    dff = ffn_w_out.shape[2]
    pad = -dff % FFN_TF
    w_ffn_in = jnp.pad(ffn_w_in.reshape(depth, 2, d, 2, dff).astype(BF16), ((0, 0),) * 4 + ((0, pad),))
    w_ffn_in = w_ffn_in.reshape(depth, 2, d, 2 * (dff + pad))
    w_ffn_out = jnp.pad(ffn_w_out.astype(BF16), ((0, 0), (0, 0), (0, pad), (0, 0)))
    ffn = lambda hh, l, k, rows: _ffn(hh, gain(norm_pre[l, 2 * k]), gain(norm_post[l, 2 * k]),
                                     w_ffn_in, w_ffn_out, l, k, rows, _row_tile(rows, FFN_TM_CAP))

    h = jnp.concatenate([x[0], meta_tokens.astype(x.dtype)], axis=0)
    rows = seq + N_META
    tm, tm_proj = _row_tile(rows, 512), _row_tile(rows, 1024)
    kv = None
    for l in range(depth):
        h = ffn(h, l, 0, rows)
        if l < n_a:
            n_main = mlstm_w_in.shape[2] - 2 * heads
            w_main = mlstm_w_in[l, :, :n_main].astype(BF16)
            w_gates = jnp.pad(mlstm_w_in[l, :, n_main:], ((0, 0), (0, V7X_LANES - 2 * heads))).astype(BF16)
            bias = jnp.pad(mlstm_gate_bias[l], (0, V7X_LANES - 2 * heads)).reshape(1, V7X_LANES)
            g_pre = gain(norm_pre[l, 1])
            qkvo = _norm_matmul(h, g_pre, w_main, rows, tm_proj, 1024)
            gates = _gates(h, g_pre, w_gates, bias, rows, tm_proj)
            n_chunks = seq // MLSTM_CHUNK + 1
            gates_t = jnp.pad(gates[:, :2 * heads].T, ((0, 0), (0, n_chunks * MLSTM_CHUNK - rows)))
            mixed = _mlstm_recurrence(qkvo, gates, gates_t, gain(mlstm_norm_g[l]), seq)
            h = _proj_res(mixed, mlstm_w_out[l].astype(BF16), gain(norm_post[l, 1]), h, rows, tm)
        else:
            j = l - n_a
            hd = d // SWA_HEADS
            q_t = _norm_matmul(h, gain(norm_pre[l, 1]), swa_w_q[j].T.astype(BF16), rows, tm_proj, 1024,
                               transposed=True, scale=hd ** -0.5)
            att_t = _swa(q_t, kv, swa_sinks[j], rows)
            h = _proj_res(att_t, swa_w_o[j].astype(BF16), gain(norm_post[l, 1]), h, rows, tm, a_transposed=True)
        if l == n_a - 1:
            h = ffn(h, l, 1, rows)
            kv =_norm_matmul(h, gain(kv_norm_g), w_kv.astype(BF16), rows, tm_proj, w_kv.shape[1])
            rows = seq
            tm, tm_proj = _row_tile(rows, 512), _row_tile(rows, 1024)
        else:
            h = ffn(h, l, 1, rows)
    return h.reshape(batch, seq, d)
```

```python
import functools

import numpy as np
import jax
import jax.numpy as jnp
from jax import lax
from jax.experimental import pallas as pl
from jax.experimental.pallas import tpu as pltpu

F32 = jnp.float32
BF16 = jnp.bfloat16

RMS_EPS = 1e-6
N_META = 16
MLSTM_HEADS = 8
SWA_HEADS = 32
SWA_KV_HEADS = 4
SWA_GROUP = SWA_HEADS // SWA_KV_HEADS
WINDOW = 128

V7X_LANES = 128
BF16_SUBLANES = 16
V7X_VMEM_BYTES = 64 * 1024 * 1024
VMEM_LIMIT = V7X_VMEM_BYTES * 3 // 4

FFN_TF = 256
FFN_TM_CAP = 1024
FFN_ROW_CHUNK = 320
MLSTM_CHUNK = 256
MASK_NEG = -1e30


def _row_tile(rows, cap):
    best = None
    for t in range(BF16_SUBLANES, cap + 1, BF16_SUBLANES):
        if rows % t == 0:
            best = t
    assert best is not None, (rows, cap)
    return best


def _params(*sem, vmem=VMEM_LIMIT):
    return pltpu.CompilerParams(dimension_semantics=sem, vmem_limit_bytes=vmem)


def _rms(x, g):
    return x * lax.rsqrt(jnp.mean(x * x, axis=-1, keepdims=True) + RMS_EPS) * g


def _ffn_step(first, last, h_ref, gpre_ref, gpost_ref, wg_ref, wu_ref, wo_ref, o_ref, xn_ref, *, tf, nf, dff, rc):
    tm = h_ref.shape[0]
    chunks = [slice(r, r + rc) for r in range(0, tm, rc)]
    if first:
        for rows in chunks:
            xn_ref[rows, :] = _rms(h_ref[rows, :], gpre_ref[...]).astype(BF16)
    xn = xn_ref[...]
    g = jnp.dot(xn, wg_ref[...].astype(BF16), preferred_element_type=F32)
    u = jnp.dot(xn, wu_ref[...].astype(BF16), preferred_element_type=F32)
    a = g * jax.nn.sigmoid(g) * u
    wo = wo_ref[...].astype(BF16)
    if not last:
        y = jnp.dot(a.astype(BF16), wo, preferred_element_type=F32)
        if first:
            o_ref[...] = y
        else:
            o_ref[...] += y
        return
    first_new = (nf - 1) * tf - (dff - tf)
    col = lax.broadcasted_iota(jnp.int32, a.shape, 1)
    a = jnp.where(col >= first_new, a, 0.0).astype(BF16)
    for rows in chunks:
        y = o_ref[rows, :] + jnp.dot(a[rows, :], wo, preferred_element_type=F32)
        o_ref[rows, :] = h_ref[rows, :] + 0.5 * _rms(y, gpost_ref[...])


def _ffn_kernel(*refs, nf, **kw):
    j = pl.program_id(1)
    assert nf >= 3
    pl.when(j == 0)(functools.partial(_ffn_step, True, False, *refs, nf=nf, **kw))
    pl.when((j > 0) & (j < nf - 1))(functools.partial(_ffn_step, False, False, *refs, nf=nf, **kw))
    pl.when(j == nf - 1)(functools.partial(_ffn_step, False, True, *refs, nf=nf, **kw))


def _ffn(h, g_pre, g_post, w_in, w_out, layer, half, rows, tm):
    d = h.shape[1]
    dff = w_out.shape[2]
    tf = FFN_TF
    nf = -(-dff // tf)
    assert nf >= 2 and dff % V7X_LANES == 0 and tf % V7X_LANES == 0
    off = lambda j, base=0: (jnp.minimum(j * (tf // V7X_LANES), (dff - tf) // V7X_LANES) + base // V7X_LANES) * V7X_LANES
    wbytes = w_in.dtype.itemsize
    cast_tmp = 3 * d * tf * 2 if w_in.dtype != BF16 else 0
    fixed = tm * d * 2 + 2 * 3 * d * tf * wbytes + cast_tmp + 4 * tm * tf * 4
    blk = tm * d * 4
    o_bufs = 2 if fixed + 4 * blk <= V7X_VMEM_BYTES * 5 // 8 else 1
    vmem = min((fixed + (2 + o_bufs) * blk) * 5 // 4, V7X_VMEM_BYTES * 7 // 8)
    return pl.pallas_call(
        functools.partial(_ffn_kernel, tf=tf, nf=nf, dff=dff, rc=_row_tile(tm, FFN_ROW_CHUNK)),
        grid=(rows // tm, nf),
        in_specs=[
            pl.BlockSpec((tm, d), lambda i, j: (i, 0)),
            pl.BlockSpec((1, d), lambda i, j: (0, 0)),
            pl.BlockSpec((1, d), lambda i, j: (0, 0)),
        ] + ([
            pl.BlockSpec((None, None, d, tf), lambda i, j: (layer, half, 0, j)),
            pl.BlockSpec((None, None, d, tf), lambda i, j: (layer, half, 0, nf + j)),
            pl.BlockSpec((None, None, tf, d), lambda i, j: (layer, half, j, 0)),
        ] if dff % tf == 0 else [
            pl.BlockSpec((None, None, pl.Element(d), pl.Element(tf)), lambda i, j: (layer, half, 0, off(j))),
            pl.BlockSpec((None, None, pl.Element(d), pl.Element(tf)), lambda i, j: (layer, half, 0, off(j, dff))),
            pl.BlockSpec((None, None, pl.Element(tf), pl.Element(d)), lambda i, j: (layer, half, off(j), 0)),
        ]),
        out_specs=pl.BlockSpec((tm, d), lambda i, j: (i, 0), pipeline_mode=pl.Buffered(o_bufs)),
        out_shape=jax.ShapeDtypeStruct((rows, d), F32),
        scratch_shapes=[pltpu.VMEM((tm, d), BF16)],
        compiler_params=_params("parallel", "arbitrary", vmem=vmem),
        name="ffn",
    )(h, g_pre, g_post, w_in, w_in, w_out)


def _norm_matmul_step(first, h_ref, g_ref, w_ref, o_ref, xn_ref, *, transposed, scale, rc):
    if first:
        for r in range(0, h_ref.shape[0], rc):
            xn_ref[r:r + rc, :] = _rms(h_ref[r:r + rc, :], g_ref[...]).astype(BF16)
    if transposed:
        y = lax.dot_general(w_ref[...], xn_ref[...], (((1,), (1,)), ((), ())), preferred_element_type=F32)
    else:
        y = jnp.dot(xn_ref[...], w_ref[...], preferred_element_type=F32)
    o_ref[...] = (y if scale == 1.0 else y * scale).astype(o_ref.dtype)


def _norm_matmul_kernel(*refs, **kw):
    j = pl.program_id(1)
    pl.when(j == 0)(functools.partial(_norm_matmul_step, True, *refs, **kw))
    pl.when(j > 0)(functools.partial(_norm_matmul_step, False, *refs, **kw))


def _norm_matmul(h, g, w, rows, tm, tn, transposed=False, scale=1.0):
    d = h.shape[1]
    n = w.shape[0] if transposed else w.shape[1]
    if transposed:
        w_spec = pl.BlockSpec((tn, d), lambda i, j: (j, 0))
        o_spec = pl.BlockSpec((tn, tm), lambda i, j: (j, i))
        o_shape = (n, rows)
    else:
        w_spec = pl.BlockSpec((d, tn), lambda i, j: (0, j))
        o_spec = pl.BlockSpec((tm, tn), lambda i, j: (i, j))
        o_shape = (rows, n)
    return pl.pallas_call(
        functools.partial(_norm_matmul_kernel, transposed=transposed, scale=scale, rc=_row_tile(tm, FFN_ROW_CHUNK)),
        grid=(rows // tm, n // tn),
        in_specs=[
            pl.BlockSpec((tm, d), lambda i, j: (i, 0)),
            pl.BlockSpec((1, d), lambda i, j: (0, 0)),
            w_spec,
        ],
        out_specs=o_spec,
        out_shape=jax.ShapeDtypeStruct(o_shape, BF16),
        scratch_shapes=[pltpu.VMEM((tm, d), BF16)],
        compiler_params=_params("parallel", "arbitrary"),
        name="norm_matmul",
    )(h, g, w)


def _gates_kernel(h_ref, g_ref, w_ref, b_ref, o_ref, *, heads):
    xn = _rms(h_ref[...], g_ref[...]).astype(BF16)
    z = jnp.dot(xn, w_ref[...], preferred_element_type=F32) + b_ref[...]
    log_sig = jnp.minimum(z, 0.0) - jnp.log1p(jnp.exp(-jnp.abs(z)))
    col = lax.broadcasted_iota(jnp.int32, z.shape, 1)
    o_ref[...] = jnp.where(col >= heads, log_sig, z)


def _gates(h, g, w_gates, bias, rows, tm):
    d = h.shape[1]
    return pl.pallas_call(
        functools.partial(_gates_kernel, heads=MLSTM_HEADS),
        grid=(rows // tm,),
        in_specs=[
            pl.BlockSpec((tm, d), lambda i: (i, 0)),
            pl.BlockSpec((1, d), lambda i: (0, 0)),
            pl.BlockSpec((d, V7X_LANES), lambda i: (0, 0)),
            pl.BlockSpec((1, V7X_LANES), lambda i: (0, 0)),
        ],
        out_specs=pl.BlockSpec((tm, V7X_LANES), lambda i: (i, 0)),
        out_shape=jax.ShapeDtypeStruct((rows, V7X_LANES), F32),
        compiler_params=_params("parallel"),
        name="mlstm_gates",
    )(h, g, w_gates, bias)


def _mlstm_head(hd, tc, qkvo_ref, gc_ref, gr_ref, ng_ref, o_ref, c_ref, n_ref, m_ref, *, heads, dk, dv):
    qk_w = heads * dk
    v_w = heads * dv
    q = qkvo_ref[0:tc, hd * dk:(hd + 1) * dk]
    k = qkvo_ref[0:tc, qk_w + hd * dk:qk_w + (hd + 1) * dk]
    v = qkvo_ref[0:tc, 2 * qk_w + hd * dv:2 * qk_w + (hd + 1) * dv]
    og = qkvo_ref[0:tc, 2 * qk_w + v_w + hd * dv:2 * qk_w + v_w + (hd + 1) * dv]
    li_col = gc_ref[0:tc, hd:hd + 1]
    lf_col = gc_ref[0:tc, heads + hd:heads + hd + 1]
    li_row = gr_ref[hd:hd + 1, 0:tc]
    lf_row = gr_ref[heads + hd:heads + hd + 1, 0:tc]
    scale = dk ** -0.5

    t_idx = lax.broadcasted_iota(jnp.int32, (tc, tc), 0)
    s_idx = lax.broadcasted_iota(jnp.int32, (tc, tc), 1)
    causal = s_idx <= t_idx
    b_col = jnp.sum(jnp.where(causal, lf_row, 0.0), axis=1, keepdims=True)
    b_row = jnp.sum(jnp.where(t_idx <= s_idx, lf_col, 0.0), axis=0, keepdims=True)

    m_prev = m_ref[hd:hd + 1, 0:1]
    d_intra = jnp.where(causal, b_col - b_row + li_row, -jnp.inf)
    d_inter = b_col + m_prev
    m_t = jnp.maximum(d_inter, jnp.max(d_intra, axis=1, keepdims=True))
    w_intra = jnp.exp(d_intra - m_t)
    w_inter = jnp.exp(d_inter - m_t)

    s = lax.dot_general(q, k, (((1,), (1,)), ((), ())), preferred_element_type=F32) * scale * w_intra
    c_prev = c_ref[hd]
    n_prev = n_ref[hd:hd + 1, :]
    q_c = jnp.dot(q, c_prev.astype(BF16), preferred_element_type=F32) * scale
    num = w_inter * q_c + jnp.dot(s.astype(BF16), v, preferred_element_type=F32)
    q_n = jnp.sum(q.astype(F32) * n_prev, axis=1, keepdims=True) * scale
    den = w_inter * q_n + jnp.sum(s, axis=1, keepdims=True)
    hh = num / jnp.maximum(jnp.abs(den), jnp.exp(-m_t))
    hh = hh * lax.rsqrt(jnp.mean(hh * hh, axis=1, keepdims=True) + RMS_EPS)
    hh = hh * ng_ref[:, hd * dv:(hd + 1) * dv] * jax.nn.sigmoid(og.astype(F32))
    o_ref[0:tc, hd * dv:(hd + 1) * dv] = hh.astype(o_ref.dtype)

    b_last = b_col[tc - 1:tc, :]
    d_state = b_last - b_col + li_col
    m_new = jnp.maximum(b_last + m_prev, jnp.max(d_state, axis=0, keepdims=True))
    w_s = jnp.exp(d_state - m_new)
    decay = jnp.exp(b_last + m_prev - m_new)
    vw = (v.astype(F32) * w_s).astype(BF16)
    c_ref[hd] = decay * c_prev + lax.dot_general(k, vw, (((0,), (0,)), ((), ())), preferred_element_type=F32)
    n_ref[hd:hd + 1, :] = decay * n_prev + jnp.sum(k.astype(F32) * w_s, axis=0, keepdims=True)
    m_ref[hd:hd + 1, :] = jnp.broadcast_to(m_new, (1, m_ref.shape[1]))


def _mlstm_kernel(qkvo_ref, gc_ref, gr_ref, ng_ref, o_ref, c_ref, n_ref, m_ref, *, heads, dk, dv, t_meta, t_chunk):
    step = pl.program_id(0)
    head = functools.partial(_mlstm_head, qkvo_ref=qkvo_ref, gc_ref=gc_ref, gr_ref=gr_ref, ng_ref=ng_ref,
                             o_ref=o_ref, c_ref=c_ref, n_ref=n_ref, m_ref=m_ref, heads=heads, dk=dk, dv=dv)

    @pl.when(step == 0)
    def _():
        c_ref[...] = jnp.zeros_like(c_ref)
        n_ref[...] = jnp.zeros_like(n_ref)
        m_ref[...] = jnp.zeros_like(m_ref)
        o_ref[...] = jnp.zeros_like(o_ref)
        for hd in range(heads):
            head(hd, t_meta)

    @pl.when(step > 0)
    def _():
        for hd in range(heads):
            head(hd, t_chunk)


def _mlstm_recurrence(qkvo, gates, gates_t, norm_g, rows_real):
    heads = MLSTM_HEADS
    rows = qkvo.shape[0]
    v_w = norm_g.shape[1]
    dv = v_w // heads
    dk = (qkvo.shape[1] - 2 * v_w) // (2 * heads)
    t = MLSTM_CHUNK
    nc = rows_real // t
    assert rows_real % t == 0 and rows - rows_real == N_META and gates_t.shape[1] == (nc + 1) * t
    blk = lambda c: (c + nc) % (nc + 1)
    return pl.pallas_call(
        functools.partial(_mlstm_kernel, heads=heads, dk=dk, dv=dv, t_meta=N_META, t_chunk=t),
        grid=(nc + 1,),
        in_specs=[
            pl.BlockSpec((t, qkvo.shape[1]), lambda c: (blk(c), 0)),
            pl.BlockSpec((t, V7X_LANES), lambda c: (blk(c), 0)),
            pl.BlockSpec((2 * heads, t), lambda c: (0, blk(c))),
            pl.BlockSpec((1, v_w), lambda c: (0, 0)),
        ],
        out_specs=pl.BlockSpec((t, v_w), lambda c: (blk(c), 0)),
        out_shape=jax.ShapeDtypeStruct((rows, v_w), BF16),
        scratch_shapes=[pltpu.VMEM((heads, dk, dv), F32), pltpu.VMEM((heads, dk), F32),
                        pltpu.VMEM((heads, V7X_LANES), F32)],
        compiler_params=_params("arbitrary"),
        name="mlstm_recurrence",
    )(qkvo, gates, gates_t, norm_g)


def _proj_res_kernel(a_ref, w_ref, g_ref, h_ref, o_ref, *, a_transposed):
    contract_a = 0 if a_transposed else 1
    y = lax.dot_general(a_ref[...], w_ref[...], (((contract_a,), (0,)), ((), ())), preferred_element_type=F32)
    o_ref[...] = h_ref[...] + _rms(y, g_ref[...])


def _proj_res(a, w, g, h, rows, tm, a_transposed=False):
    k, d = w.shape
    a_spec = pl.BlockSpec((k, tm), lambda i: (0, i)) if a_transposed else pl.BlockSpec((tm, k), lambda i: (i, 0))
    return pl.pallas_call(
        functools.partial(_proj_res_kernel, a_transposed=a_transposed),
        grid=(rows // tm,),
        in_specs=[
            a_spec,
            pl.BlockSpec((k, d), lambda i: (0, 0)),
            pl.BlockSpec((1, d), lambda i: (0, 0)),
            pl.BlockSpec((tm, d), lambda i: (i, 0)),
        ],
        out_specs=pl.BlockSpec((tm, d), lambda i: (i, 0)),
        out_shape=jax.ShapeDtypeStruct((rows, d), F32),
        compiler_params=_params("parallel"),
        name="proj_res",
    )(a, w, g, h)


def _swa_kernel(sink_ref, qt_ref, kp_ref, kc_ref, km_ref, vp_ref, vc_ref, vm_ref, bias_ref, o_ref, *, kvh, grp, hd, n_keys):
    w = qt_ref.shape[1]
    n_meta = km_ref.shape[0]
    sink_row_idx = 2 * w + n_meta
    zpad = jnp.zeros((n_keys - sink_row_idx, hd), BF16)
    bias = bias_ref[0]
    key_idx = lax.broadcasted_iota(jnp.int32, (n_keys - 2 * w, grp * w), 0) + 2 * w
    for h in range(kvh):
        base = h * grp * hd
        qt = jnp.concatenate([qt_ref[base + g * hd:base + (g + 1) * hd, :] for g in range(grp)], axis=1)
        cols = slice(h * hd, (h + 1) * hd)
        kk = jnp.concatenate([kp_ref[:, cols], kc_ref[:, cols], km_ref[:, cols], zpad], axis=0)
        vv = jnp.concatenate([vp_ref[:, cols], vc_ref[:, cols], vm_ref[:, cols], zpad], axis=0)
        s = jnp.dot(kk, qt, preferred_element_type=F32) + bias
        sink = jnp.concatenate([jnp.full((1, w), sink_ref[h * grp + g], F32) for g in range(grp)], axis=1)
        s = jnp.concatenate([s[:2 * w], jnp.where(key_idx == sink_row_idx, sink, s[2 * w:])], axis=0)
        m = jnp.max(s, axis=0, keepdims=True)
        p = jnp.exp(s - m)
        denom = jnp.sum(p, axis=0, keepdims=True)
        o = lax.dot_general(vv, p.astype(BF16), (((0,), (0,)), ((), ())), preferred_element_type=F32) / denom
        for g in range(grp):
            o_ref[base + g * hd:base + (g + 1) * hd, :] = o[:, g * w:(g + 1) * w].astype(o_ref.dtype)


def _swa_bias(n_keys):
    qi = np.arange(WINDOW)[None, :]
    r = np.arange(n_keys)[:, None]
    band = (r > qi) & (r <= qi + WINDOW) & (r < 2 * WINDOW)
    meta = (r >= 2 * WINDOW) & (r < 2 * WINDOW + N_META)
    later = band | meta
    first = (band & (r >= WINDOW)) | meta
    both = np.stack([first, later])
    both = np.where(both, 0.0, MASK_NEG).astype(np.float32)
    return jnp.asarray(np.tile(both, (1, 1, SWA_GROUP)))


def _swa(q_t, kv, sinks, rows):
    d = q_t.shape[0]
    hd = d // SWA_HEADS
    kv_w = SWA_KV_HEADS * hd
    nb = rows // WINDOW
    n_keys = -(-(2 * WINDOW + N_META + 1) // V7X_LANES) * V7X_LANES
    bias = _swa_bias(n_keys)
    meta_blk = rows // N_META
    prev = lambda n: jnp.maximum(n - 1, 0)
    return pl.pallas_call(
        functools.partial(_swa_kernel, kvh=SWA_KV_HEADS, grp=SWA_GROUP, hd=hd, n_keys=n_keys),
        grid=(nb,),
        in_specs=[
            pl.BlockSpec(memory_space=pltpu.SMEM),
            pl.BlockSpec((d, WINDOW), lambda n: (0, n)),
            pl.BlockSpec((WINDOW, kv_w), lambda n: (prev(n), 0)),
            pl.BlockSpec((WINDOW, kv_w), lambda n: (n, 0)),
            pl.BlockSpec((N_META, kv_w), lambda n: (meta_blk, 0)),
            pl.BlockSpec((WINDOW, kv_w), lambda n: (prev(n), 1)),
            pl.BlockSpec((WINDOW, kv_w), lambda n: (n, 1)),
            pl.BlockSpec((N_META, kv_w), lambda n: (meta_blk, 1)),
            pl.BlockSpec((1,) + bias.shape[1:], lambda n: (jnp.minimum(n, 1), 0, 0)),
        ],
        out_specs=pl.BlockSpec((d, WINDOW), lambda n: (0, n)),
        out_shape=jax.ShapeDtypeStruct((d, rows), BF16),
        compiler_params=_params("parallel"),
        name="swa_attention",
    )(sinks, q_t, kv, kv, kv, kv, kv, kv, bias)


def kernel(x, meta_tokens, norm_pre, norm_post, ffn_w_in, ffn_w_out, mlstm_w_in, mlstm_gate_bias, mlstm_norm_g,
           mlstm_w_out, kv_norm_g, w_kv, swa_w_q, swa_sinks, swa_w_o):
    batch, seq, d = x.shape
    assert batch == 1 and meta_tokens.shape[0] == N_META and seq % MLSTM_CHUNK == 0 and seq % WINDOW == 0
    depth = norm_pre.shape[0]
    n_a = mlstm_w_in.shape[0]
    heads = MLSTM_HEADS
    gain = lambda g: g.reshape(1, -1)
    ffn = lambda hh, l, k, rows: _ffn(hh, gain(norm_pre[l, 2 * k]), gain(norm_post[l, 2 * k]),
                                     ffn_w_in, ffn_w_out, l, k, rows, _row_tile(rows, FFN_TM_CAP))

    h = jnp.concatenate([x[0], meta_tokens.astype(x.dtype)], axis=0)
    rows = seq + N_META
    tm, tm_proj = _row_tile(rows, 512), _row_tile(rows, 1024)
    kv = None
    for l in range(depth):
        h = ffn(h, l, 0, rows)
        if l < n_a:
            n_main = mlstm_w_in.shape[2] - 2 * heads
            w_main = mlstm_w_in[l, :, :n_main].astype(BF16)
            w_gates = jnp.pad(mlstm_w_in[l, :, n_main:], ((0, 0), (0, V7X_LANES - 2 * heads))).astype(BF16)
            bias = jnp.pad(mlstm_gate_bias[l], (0, V7X_LANES - 2 * heads)).reshape(1, V7X_LANES)
            g_pre = gain(norm_pre[l, 1])
            qkvo = _norm_matmul(h, g_pre, w_main, rows, tm_proj, 1024)
            gates = _gates(h, g_pre, w_gates, bias, rows, tm_proj)
            n_chunks = seq // MLSTM_CHUNK + 1
            gates_t = jnp.pad(gates[:, :2 * heads].T, ((0, 0), (0, n_chunks * MLSTM_CHUNK - rows)))
            mixed = _mlstm_recurrence(qkvo, gates, gates_t, gain(mlstm_norm_g[l]), seq)
            h = _proj_res(mixed, mlstm_w_out[l].astype(BF16), gain(norm_post[l, 1]), h, rows, tm)
        else:
            j = l - n_a
            hd = d // SWA_HEADS
            q_t = _norm_matmul(h, gain(norm_pre[l, 1]), swa_w_q[j].T.astype(BF16), rows, tm_proj, 1024,
                               transposed=True, scale=hd ** -0.5)
            att_t = _swa(q_t, kv, swa_sinks[j], rows)
            h = _proj_res(att_t, swa_w_o[j].astype(BF16), gain(norm_post[l, 1]), h, rows, tm, a_transposed=True)
        if l == n_a - 1:
            h = ffn(h, l, 1, rows)
            kv =_norm_matmul(h, gain(kv_norm_g), w_kv.astype(BF16), rows, tm_proj, w_kv.shape[1])
            rows = seq
            tm, tm_proj = _row_tile(rows, 512), _row_tile(rows, 1024)
        else:
            h = ffn(h, l, 1, rows)
    return h.reshape(batch, seq, d)
```

```python
import functools

import numpy as np
import jax
import jax.numpy as jnp
from jax import lax
from jax.experimental import pallas as pl
from jax.experimental.pallas import tpu as pltpu

F32 = jnp.float32
BF16 = jnp.bfloat16

RMS_EPS = 1e-6
N_META = 16
MLSTM_HEADS = 8
SWA_HEADS = 32
SWA_KV_HEADS = 4
SWA_GROUP = SWA_HEADS // SWA_KV_HEADS
WINDOW = 128

V7X_LANES = 128
BF16_SUBLANES = 16
V7X_VMEM_BYTES = 64 * 1024 * 1024
VMEM_LIMIT = V7X_VMEM_BYTES * 3 // 4

FFN_TF = 256
FFN_TM_CAP = 1024
FFN_ROW_CHUNK = 320
MLSTM_CHUNK = 256
MASK_NEG = -1e30


def _row_tile(rows, cap):
    best = None
    for t in range(BF16_SUBLANES, cap + 1, BF16_SUBLANES):
        if rows % t == 0:
            best = t
    assert best is not None, (rows, cap)
    return best


def _params(*sem, vmem=VMEM_LIMIT):
    return pltpu.CompilerParams(dimension_semantics=sem, vmem_limit_bytes=vmem)


def _rms(x, g):
    return x * lax.rsqrt(jnp.mean(x * x, axis=-1, keepdims=True) + RMS_EPS) * g


def _ffn_step(first, last, h_ref, gpre_ref, gpost_ref, wg_ref, wu_ref, wo_ref, o_ref, xn_ref, *, tf, nf, dff, rc,
              chunk_done=None):
    tm = h_ref.shape[0]
    chunks = [slice(r, r + rc) for r in range(0, tm, rc)]
    if first:
        for rows in chunks:
            xn_ref[rows, :] = _rms(h_ref[rows, :], gpre_ref[...]).astype(BF16)
    xn = xn_ref[...]
    g = jnp.dot(xn, wg_ref[...].astype(BF16), preferred_element_type=F32)
    u = jnp.dot(xn, wu_ref[...].astype(BF16), preferred_element_type=F32)
    a = g * jax.nn.sigmoid(g) * u
    wo = wo_ref[...].astype(BF16)
    if not last:
        y = jnp.dot(a.astype(BF16), wo, preferred_element_type=F32)
        if first:
            o_ref[...] = y
        else:
            o_ref[...] += y
        return
    first_new = (nf - 1) * tf - (dff - tf)
    col = lax.broadcasted_iota(jnp.int32, a.shape, 1)
    a = jnp.where(col >= first_new, a, 0.0).astype(BF16)
    for k, rows in enumerate(chunks):
        y = o_ref[rows, :] + jnp.dot(a[rows, :], wo, preferred_element_type=F32)
        o_ref[rows, :] = h_ref[rows, :] + 0.5 * _rms(y, gpost_ref[...])
        if chunk_done is not None:
            chunk_done(k, rows.start)


def _ffn_kernel(h_hbm, gpre_ref, gpost_ref, wg_ref, wu_ref, wo_ref, o_hbm, xn_ref, acc_ref, hbuf_ref, sem_h, sem_o,
                *, nf, rc, n_blocks, **kw):
    i = pl.program_id(0)
    j = pl.program_id(1)
    tm = acc_ref.shape[0]
    assert nf >= 3
    slot = i % 2
    starts = list(range(0, tm, rc))

    def h_copy(block, s):
        return pltpu.make_async_copy(h_hbm.at[pl.ds(pl.multiple_of(block * tm, 8), tm), :], hbuf_ref.at[s], sem_h.at[s])

    def o_copy(block, k, r):
        dst = o_hbm.at[pl.ds(pl.multiple_of(block * tm + r, 8), rc), :]
        return pltpu.make_async_copy(acc_ref.at[pl.ds(r, rc), :], dst, sem_o.at[k])

    step = functools.partial(_ffn_step, h_ref=hbuf_ref.at[slot], gpre_ref=gpre_ref, gpost_ref=gpost_ref, wg_ref=wg_ref,
                             wu_ref=wu_ref, wo_ref=wo_ref, o_ref=acc_ref, xn_ref=xn_ref, nf=nf, rc=rc, **kw)

    @pl.when(j == 0)
    def _():
        @pl.when(i == 0)
        def _():
            h_copy(0, 0).start()

        h_copy(i, slot).wait()

        @pl.when(i > 0)
        def _():
            for k, r in enumerate(starts):
                o_copy(i - 1, k, r).wait()

        step(True, False)

    @pl.when((j > 0) & (j < nf - 1))
    def _():
        @pl.when((j == 1) & (i + 1 < n_blocks))
        def _():
            h_copy(i + 1, 1 - slot).start()

        step(False, False)

    @pl.when(j == nf - 1)
    def _():
        step(False, True, chunk_done=lambda k, r: o_copy(i, k, r).start())

        @pl.when(i == n_blocks - 1)
        def _():
            for k, r in enumerate(starts):
                o_copy(i, k, r).wait()


def _ffn(h, g_pre, g_post, w_in, w_out, layer, half, rows, tm):
    d = h.shape[1]
    dff = w_out.shape[2]
    tf = FFN_TF
    nf = -(-dff // tf)
    assert nf >= 2 and dff % V7X_LANES == 0 and tf % V7X_LANES == 0
    off = lambda j, base=0: (jnp.minimum(j * (tf // V7X_LANES), (dff - tf) // V7X_LANES) + base // V7X_LANES) * V7X_LANES
    wbytes = w_in.dtype.itemsize
    cast_tmp = 3 * d * tf * 2 if w_in.dtype != BF16 else 0
    vmem = 3 * tm * d * 4 + tm * d * 2 + 2 * 3 * d * tf * wbytes + cast_tmp + 4 * tm * tf * 4
    vmem = min(vmem * 5 // 4, V7X_VMEM_BYTES * 7 // 8)
    rc = _row_tile(tm, FFN_ROW_CHUNK)
    return pl.pallas_call(
        functools.partial(_ffn_kernel, tf=tf, nf=nf, dff=dff, rc=rc, n_blocks=rows // tm),
        grid=(rows // tm, nf),
        in_specs=[
            pl.BlockSpec(memory_space=pl.ANY),
            pl.BlockSpec((1, d), lambda i, j: (0, 0)),
            pl.BlockSpec((1, d), lambda i, j: (0, 0)),
        ] + ([
            pl.BlockSpec((None, None, d, tf), lambda i, j: (layer, half, 0, j)),
            pl.BlockSpec((None, None, d, tf), lambda i, j: (layer, half, 0, nf + j)),
            pl.BlockSpec((None, None, tf, d), lambda i, j: (layer, half, j, 0)),
        ] if dff % tf == 0 else [
            pl.BlockSpec((None, None, pl.Element(d), pl.Element(tf)), lambda i, j: (layer, half, 0, off(j))),
            pl.BlockSpec((None, None, pl.Element(d), pl.Element(tf)), lambda i, j: (layer, half, 0, off(j, dff))),
            pl.BlockSpec((None, None, pl.Element(tf), pl.Element(d)), lambda i, j: (layer, half, off(j), 0)),
        ]),
        out_specs=pl.BlockSpec(memory_space=pl.ANY),
        out_shape=jax.ShapeDtypeStruct((rows, d), F32),
        scratch_shapes=[pltpu.VMEM((tm, d), BF16), pltpu.VMEM((tm, d), F32), pltpu.VMEM((2, tm, d), F32),
                        pltpu.SemaphoreType.DMA((2,)), pltpu.SemaphoreType.DMA((tm // rc,))],
        compiler_params=_params("arbitrary", "arbitrary", vmem=vmem),
        name="ffn",
    )(h, g_pre, g_post, w_in, w_in, w_out)


def _norm_matmul_step(first, h_ref, g_ref, w_ref, o_ref, xn_ref, *, transposed, scale, rc):
    if first:
        for r in range(0, h_ref.shape[0], rc):
            xn_ref[r:r + rc, :] = _rms(h_ref[r:r + rc, :], g_ref[...]).astype(BF16)
    if transposed:
        y = lax.dot_general(w_ref[...], xn_ref[...], (((1,), (1,)), ((), ())), preferred_element_type=F32)
    else:
        y = jnp.dot(xn_ref[...], w_ref[...].astype(BF16), preferred_element_type=F32)
    o_ref[...] = (y if scale == 1.0 else y * scale).astype(o_ref.dtype)


def _norm_matmul_kernel(*refs, **kw):
    j = pl.program_id(1)
    pl.when(j == 0)(functools.partial(_norm_matmul_step, True, *refs, **kw))
    pl.when(j > 0)(functools.partial(_norm_matmul_step, False, *refs, **kw))


def _norm_matmul(h, g, w, rows, tm, tn, transposed=False, scale=1.0, layer=None, n=None):
    d = h.shape[1]
    if layer is not None:
        assert not transposed and n % tn == 0
        w_spec = pl.BlockSpec((None, d, tn), lambda i, j: (layer, 0, j))
        o_spec = pl.BlockSpec((tm, tn), lambda i, j: (i, j))
        o_shape = (rows, n)
    elif transposed:
        n = w.shape[0]
        w_spec = pl.BlockSpec((tn, d), lambda i, j: (j, 0))
        o_spec = pl.BlockSpec((tn, tm), lambda i, j: (j, i))
        o_shape = (n, rows)
    else:
        n = w.shape[1]
        w_spec = pl.BlockSpec((d, tn), lambda i, j: (0, j))
        o_spec = pl.BlockSpec((tm, tn), lambda i, j: (i, j))
        o_shape = (rows, n)
    return pl.pallas_call(
        functools.partial(_norm_matmul_kernel, transposed=transposed, scale=scale, rc=_row_tile(tm, FFN_ROW_CHUNK)),
        grid=(rows // tm, n // tn),
        in_specs=[
            pl.BlockSpec((tm, d), lambda i, j: (i, 0)),
            pl.BlockSpec((1, d), lambda i, j: (0, 0)),
            w_spec,
        ],
        out_specs=o_spec,
        out_shape=jax.ShapeDtypeStruct(o_shape, BF16),
        scratch_shapes=[pltpu.VMEM((tm, d), BF16)],
        compiler_params=_params("parallel", "arbitrary"),
        name="norm_matmul",
    )(h, g, w)


def _gates_kernel(h_ref, g_ref, w_ref, b_ref, o_ref, *, heads):
    xn = _rms(h_ref[...], g_ref[...]).astype(BF16)
    z = jnp.dot(xn, w_ref[...], preferred_element_type=F32) + b_ref[...]
    log_sig = jnp.minimum(z, 0.0) - jnp.log1p(jnp.exp(-jnp.abs(z)))
    col = lax.broadcasted_iota(jnp.int32, z.shape, 1)
    o_ref[...] = jnp.where(col >= heads, log_sig, z)


def _gates(h, g, w_gates, bias, rows, tm):
    d = h.shape[1]
    return pl.pallas_call(
        functools.partial(_gates_kernel, heads=MLSTM_HEADS),
        grid=(rows // tm,),
        in_specs=[
            pl.BlockSpec((tm, d), lambda i: (i, 0)),
            pl.BlockSpec((1, d), lambda i: (0, 0)),
            pl.BlockSpec((d, V7X_LANES), lambda i: (0, 0)),
            pl.BlockSpec((1, V7X_LANES), lambda i: (0, 0)),
        ],
        out_specs=pl.BlockSpec((tm, V7X_LANES), lambda i: (i, 0)),
        out_shape=jax.ShapeDtypeStruct((rows, V7X_LANES), F32),
        compiler_params=_params("parallel"),
        name="mlstm_gates",
    )(h, g, w_gates, bias)


def _mlstm_head(hd, tc, qkvo_ref, gc_ref, gr_ref, ng_ref, o_ref, c_ref, n_ref, m_ref, *, heads, dk, dv):
    qk_w = heads * dk
    v_w = heads * dv
    q = qkvo_ref[0:tc, hd * dk:(hd + 1) * dk]
    k = qkvo_ref[0:tc, qk_w + hd * dk:qk_w + (hd + 1) * dk]
    v = qkvo_ref[0:tc, 2 * qk_w + hd * dv:2 * qk_w + (hd + 1) * dv]
    og = qkvo_ref[0:tc, 2 * qk_w + v_w + hd * dv:2 * qk_w + v_w + (hd + 1) * dv]
    li_col = gc_ref[0:tc, hd:hd + 1]
    lf_col = gc_ref[0:tc, heads + hd:heads + hd + 1]
    li_row = gr_ref[hd:hd + 1, 0:tc]
    lf_row = gr_ref[heads + hd:heads + hd + 1, 0:tc]
    scale = dk ** -0.5

    t_idx = lax.broadcasted_iota(jnp.int32, (tc, tc), 0)
    s_idx = lax.broadcasted_iota(jnp.int32, (tc, tc), 1)
    causal = s_idx <= t_idx
    b_col = jnp.sum(jnp.where(causal, lf_row, 0.0), axis=1, keepdims=True)
    b_row = jnp.sum(jnp.where(t_idx <= s_idx, lf_col, 0.0), axis=0, keepdims=True)

    m_prev = m_ref[hd:hd + 1, 0:1]
    d_intra = jnp.where(causal, b_col - b_row + li_row, -jnp.inf)
    d_inter = b_col + m_prev
    m_t = jnp.maximum(d_inter, jnp.max(d_intra, axis=1, keepdims=True))
    w_intra = jnp.exp(d_intra - m_t)
    w_inter = jnp.exp(d_inter - m_t)

    s = lax.dot_general(q, k, (((1,), (1,)), ((), ())), preferred_element_type=F32) * scale * w_intra
    c_prev = c_ref[hd]
    n_prev = n_ref[hd:hd + 1, :]
    q_c = jnp.dot(q, c_prev.astype(BF16), preferred_element_type=F32) * scale
    num = w_inter * q_c + jnp.dot(s.astype(BF16), v, preferred_element_type=F32)
    q_n = jnp.sum(q.astype(F32) * n_prev, axis=1, keepdims=True) * scale
    den = w_inter * q_n + jnp.sum(s, axis=1, keepdims=True)
    hh = num / jnp.maximum(jnp.abs(den), jnp.exp(-m_t))
    hh = hh * lax.rsqrt(jnp.mean(hh * hh, axis=1, keepdims=True) + RMS_EPS)
    hh = hh * ng_ref[:, hd * dv:(hd + 1) * dv] * jax.nn.sigmoid(og.astype(F32))
    o_ref[0:tc, hd * dv:(hd + 1) * dv] = hh.astype(o_ref.dtype)

    b_last = b_col[tc - 1:tc, :]
    d_state = b_last - b_col + li_col
    m_new = jnp.maximum(b_last + m_prev, jnp.max(d_state, axis=0, keepdims=True))
    w_s = jnp.exp(d_state - m_new)
    decay = jnp.exp(b_last + m_prev - m_new)
    vw = (v.astype(F32) * w_s).astype(BF16)
    c_ref[hd] = decay * c_prev + lax.dot_general(k, vw, (((0,), (0,)), ((), ())), preferred_element_type=F32)
    n_ref[hd:hd + 1, :] = decay * n_prev + jnp.sum(k.astype(F32) * w_s, axis=0, keepdims=True)
    m_ref[hd:hd + 1, :] = jnp.broadcast_to(m_new, (1, m_ref.shape[1]))


def _mlstm_kernel(qkvo_ref, gc_ref, gr_ref, ng_ref, o_ref, c_ref, n_ref, m_ref, *, heads, dk, dv, t_meta, t_chunk):
    step = pl.program_id(0)

    def chunk(tc):
        for hd in range(heads):
            _mlstm_head(hd, tc, qkvo_ref, gc_ref, gr_ref, ng_ref, o_ref, c_ref, n_ref, m_ref,
                        heads=heads, dk=dk, dv=dv)

    @pl.when(step == 0)
    def _():
        c_ref[...] = jnp.zeros_like(c_ref)
        n_ref[...] = jnp.zeros_like(n_ref)
        m_ref[...] = jnp.zeros_like(m_ref)
        o_ref[...] = jnp.zeros_like(o_ref)
        chunk(t_meta)

    @pl.when(step > 0)
    def _():
        chunk(t_chunk)


def _mlstm_recurrence(qkvo, gates, gates_t, norm_g, rows_real):
    heads = MLSTM_HEADS
    rows = qkvo.shape[0]
    v_w = norm_g.shape[1]
    dv = v_w // heads
    dk = (qkvo.shape[1] - 2 * v_w) // (2 * heads)
    t = MLSTM_CHUNK
    nc = rows_real // t
    assert rows_real % t == 0 and rows - rows_real == N_META and gates_t.shape[1] == (nc + 1) * t
    blk = lambda c: (c + nc) % (nc + 1)
    return pl.pallas_call(
        functools.partial(_mlstm_kernel, heads=heads, dk=dk, dv=dv, t_meta=N_META, t_chunk=t),
        grid=(nc + 1,),
        in_specs=[
            pl.BlockSpec((t, qkvo.shape[1]), lambda c: (blk(c), 0)),
            pl.BlockSpec((t, V7X_LANES), lambda c: (blk(c), 0)),
            pl.BlockSpec((2 * heads, t), lambda c: (0, blk(c))),
            pl.BlockSpec((1, v_w), lambda c: (0, 0)),
        ],
        out_specs=pl.BlockSpec((t, v_w), lambda c: (blk(c), 0)),
        out_shape=jax.ShapeDtypeStruct((rows, v_w), BF16),
        scratch_shapes=[pltpu.VMEM((heads, dk, dv), F32), pltpu.VMEM((heads, dk), F32),
                        pltpu.VMEM((heads, V7X_LANES), F32)],
        compiler_params=_params("arbitrary"),
        name="mlstm_recurrence",
    )(qkvo, gates, gates_t, norm_g)


def _proj_res_kernel(a_ref, w_ref, g_ref, h_ref, o_ref, *, a_transposed):
    contract_a = 0 if a_transposed else 1
    y = lax.dot_general(a_ref[...], w_ref[...], (((contract_a,), (0,)), ((), ())), preferred_element_type=F32)
    o_ref[...] = h_ref[...] + _rms(y, g_ref[...])


def _proj_res(a, w, g, h, rows, tm, a_transposed=False):
    k, d = w.shape
    a_spec = pl.BlockSpec((k, tm), lambda i: (0, i)) if a_transposed else pl.BlockSpec((tm, k), lambda i: (i, 0))
    return pl.pallas_call(
        functools.partial(_proj_res_kernel, a_transposed=a_transposed),
        grid=(rows // tm,),
        in_specs=[
            a_spec,
            pl.BlockSpec((k, d), lambda i: (0, 0)),
            pl.BlockSpec((1, d), lambda i: (0, 0)),
            pl.BlockSpec((tm, d), lambda i: (i, 0)),
        ],
        out_specs=pl.BlockSpec((tm, d), lambda i: (i, 0)),
        out_shape=jax.ShapeDtypeStruct((rows, d), F32),
        compiler_params=_params("parallel"),
        name="proj_res",
    )(a, w, g, h)


def _swa_kernel(sink_ref, qt_ref, kp_ref, kc_ref, km_ref, vp_ref, vc_ref, vm_ref, bias_ref, o_ref, *, kvh, grp, hd, n_keys):
    w = qt_ref.shape[1]
    n_meta = km_ref.shape[0]
    sink_row_idx = 2 * w + n_meta
    zpad = jnp.zeros((n_keys - sink_row_idx, hd), BF16)
    bias = bias_ref[0]
    key_idx = lax.broadcasted_iota(jnp.int32, (n_keys - 2 * w, grp * w), 0) + 2 * w
    for h in range(kvh):
        base = h * grp * hd
        qt = jnp.concatenate([qt_ref[base + g * hd:base + (g + 1) * hd, :] for g in range(grp)], axis=1)
        cols = slice(h * hd, (h + 1) * hd)
        kk = jnp.concatenate([kp_ref[:, cols], kc_ref[:, cols], km_ref[:, cols], zpad], axis=0)
        vv = jnp.concatenate([vp_ref[:, cols], vc_ref[:, cols], vm_ref[:, cols], zpad], axis=0)
        s = jnp.dot(kk, qt, preferred_element_type=F32) + bias
        sink = jnp.concatenate([jnp.full((1, w), sink_ref[h * grp + g], F32) for g in range(grp)], axis=1)
        s = jnp.concatenate([s[:2 * w], jnp.where(key_idx == sink_row_idx, sink, s[2 * w:])], axis=0)
        m = jnp.max(s, axis=0, keepdims=True)
        p = jnp.exp(s - m)
        denom = jnp.sum(p, axis=0, keepdims=True)
        o = lax.dot_general(vv, p.astype(BF16), (((0,), (0,)), ((), ())), preferred_element_type=F32) / denom
        for g in range(grp):
            o_ref[base + g * hd:base + (g + 1) * hd, :] = o[:, g * w:(g + 1) * w].astype(o_ref.dtype)


def _swa_bias(n_keys):
    qi = np.arange(WINDOW)[None, :]
    r = np.arange(n_keys)[:, None]
    band = (r > qi) & (r <= qi + WINDOW) & (r < 2 * WINDOW)
    meta = (r >= 2 * WINDOW) & (r < 2 * WINDOW + N_META)
    later = band | meta
    first = (band & (r >= WINDOW)) | meta
    both = np.stack([first, later])
    both = np.where(both, 0.0, MASK_NEG).astype(np.float32)
    return jnp.asarray(np.tile(both, (1, 1, SWA_GROUP)))


def _swa(q_t, kv, sinks, rows):
    d = q_t.shape[0]
    hd = d // SWA_HEADS
    kv_w = SWA_KV_HEADS * hd
    nb = rows // WINDOW
    n_keys = -(-(2 * WINDOW + N_META + 1) // BF16_SUBLANES) * BF16_SUBLANES
    bias = _swa_bias(n_keys)
    meta_blk = rows // N_META
    prev = lambda n: jnp.maximum(n - 1, 0)
    return pl.pallas_call(
        functools.partial(_swa_kernel, kvh=SWA_KV_HEADS, grp=SWA_GROUP, hd=hd, n_keys=n_keys),
        grid=(nb,),
        in_specs=[
            pl.BlockSpec(memory_space=pltpu.SMEM),
            pl.BlockSpec((d, WINDOW), lambda n: (0, n)),
            pl.BlockSpec((WINDOW, kv_w), lambda n: (prev(n), 0)),
            pl.BlockSpec((WINDOW, kv_w), lambda n: (n, 0)),
            pl.BlockSpec((N_META, kv_w), lambda n: (meta_blk, 0)),
            pl.BlockSpec((WINDOW, kv_w), lambda n: (prev(n), 1)),
            pl.BlockSpec((WINDOW, kv_w), lambda n: (n, 1)),
            pl.BlockSpec((N_META, kv_w), lambda n: (meta_blk, 1)),
            pl.BlockSpec((1,) + bias.shape[1:], lambda n: (jnp.minimum(n, 1), 0, 0)),
        ],
        out_specs=pl.BlockSpec((d, WINDOW), lambda n: (0, n)),
        out_shape=jax.ShapeDtypeStruct((d, rows), BF16),
        compiler_params=_params("parallel"),
        name="swa_attention",
    )(sinks, q_t, kv, kv, kv, kv, kv, kv, bias)


def kernel(x, meta_tokens, norm_pre, norm_post, ffn_w_in, ffn_w_out, mlstm_w_in, mlstm_gate_bias, mlstm_norm_g,
           mlstm_w_out, kv_norm_g, w_kv, swa_w_q, swa_sinks, swa_w_o):
    batch, seq, d = x.shape
    assert batch == 1 and meta_tokens.shape[0] == N_META and seq % MLSTM_CHUNK == 0 and seq % WINDOW == 0
    depth = norm_pre.shape[0]
    n_a = mlstm_w_in.shape[0]
    heads = MLSTM_HEADS
    gain = lambda g: g.reshape(1, -1)
    ffn = lambda hh, l, k, rows: _ffn(hh, gain(norm_pre[l, 2 * k]), gain(norm_post[l, 2 * k]),
                                     ffn_w_in, ffn_w_out, l, k, rows, _row_tile(rows, FFN_TM_CAP))

    h = jnp.concatenate([x[0], meta_tokens.astype(x.dtype)], axis=0)
    rows = seq + N_META
    tm, tm_proj = _row_tile(rows, 512), _row_tile(rows, 1024)
    kv = None
    for l in range(depth):
        h = ffn(h, l, 0, rows)
        if l < n_a:
            n_main = mlstm_w_in.shape[2] - 2 * heads
            w_gates = jnp.pad(mlstm_w_in[l, :, n_main:], ((0, 0), (0, V7X_LANES - 2 * heads))).astype(BF16)
            bias = jnp.pad(mlstm_gate_bias[l], (0, V7X_LANES - 2 * heads)).reshape(1, V7X_LANES)
            g_pre = gain(norm_pre[l, 1])
            qkvo = _norm_matmul(h, g_pre, mlstm_w_in, rows, tm_proj, 1024, layer=l, n=n_main)
            gates = _gates(h, g_pre, w_gates, bias, rows, tm_proj)
            n_chunks = seq // MLSTM_CHUNK + 1
            gates_t = jnp.pad(gates[:, :2 * heads].T, ((0, 0), (0, n_chunks * MLSTM_CHUNK - rows)))
            mixed = _mlstm_recurrence(qkvo, gates, gates_t, gain(mlstm_norm_g[l]), seq)
            h = _proj_res(mixed, mlstm_w_out[l].astype(BF16), gain(norm_post[l, 1]), h, rows, tm)
        else:
            j = l - n_a
            hd = d // SWA_HEADS
            q_t = _norm_matmul(h, gain(norm_pre[l, 1]), swa_w_q[j].T.astype(BF16), rows, tm_proj, 1024,
                               transposed=True, scale=hd ** -0.5)
            att_t = _swa(q_t, kv, swa_sinks[j], rows)
            h = _proj_res(att_t, swa_w_o[j].astype(BF16), gain(norm_post[l, 1]), h, rows, tm, a_transposed=True)
        if l == n_a - 1:
            h = ffn(h, l, 1, rows)
            kv =_norm_matmul(h, gain(kv_norm_g), w_kv.astype(BF16), rows, tm_proj, w_kv.shape[1])
            rows = seq
            tm, tm_proj = _row_tile(rows, 512), _row_tile(rows, 1024)
        else:
            h = ffn(h, l, 1, rows)
    return h.reshape(batch, seq, d)
```

```python
import functools

import numpy as np
import jax
import jax.numpy as jnp
from jax import lax
from jax.experimental import pallas as pl
from jax.experimental.pallas import tpu as pltpu

F32 = jnp.float32
BF16 = jnp.bfloat16

RMS_EPS = 1e-6
N_META = 16
MLSTM_HEADS = 8
SWA_HEADS = 32
SWA_KV_HEADS = 4
SWA_GROUP = SWA_HEADS // SWA_KV_HEADS
WINDOW = 128

V7X_LANES = 128
BF16_SUBLANES = 16
V7X_VMEM_BYTES = 64 * 1024 * 1024
VMEM_LIMIT = V7X_VMEM_BYTES * 3 // 4

FFN_TF = 256
FFN_TM_CAP = 1024
FFN_ROW_CHUNK = 320
MLSTM_CHUNK = 256
MASK_NEG = -1e30


def _row_tile(rows, cap):
    best = None
    for t in range(BF16_SUBLANES, cap + 1, BF16_SUBLANES):
        if rows % t == 0:
            best = t
    assert best is not None, (rows, cap)
    return best


def _params(*sem, vmem=VMEM_LIMIT):
    return pltpu.CompilerParams(dimension_semantics=sem, vmem_limit_bytes=vmem)


def _rms(x, g):
    return x * lax.rsqrt(jnp.mean(x * x, axis=-1, keepdims=True) + RMS_EPS) * g


def _ffn_step(first, last, h_ref, gpre_ref, gpost_ref, wg_ref, wu_ref, wo_ref, o_ref, xn_ref, *, tf, nf, dff, rc,
              chunk_done=None):
    tm = h_ref.shape[0]
    chunks = [slice(r, r + rc) for r in range(0, tm, rc)]
    if first:
        for rows in chunks:
            xn_ref[rows, :] = _rms(h_ref[rows, :], gpre_ref[...]).astype(BF16)
    xn = xn_ref[...]
    g = jnp.dot(xn, wg_ref[...].astype(BF16), preferred_element_type=F32)
    u = jnp.dot(xn, wu_ref[...].astype(BF16), preferred_element_type=F32)
    a = g * jax.nn.sigmoid(g) * u
    wo = wo_ref[...].astype(BF16)
    if not last:
        y = jnp.dot(a.astype(BF16), wo, preferred_element_type=F32)
        if first:
            o_ref[...] = y
        else:
            o_ref[...] += y
        return
    first_new = (nf - 1) * tf - (dff - tf)
    col = lax.broadcasted_iota(jnp.int32, a.shape, 1)
    a = jnp.where(col >= first_new, a, 0.0).astype(BF16)
    for k, rows in enumerate(chunks):
        y = o_ref[rows, :] + jnp.dot(a[rows, :], wo, preferred_element_type=F32)
        o_ref[rows, :] = h_ref[rows, :] + 0.5 * _rms(y, gpost_ref[...])
        if chunk_done is not None:
            chunk_done(k, rows.start)


def _ffn_kernel(*refs, nf, rc, n_blocks, cast_slabs, **kw):
    h_hbm, gpre_ref, gpost_ref, wg_ref, wu_ref, wo_ref = refs[:6]
    if cast_slabs is None:
        o_hbm, xn_ref, acc_ref, hbuf_ref, sem_h, sem_o = refs[6:]
    else:
        cin_ref, cout_ref, o_hbm, cin_o_ref, cout_o_ref, xn_ref, acc_ref, hbuf_ref, sem_h, sem_o = refs[6:]
    i = pl.program_id(0)
    j = pl.program_id(1)
    tm = acc_ref.shape[0]
    assert nf >= 3
    slot = i % 2
    starts = list(range(0, tm, rc))

    if cast_slabs is not None:
        flat = i * nf + j

        @pl.when(flat < cast_slabs[0])
        def _():
            cin_o_ref[...] = cin_ref[...].astype(BF16)

        @pl.when(flat < cast_slabs[1])
        def _():
            cout_o_ref[...] = cout_ref[...].astype(BF16)

    def h_copy(block, s):
        return pltpu.make_async_copy(h_hbm.at[pl.ds(pl.multiple_of(block * tm, 8), tm), :], hbuf_ref.at[s], sem_h.at[s])

    def o_copy(block, k, r):
        dst = o_hbm.at[pl.ds(pl.multiple_of(block * tm + r, 8), rc), :]
        return pltpu.make_async_copy(acc_ref.at[pl.ds(r, rc), :], dst, sem_o.at[k])

    step = functools.partial(_ffn_step, h_ref=hbuf_ref.at[slot], gpre_ref=gpre_ref, gpost_ref=gpost_ref, wg_ref=wg_ref,
                             wu_ref=wu_ref, wo_ref=wo_ref, o_ref=acc_ref, xn_ref=xn_ref, nf=nf, rc=rc, **kw)

    @pl.when(j == 0)
    def _():
        @pl.when(i == 0)
        def _():
            h_copy(0, 0).start()

        h_copy(i, slot).wait()

        @pl.when(i > 0)
        def _():
            for k, r in enumerate(starts):
                o_copy(i - 1, k, r).wait()

        step(True, False)

    @pl.when((j > 0) & (j < nf - 1))
    def _():
        @pl.when((j == 1) & (i + 1 < n_blocks))
        def _():
            h_copy(i + 1, 1 - slot).start()

        step(False, False)

    @pl.when(j == nf - 1)
    def _():
        step(False, True, chunk_done=lambda k, r: o_copy(i, k, r).start())

        @pl.when(i == n_blocks - 1)
        def _():
            for k, r in enumerate(starts):
                o_copy(i, k, r).wait()


def _slab_rows(total, max_slabs):
    for r in range(BF16_SUBLANES, total + 1, BF16_SUBLANES):
        if total % r == 0 and total // r <= max_slabs:
            return r
    raise ValueError((total, max_slabs))


def _ffn(h, g_pre, g_post, w_in, w_out, rows, tm, cast_next=None):
    d = h.shape[1]
    dff = w_out.shape[0]
    tf = FFN_TF
    nf = -(-dff // tf)
    n_blocks = rows // tm
    assert nf >= 3 and dff % V7X_LANES == 0 and tf % V7X_LANES == 0
    off = lambda j, base=0: (jnp.minimum(j * (tf // V7X_LANES), (dff - tf) // V7X_LANES) + base // V7X_LANES) * V7X_LANES
    wbytes = w_in.dtype.itemsize
    vmem = 3 * tm * d * 4 + tm * d * 2 + 2 * 3 * d * tf * wbytes + 4 * tm * tf * 4
    rc = _row_tile(tm, FFN_ROW_CHUNK)
    in_specs = [
        pl.BlockSpec(memory_space=pl.ANY),
        pl.BlockSpec((1, d), lambda i, j: (0, 0)),
        pl.BlockSpec((1, d), lambda i, j: (0, 0)),
        pl.BlockSpec((pl.Element(d), pl.Element(tf)), lambda i, j: (0, off(j))),
        pl.BlockSpec((pl.Element(d), pl.Element(tf)), lambda i, j: (0, off(j, dff))),
        pl.BlockSpec((pl.Element(tf), pl.Element(d)), lambda i, j: (off(j), 0)),
    ]
    out_specs = [pl.BlockSpec(memory_space=pl.ANY)]
    out_shape = [jax.ShapeDtypeStruct((rows, d), F32)]
    operands = [h, g_pre, g_post, w_in, w_in, w_out]
    cast_slabs = None
    if cast_next is not None:
        src_in, src_out, layer, half = cast_next
        steps = n_blocks * nf
        r_in, r_out = _slab_rows(d, steps), _slab_rows(dff, steps)
        cast_slabs = (d // r_in, dff // r_out)
        slab = lambda n: (lambda i, j: jnp.minimum(i * nf + j, n - 1))
        s_in, s_out = slab(cast_slabs[0]), slab(cast_slabs[1])
        in_specs += [pl.BlockSpec((None, None, r_in, 2 * dff), lambda i, j: (layer, half, s_in(i, j), 0)),
                     pl.BlockSpec((None, None, r_out, d), lambda i, j: (layer, half, s_out(i, j), 0))]
        out_specs += [pl.BlockSpec((r_in, 2 * dff), lambda i, j: (s_in(i, j), 0)),
                      pl.BlockSpec((r_out, d), lambda i, j: (s_out(i, j), 0))]
        out_shape += [jax.ShapeDtypeStruct((d, 2 * dff), BF16), jax.ShapeDtypeStruct((dff, d), BF16)]
        operands += [src_in, src_out]
        vmem += 2 * 6 * (r_in * 2 * dff + r_out * d)
    vmem = min(vmem * 5 // 4, V7X_VMEM_BYTES * 7 // 8)
    res = pl.pallas_call(
        functools.partial(_ffn_kernel, tf=tf, nf=nf, dff=dff, rc=rc, n_blocks=n_blocks, cast_slabs=cast_slabs),
        grid=(n_blocks, nf),
        in_specs=in_specs,
        out_specs=out_specs,
        out_shape=out_shape,
        scratch_shapes=[pltpu.VMEM((tm, d), BF16), pltpu.VMEM((tm, d), F32), pltpu.VMEM((2, tm, d), F32),
                        pltpu.SemaphoreType.DMA((2,)), pltpu.SemaphoreType.DMA((tm // rc,))],
        compiler_params=_params("arbitrary", "arbitrary", vmem=vmem),
        name="ffn",
    )(*operands)
    return res if cast_next is not None else (res[0], None, None)


def _norm_matmul_step(first, h_ref, g_ref, w_ref, o_ref, xn_ref, *, transposed, scale, rc):
    if first:
        for r in range(0, h_ref.shape[0], rc):
            xn_ref[r:r + rc, :] = _rms(h_ref[r:r + rc, :], g_ref[...]).astype(BF16)
    if transposed:
        y = lax.dot_general(w_ref[...], xn_ref[...], (((1,), (1,)), ((), ())), preferred_element_type=F32)
    else:
        y = jnp.dot(xn_ref[...], w_ref[...].astype(BF16), preferred_element_type=F32)
    o_ref[...] = (y if scale == 1.0 else y * scale).astype(o_ref.dtype)


def _norm_matmul_kernel(*refs, **kw):
    j = pl.program_id(1)
    pl.when(j == 0)(functools.partial(_norm_matmul_step, True, *refs, **kw))
    pl.when(j > 0)(functools.partial(_norm_matmul_step, False, *refs, **kw))


def _norm_matmul(h, g, w, rows, tm, tn, transposed=False, scale=1.0, layer=None, n=None):
    d = h.shape[1]
    if layer is not None:
        assert not transposed and n % tn == 0
        w_spec = pl.BlockSpec((None, d, tn), lambda i, j: (layer, 0, j))
        o_spec = pl.BlockSpec((tm, tn), lambda i, j: (i, j))
        o_shape = (rows, n)
    elif transposed:
        n = w.shape[0]
        w_spec = pl.BlockSpec((tn, d), lambda i, j: (j, 0))
        o_spec = pl.BlockSpec((tn, tm), lambda i, j: (j, i))
        o_shape = (n, rows)
    else:
        n = w.shape[1]
        w_spec = pl.BlockSpec((d, tn), lambda i, j: (0, j))
        o_spec = pl.BlockSpec((tm, tn), lambda i, j: (i, j))
        o_shape = (rows, n)
    return pl.pallas_call(
        functools.partial(_norm_matmul_kernel, transposed=transposed, scale=scale, rc=_row_tile(tm, FFN_ROW_CHUNK)),
        grid=(rows // tm, n // tn),
        in_specs=[
            pl.BlockSpec((tm, d), lambda i, j: (i, 0)),
            pl.BlockSpec((1, d), lambda i, j: (0, 0)),
            w_spec,
        ],
        out_specs=o_spec,
        out_shape=jax.ShapeDtypeStruct(o_shape, BF16),
        scratch_shapes=[pltpu.VMEM((tm, d), BF16)],
        compiler_params=_params("parallel", "arbitrary"),
        name="norm_matmul",
    )(h, g, w)


def _gates_kernel(h_ref, g_ref, w_ref, b_ref, o_ref, *, heads):
    xn = _rms(h_ref[...], g_ref[...]).astype(BF16)
    z = jnp.dot(xn, w_ref[...], preferred_element_type=F32) + b_ref[...]
    log_sig = jnp.minimum(z, 0.0) - jnp.log1p(jnp.exp(-jnp.abs(z)))
    col = lax.broadcasted_iota(jnp.int32, z.shape, 1)
    o_ref[...] = jnp.where(col >= heads, log_sig, z)


def _gates(h, g, w_gates, bias, rows, tm):
    d = h.shape[1]
    return pl.pallas_call(
        functools.partial(_gates_kernel, heads=MLSTM_HEADS),
        grid=(rows // tm,),
        in_specs=[
            pl.BlockSpec((tm, d), lambda i: (i, 0)),
            pl.BlockSpec((1, d), lambda i: (0, 0)),
            pl.BlockSpec((d, V7X_LANES), lambda i: (0, 0)),
            pl.BlockSpec((1, V7X_LANES), lambda i: (0, 0)),
        ],
        out_specs=pl.BlockSpec((tm, V7X_LANES), lambda i: (i, 0)),
        out_shape=jax.ShapeDtypeStruct((rows, V7X_LANES), F32),
        compiler_params=_params("parallel"),
        name="mlstm_gates",
    )(h, g, w_gates, bias)


def _mlstm_head(hd, tc, qkvo_ref, gc_ref, gr_ref, ng_ref, o_ref, c_ref, n_ref, m_ref, *, heads, dk, dv):
    qk_w = heads * dk
    v_w = heads * dv
    q = qkvo_ref[0:tc, hd * dk:(hd + 1) * dk]
    k = qkvo_ref[0:tc, qk_w + hd * dk:qk_w + (hd + 1) * dk]
    v = qkvo_ref[0:tc, 2 * qk_w + hd * dv:2 * qk_w + (hd + 1) * dv]
    og = qkvo_ref[0:tc, 2 * qk_w + v_w + hd * dv:2 * qk_w + v_w + (hd + 1) * dv]
    li_col = gc_ref[0:tc, hd:hd + 1]
    lf_col = gc_ref[0:tc, heads + hd:heads + hd + 1]
    li_row = gr_ref[hd:hd + 1, 0:tc]
    lf_row = gr_ref[heads + hd:heads + hd + 1, 0:tc]
    scale = dk ** -0.5

    t_idx = lax.broadcasted_iota(jnp.int32, (tc, tc), 0)
    s_idx = lax.broadcasted_iota(jnp.int32, (tc, tc), 1)
    causal = s_idx <= t_idx
    b_col = jnp.sum(jnp.where(causal, lf_row, 0.0), axis=1, keepdims=True)
    b_row = jnp.sum(jnp.where(t_idx <= s_idx, lf_col, 0.0), axis=0, keepdims=True)

    m_prev = m_ref[hd:hd + 1, 0:1]
    d_intra = jnp.where(causal, b_col - b_row + li_row, -jnp.inf)
    d_inter = b_col + m_prev
    m_t = jnp.maximum(d_inter, jnp.max(d_intra, axis=1, keepdims=True))
    w_intra = jnp.exp(d_intra - m_t)
    w_inter = jnp.exp(d_inter - m_t)

    s = lax.dot_general(q, k, (((1,), (1,)), ((), ())), preferred_element_type=F32) * scale * w_intra
    c_prev = c_ref[hd]
    n_prev = n_ref[hd:hd + 1, :]
    q_c = jnp.dot(q, c_prev.astype(BF16), preferred_element_type=F32) * scale
    num = w_inter * q_c + jnp.dot(s.astype(BF16), v, preferred_element_type=F32)
    q_n = jnp.sum(q.astype(F32) * n_prev, axis=1, keepdims=True) * scale
    den = w_inter * q_n + jnp.sum(s, axis=1, keepdims=True)
    hh = num / jnp.maximum(jnp.abs(den), jnp.exp(-m_t))
    hh = hh * lax.rsqrt(jnp.mean(hh * hh, axis=1, keepdims=True) + RMS_EPS)
    hh = hh * ng_ref[:, hd * dv:(hd + 1) * dv] * jax.nn.sigmoid(og.astype(F32))
    o_ref[0:tc, hd * dv:(hd + 1) * dv] = hh.astype(o_ref.dtype)

    b_last = b_col[tc - 1:tc, :]
    d_state = b_last - b_col + li_col
    m_new = jnp.maximum(b_last + m_prev, jnp.max(d_state, axis=0, keepdims=True))
    w_s = jnp.exp(d_state - m_new)
    decay = jnp.exp(b_last + m_prev - m_new)
    vw = (v.astype(F32) * w_s).astype(BF16)
    c_ref[hd] = decay * c_prev + lax.dot_general(k, vw, (((0,), (0,)), ((), ())), preferred_element_type=F32)
    n_ref[hd:hd + 1, :] = decay * n_prev + jnp.sum(k.astype(F32) * w_s, axis=0, keepdims=True)
    m_ref[hd:hd + 1, :] = jnp.broadcast_to(m_new, (1, m_ref.shape[1]))


def _mlstm_kernel(qkvo_ref, gc_ref, gr_ref, ng_ref, o_ref, c_ref, n_ref, m_ref, *, heads, dk, dv, t_meta, t_chunk):
    step = pl.program_id(0)

    def chunk(tc):
        for hd in range(heads):
            _mlstm_head(hd, tc, qkvo_ref, gc_ref, gr_ref, ng_ref, o_ref, c_ref, n_ref, m_ref,
                        heads=heads, dk=dk, dv=dv)

    @pl.when(step == 0)
    def _():
        c_ref[...] = jnp.zeros_like(c_ref)
        n_ref[...] = jnp.zeros_like(n_ref)
        m_ref[...] = jnp.zeros_like(m_ref)
        o_ref[...] = jnp.zeros_like(o_ref)
        chunk(t_meta)

    @pl.when(step > 0)
    def _():
        chunk(t_chunk)


def _mlstm_recurrence(qkvo, gates, gates_t, norm_g, rows_real):
    heads = MLSTM_HEADS
    rows = qkvo.shape[0]
    v_w = norm_g.shape[1]
    dv = v_w // heads
    dk = (qkvo.shape[1] - 2 * v_w) // (2 * heads)
    t = MLSTM_CHUNK
    nc = rows_real // t
    assert rows_real % t == 0 and rows - rows_real == N_META and gates_t.shape[1] == (nc + 1) * t
    blk = lambda c: (c + nc) % (nc + 1)
    return pl.pallas_call(
        functools.partial(_mlstm_kernel, heads=heads, dk=dk, dv=dv, t_meta=N_META, t_chunk=t),
        grid=(nc + 1,),
        in_specs=[
            pl.BlockSpec((t, qkvo.shape[1]), lambda c: (blk(c), 0)),
            pl.BlockSpec((t, V7X_LANES), lambda c: (blk(c), 0)),
            pl.BlockSpec((2 * heads, t), lambda c: (0, blk(c))),
            pl.BlockSpec((1, v_w), lambda c: (0, 0)),
        ],
        out_specs=pl.BlockSpec((t, v_w), lambda c: (blk(c), 0)),
        out_shape=jax.ShapeDtypeStruct((rows, v_w), BF16),
        scratch_shapes=[pltpu.VMEM((heads, dk, dv), F32), pltpu.VMEM((heads, dk), F32),
                        pltpu.VMEM((heads, V7X_LANES), F32)],
        compiler_params=_params("arbitrary"),
        name="mlstm_recurrence",
    )(qkvo, gates, gates_t, norm_g)


def _proj_res_kernel(a_ref, w_ref, g_ref, h_ref, o_ref, *, a_transposed):
    contract_a = 0 if a_transposed else 1
    y = lax.dot_general(a_ref[...], w_ref[...], (((contract_a,), (0,)), ((), ())), preferred_element_type=F32)
    o_ref[...] = h_ref[...] + _rms(y, g_ref[...])


def _proj_res(a, w, g, h, rows, tm, a_transposed=False):
    k, d = w.shape
    a_spec = pl.BlockSpec((k, tm), lambda i: (0, i)) if a_transposed else pl.BlockSpec((tm, k), lambda i: (i, 0))
    return pl.pallas_call(
        functools.partial(_proj_res_kernel, a_transposed=a_transposed),
        grid=(rows // tm,),
        in_specs=[
            a_spec,
            pl.BlockSpec((k, d), lambda i: (0, 0)),
            pl.BlockSpec((1, d), lambda i: (0, 0)),
            pl.BlockSpec((tm, d), lambda i: (i, 0)),
        ],
        out_specs=pl.BlockSpec((tm, d), lambda i: (i, 0)),
        out_shape=jax.ShapeDtypeStruct((rows, d), F32),
        compiler_params=_params("parallel"),
        name="proj_res",
    )(a, w, g, h)


def _swa_kernel(sink_ref, qt_ref, kp_ref, kc_ref, km_ref, vp_ref, vc_ref, vm_ref, bias_ref, o_ref, *, kvh, grp, hd, n_keys):
    w = qt_ref.shape[1]
    n_meta = km_ref.shape[0]
    sink_row_idx = 2 * w + n_meta
    zpad = jnp.zeros((n_keys - sink_row_idx, hd), BF16)
    bias = bias_ref[0]
    key_idx = lax.broadcasted_iota(jnp.int32, (n_keys - 2 * w, grp * w), 0) + 2 * w
    for h in range(kvh):
        base = h * grp * hd
        qt = jnp.concatenate([qt_ref[base + g * hd:base + (g + 1) * hd, :] for g in range(grp)], axis=1)
        cols = slice(h * hd, (h + 1) * hd)
        kk = jnp.concatenate([kp_ref[:, cols], kc_ref[:, cols], km_ref[:, cols], zpad], axis=0)
        vv = jnp.concatenate([vp_ref[:, cols], vc_ref[:, cols], vm_ref[:, cols], zpad], axis=0)
        s = jnp.dot(kk, qt, preferred_element_type=F32) + bias
        sink = jnp.concatenate([jnp.full((1, w), sink_ref[h * grp + g], F32) for g in range(grp)], axis=1)
        s = jnp.concatenate([s[:2 * w], jnp.where(key_idx == sink_row_idx, sink, s[2 * w:])], axis=0)
        m = jnp.max(s, axis=0, keepdims=True)
        p = jnp.exp(s - m)
        denom = jnp.sum(p, axis=0, keepdims=True)
        o = lax.dot_general(vv, p.astype(BF16), (((0,), (0,)), ((), ())), preferred_element_type=F32) / denom
        for g in range(grp):
            o_ref[base + g * hd:base + (g + 1) * hd, :] = o[:, g * w:(g + 1) * w].astype(o_ref.dtype)


def _swa_bias(n_keys):
    qi = np.arange(WINDOW)[None, :]
    r = np.arange(n_keys)[:, None]
    band = (r > qi) & (r <= qi + WINDOW) & (r < 2 * WINDOW)
    meta = (r >= 2 * WINDOW) & (r < 2 * WINDOW + N_META)
    later = band | meta
    first = (band & (r >= WINDOW)) | meta
    both = np.stack([first, later])
    both = np.where(both, 0.0, MASK_NEG).astype(np.float32)
    return jnp.asarray(np.tile(both, (1, 1, SWA_GROUP)))


def _swa(q_t, kv, sinks, rows):
    d = q_t.shape[0]
    hd = d // SWA_HEADS
    kv_w = SWA_KV_HEADS * hd
    nb = rows // WINDOW
    n_keys = -(-(2 * WINDOW + N_META + 1) // BF16_SUBLANES) * BF16_SUBLANES
    bias = _swa_bias(n_keys)
    meta_blk = rows // N_META
    prev = lambda n: jnp.maximum(n - 1, 0)
    return pl.pallas_call(
        functools.partial(_swa_kernel, kvh=SWA_KV_HEADS, grp=SWA_GROUP, hd=hd, n_keys=n_keys),
        grid=(nb,),
        in_specs=[
            pl.BlockSpec(memory_space=pltpu.SMEM),
            pl.BlockSpec((d, WINDOW), lambda n: (0, n)),
            pl.BlockSpec((WINDOW, kv_w), lambda n: (prev(n), 0)),
            pl.BlockSpec((WINDOW, kv_w), lambda n: (n, 0)),
            pl.BlockSpec((N_META, kv_w), lambda n: (meta_blk, 0)),
            pl.BlockSpec((WINDOW, kv_w), lambda n: (prev(n), 1)),
            pl.BlockSpec((WINDOW, kv_w), lambda n: (n, 1)),
            pl.BlockSpec((N_META, kv_w), lambda n: (meta_blk, 1)),
            pl.BlockSpec((1,) + bias.shape[1:], lambda n: (jnp.minimum(n, 1), 0, 0)),
        ],
        out_specs=pl.BlockSpec((d, WINDOW), lambda n: (0, n)),
        out_shape=jax.ShapeDtypeStruct((d, rows), BF16),
        compiler_params=_params("parallel"),
        name="swa_attention",
    )(sinks, q_t, kv, kv, kv, kv, kv, kv, bias)


def kernel(x, meta_tokens, norm_pre, norm_post, ffn_w_in, ffn_w_out, mlstm_w_in, mlstm_gate_bias, mlstm_norm_g,
           mlstm_w_out, kv_norm_g, w_kv, swa_w_q, swa_sinks, swa_w_o):
    batch, seq, d = x.shape
    assert batch == 1 and meta_tokens.shape[0] == N_META and seq % MLSTM_CHUNK == 0 and seq % WINDOW == 0
    depth = norm_pre.shape[0]
    n_a = mlstm_w_in.shape[0]
    heads = MLSTM_HEADS
    gain = lambda g: g.reshape(1, -1)
    w_ffn = [ffn_w_in[0, 0].astype(BF16), ffn_w_out[0, 0].astype(BF16)]

    def ffn(hh, l, k, rows):
        nxt = (l, k + 1) if k == 0 else (l + 1, 0)
        cast_next = (ffn_w_in, ffn_w_out) + nxt if nxt[0] < depth else None
        out, w_ffn[0], w_ffn[1] = _ffn(hh, gain(norm_pre[l, 2 * k]), gain(norm_post[l, 2 * k]), w_ffn[0], w_ffn[1],
                                       rows, _row_tile(rows, FFN_TM_CAP), cast_next)
        return out

    h = jnp.concatenate([x[0], meta_tokens.astype(x.dtype)], axis=0)
    rows = seq + N_META
    tm, tm_proj = _row_tile(rows, 512), _row_tile(rows, 1024)
    kv = None
    for l in range(depth):
        h = ffn(h, l, 0, rows)
        if l < n_a:
            n_main = mlstm_w_in.shape[2] - 2 * heads
            w_gates = jnp.pad(mlstm_w_in[l, :, n_main:], ((0, 0), (0, V7X_LANES - 2 * heads))).astype(BF16)
            bias = jnp.pad(mlstm_gate_bias[l], (0, V7X_LANES - 2 * heads)).reshape(1, V7X_LANES)
            g_pre = gain(norm_pre[l, 1])
            qkvo = _norm_matmul(h, g_pre, mlstm_w_in[l, :, :n_main].astype(BF16), rows, tm_proj, 1024)
            gates = _gates(h, g_pre, w_gates, bias, rows, tm_proj)
            n_chunks = seq // MLSTM_CHUNK + 1
            gates_t = jnp.pad(gates[:, :2 * heads].T, ((0, 0), (0, n_chunks * MLSTM_CHUNK - rows)))
            mixed = _mlstm_recurrence(qkvo, gates, gates_t, gain(mlstm_norm_g[l]), seq)
            h = _proj_res(mixed, mlstm_w_out[l].astype(BF16), gain(norm_post[l, 1]), h, rows, tm)
        else:
            j = l - n_a
            hd = d // SWA_HEADS
            q_t = _norm_matmul(h, gain(norm_pre[l, 1]), swa_w_q[j].T.astype(BF16), rows, tm_proj, 1024,
                               transposed=True, scale=hd ** -0.5)
            att_t = _swa(q_t, kv, swa_sinks[j], rows)
            h = _proj_res(att_t, swa_w_o[j].astype(BF16), gain(norm_post[l, 1]), h, rows, tm, a_transposed=True)
        if l == n_a - 1:
            h = ffn(h, l, 1, rows)
            kv =_norm_matmul(h, gain(kv_norm_g), w_kv.astype(BF16), rows, tm_proj, w_kv.shape[1])
            rows = seq
            tm, tm_proj = _row_tile(rows, 512), _row_tile(rows, 1024)
        else:
            h = ffn(h, l, 1, rows)
    return h.reshape(batch, seq, d)
```

```python
import functools

import numpy as np
import jax
import jax.numpy as jnp
from jax import lax
from jax.experimental import pallas as pl
from jax.experimental.pallas import tpu as pltpu

F32 = jnp.float32
BF16 = jnp.bfloat16

RMS_EPS = 1e-6
N_META = 16
MLSTM_HEADS = 8
SWA_HEADS = 32
SWA_KV_HEADS = 4
SWA_GROUP = SWA_HEADS // SWA_KV_HEADS
WINDOW = 128

V7X_LANES = 128
BF16_SUBLANES = 16
V7X_VMEM_BYTES = 64 * 1024 * 1024
VMEM_LIMIT = V7X_VMEM_BYTES * 3 // 4

FFN_TF = 256
FFN_TM_CAP = 1024
FFN_ROW_CHUNK = 320
MLSTM_CHUNK = 256
MASK_NEG = -1e30


def _row_tile(rows, cap):
    best = None
    for t in range(BF16_SUBLANES, cap + 1, BF16_SUBLANES):
        if rows % t == 0:
            best = t
    assert best is not None, (rows, cap)
    return best


def _params(*sem, vmem=VMEM_LIMIT):
    return pltpu.CompilerParams(dimension_semantics=sem, vmem_limit_bytes=vmem)


def _rms(x, g):
    return x * lax.rsqrt(jnp.mean(x * x, axis=-1, keepdims=True) + RMS_EPS) * g


def _ffn_step(first, last, h_ref, gpre_ref, gpost_ref, wg_ref, wu_ref, wo_ref, o_ref, xn_ref, *, tf, nf, dff, rc,
              chunk_done=None):
    tm = h_ref.shape[0]
    chunks = [slice(r, r + rc) for r in range(0, tm, rc)]
    if first:
        for rows in chunks:
            xn_ref[rows, :] = _rms(h_ref[rows, :], gpre_ref[...]).astype(BF16)
    xn = xn_ref[...]
    g = jnp.dot(xn, wg_ref[...].astype(BF16), preferred_element_type=F32)
    u = jnp.dot(xn, wu_ref[...].astype(BF16), preferred_element_type=F32)
    a = g * jax.nn.sigmoid(g) * u
    wo = wo_ref[...].astype(BF16)
    if not last:
        y = jnp.dot(a.astype(BF16), wo, preferred_element_type=F32)
        if first:
            o_ref[...] = y
        else:
            o_ref[...] += y
        return
    first_new = (nf - 1) * tf - (dff - tf)
    col = lax.broadcasted_iota(jnp.int32, a.shape, 1)
    a = jnp.where(col >= first_new, a, 0.0).astype(BF16)
    for k, rows in enumerate(chunks):
        y = o_ref[rows, :] + jnp.dot(a[rows, :], wo, preferred_element_type=F32)
        o_ref[rows, :] = h_ref[rows, :] + 0.5 * _rms(y, gpost_ref[...])
        if chunk_done is not None:
            chunk_done(k, rows.start)


def _ffn_kernel(h_hbm, gpre_ref, gpost_ref, win_hbm, wout_hbm, o_hbm, xn_ref, acc_ref, hbuf_ref, wg_buf, wu_buf, wo_buf,
                sem_h, sem_o, sem_w, *, layer, half, tf, nf, dff, rc, n_blocks):
    i = pl.program_id(0)
    tm = acc_ref.shape[0]
    assert nf >= 4 and nf % 2 == 0
    slot = i % 2
    starts = list(range(0, tm, rc))
    last_off = dff - tf

    def h_copy(block, s):
        return pltpu.make_async_copy(h_hbm.at[pl.ds(pl.multiple_of(block * tm, 8), tm), :], hbuf_ref.at[s], sem_h.at[s])

    def o_copy(block, k, r):
        dst = o_hbm.at[pl.ds(pl.multiple_of(block * tm + r, 8), rc), :]
        return pltpu.make_async_copy(acc_ref.at[pl.ds(r, rc), :], dst, sem_o.at[k])

    def w_copies(off, s):
        if not isinstance(off, int):
            off = pl.multiple_of(off, V7X_LANES)
        up = off + dff
        if not isinstance(up, int):
            up = pl.multiple_of(up, V7X_LANES)
        return (pltpu.make_async_copy(win_hbm.at[layer, half, :, pl.ds(off, tf)], wg_buf.at[s], sem_w.at[s, 0]),
                pltpu.make_async_copy(win_hbm.at[layer, half, :, pl.ds(up, tf)], wu_buf.at[s], sem_w.at[s, 1]),
                pltpu.make_async_copy(wout_hbm.at[layer, half, pl.ds(off, tf), :], wo_buf.at[s], sem_w.at[s, 2]))

    def start_all(copies):
        for c in copies:
            c.start()

    def wait_all(copies):
        for c in copies:
            c.wait()

    def step(first, last, s, **kw):
        _ffn_step(first, last, hbuf_ref.at[slot], gpre_ref, gpost_ref, wg_buf.at[s], wu_buf.at[s], wo_buf.at[s],
                  acc_ref, xn_ref, tf=tf, nf=nf, dff=dff, rc=rc, **kw)

    @pl.when(i == 0)
    def _():
        h_copy(0, 0).start()
        start_all(w_copies(0, 0))

    h_copy(i, slot).wait()

    @pl.when(i > 0)
    def _():
        for k, r in enumerate(starts):
            o_copy(i - 1, k, r).wait()

    wait_all(w_copies(0, 0))
    start_all(w_copies(tf, 1))
    step(True, False, 0)

    @pl.when(i + 1 < n_blocks)
    def _():
        h_copy(i + 1, 1 - slot).start()

    def middle(j, carry):
        s = j % 2
        wait_all(w_copies(j * tf, s))
        start_all(w_copies(jnp.minimum((j + 1) * tf, last_off), 1 - s))
        step(False, False, s)
        return carry

    lax.fori_loop(1, nf - 1, middle, 0)

    wait_all(w_copies(last_off, (nf - 1) % 2))

    @pl.when(i + 1 < n_blocks)
    def _():
        start_all(w_copies(0, 0))

    step(False, True, (nf - 1) % 2, chunk_done=lambda k, r: o_copy(i, k, r).start())

    @pl.when(i == n_blocks - 1)
    def _():
        for k, r in enumerate(starts):
            o_copy(i, k, r).wait()


def _ffn(h, g_pre, g_post, w_in, w_out, layer, half, rows, tm):
    d = h.shape[1]
    dff = w_out.shape[2]
    tf = FFN_TF
    nf = -(-dff // tf)
    assert dff % V7X_LANES == 0 and tf % V7X_LANES == 0 and dff >= tf
    wbytes = w_in.dtype.itemsize
    cast_tmp = 3 * d * tf * 2 if w_in.dtype != BF16 else 0
    vmem = 3 * tm * d * 4 + tm * d * 2 + 2 * 3 * d * tf * wbytes + cast_tmp + 4 * tm * tf * 4
    vmem = min(vmem * 5 // 4, V7X_VMEM_BYTES * 7 // 8)
    rc = _row_tile(tm, FFN_ROW_CHUNK)
    n_blocks = rows // tm
    return pl.pallas_call(
        functools.partial(_ffn_kernel, layer=layer, half=half, tf=tf, nf=nf, dff=dff, rc=rc, n_blocks=n_blocks),
        grid=(n_blocks,),
        in_specs=[
            pl.BlockSpec(memory_space=pl.ANY),
            pl.BlockSpec((1, d), lambda i: (0, 0)),
            pl.BlockSpec((1, d), lambda i: (0, 0)),
            pl.BlockSpec(memory_space=pl.ANY),
            pl.BlockSpec(memory_space=pl.ANY),
        ],
        out_specs=pl.BlockSpec(memory_space=pl.ANY),
        out_shape=jax.ShapeDtypeStruct((rows, d), F32),
        scratch_shapes=[pltpu.VMEM((tm, d), BF16), pltpu.VMEM((tm, d), F32), pltpu.VMEM((2, tm, d), F32),
                        pltpu.VMEM((2, d, tf), w_in.dtype), pltpu.VMEM((2, d, tf), w_in.dtype),
                        pltpu.VMEM((2, tf, d), w_out.dtype),
                        pltpu.SemaphoreType.DMA((2,)), pltpu.SemaphoreType.DMA((tm // rc,)),
                        pltpu.SemaphoreType.DMA((2, 3))],
        compiler_params=_params("arbitrary", vmem=vmem),
        name="ffn",
    )(h, g_pre, g_post, w_in, w_out)


def _norm_matmul_step(first, h_ref, g_ref, w_ref, o_ref, xn_ref, *, transposed, scale, rc):
    if first:
        for r in range(0, h_ref.shape[0], rc):
            xn_ref[r:r + rc, :] = _rms(h_ref[r:r + rc, :], g_ref[...]).astype(BF16)
    if transposed:
        y = lax.dot_general(w_ref[...], xn_ref[...], (((1,), (1,)), ((), ())), preferred_element_type=F32)
    else:
        y = jnp.dot(xn_ref[...], w_ref[...].astype(BF16), preferred_element_type=F32)
    o_ref[...] = (y if scale == 1.0 else y * scale).astype(o_ref.dtype)


def _norm_matmul_kernel(*refs, **kw):
    j = pl.program_id(1)
    pl.when(j == 0)(functools.partial(_norm_matmul_step, True, *refs, **kw))
    pl.when(j > 0)(functools.partial(_norm_matmul_step, False, *refs, **kw))


def _norm_matmul(h, g, w, rows, tm, tn, transposed=False, scale=1.0, layer=None, n=None):
    d = h.shape[1]
    if layer is not None:
        assert not transposed and n % tn == 0
        w_spec = pl.BlockSpec((None, d, tn), lambda i, j: (layer, 0, j))
        o_spec = pl.BlockSpec((tm, tn), lambda i, j: (i, j))
        o_shape = (rows, n)
    elif transposed:
        n = w.shape[0]
        w_spec = pl.BlockSpec((tn, d), lambda i, j: (j, 0))
        o_spec = pl.BlockSpec((tn, tm), lambda i, j: (j, i))
        o_shape = (n, rows)
    else:
        n = w.shape[1]
        w_spec = pl.BlockSpec((d, tn), lambda i, j: (0, j))
        o_spec = pl.BlockSpec((tm, tn), lambda i, j: (i, j))
        o_shape = (rows, n)
    return pl.pallas_call(
        functools.partial(_norm_matmul_kernel, transposed=transposed, scale=scale, rc=_row_tile(tm, FFN_ROW_CHUNK)),
        grid=(rows // tm, n // tn),
        in_specs=[
            pl.BlockSpec((tm, d), lambda i, j: (i, 0)),
            pl.BlockSpec((1, d), lambda i, j: (0, 0)),
            w_spec,
        ],
        out_specs=o_spec,
        out_shape=jax.ShapeDtypeStruct(o_shape, BF16),
        scratch_shapes=[pltpu.VMEM((tm, d), BF16)],
        compiler_params=_params("parallel", "arbitrary"),
        name="norm_matmul",
    )(h, g, w)


def _gates_kernel(h_ref, g_ref, w_ref, b_ref, o_ref, *, heads):
    xn = _rms(h_ref[...], g_ref[...]).astype(BF16)
    z = jnp.dot(xn, w_ref[...], preferred_element_type=F32) + b_ref[...]
    log_sig = jnp.minimum(z, 0.0) - jnp.log1p(jnp.exp(-jnp.abs(z)))
    col = lax.broadcasted_iota(jnp.int32, z.shape, 1)
    o_ref[...] = jnp.where(col >= heads, log_sig, z)


def _gates(h, g, w_gates, bias, rows, tm):
    d = h.shape[1]
    return pl.pallas_call(
        functools.partial(_gates_kernel, heads=MLSTM_HEADS),
        grid=(rows // tm,),
        in_specs=[
            pl.BlockSpec((tm, d), lambda i: (i, 0)),
            pl.BlockSpec((1, d), lambda i: (0, 0)),
            pl.BlockSpec((d, V7X_LANES), lambda i: (0, 0)),
            pl.BlockSpec((1, V7X_LANES), lambda i: (0, 0)),
        ],
        out_specs=pl.BlockSpec((tm, V7X_LANES), lambda i: (i, 0)),
        out_shape=jax.ShapeDtypeStruct((rows, V7X_LANES), F32),
        compiler_params=_params("parallel"),
        name="mlstm_gates",
    )(h, g, w_gates, bias)


def _mlstm_head(hd, tc, qkvo_ref, gc_ref, gr_ref, ng_ref, o_ref, c_ref, n_ref, m_ref, *, heads, dk, dv):
    qk_w = heads * dk
    v_w = heads * dv
    q = qkvo_ref[0:tc, hd * dk:(hd + 1) * dk]
    k = qkvo_ref[0:tc, qk_w + hd * dk:qk_w + (hd + 1) * dk]
    v = qkvo_ref[0:tc, 2 * qk_w + hd * dv:2 * qk_w + (hd + 1) * dv]
    og = qkvo_ref[0:tc, 2 * qk_w + v_w + hd * dv:2 * qk_w + v_w + (hd + 1) * dv]
    li_col = gc_ref[0:tc, hd:hd + 1]
    lf_col = gc_ref[0:tc, heads + hd:heads + hd + 1]
    li_row = gr_ref[hd:hd + 1, 0:tc]
    lf_row = gr_ref[heads + hd:heads + hd + 1, 0:tc]
    scale = dk ** -0.5

    t_idx = lax.broadcasted_iota(jnp.int32, (tc, tc), 0)
    s_idx = lax.broadcasted_iota(jnp.int32, (tc, tc), 1)
    causal = s_idx <= t_idx
    b_col = jnp.sum(jnp.where(causal, lf_row, 0.0), axis=1, keepdims=True)
    b_row = jnp.sum(jnp.where(t_idx <= s_idx, lf_col, 0.0), axis=0, keepdims=True)

    m_prev = m_ref[hd:hd + 1, 0:1]
    d_intra = jnp.where(causal, b_col - b_row + li_row, -jnp.inf)
    d_inter = b_col + m_prev
    m_t = jnp.maximum(d_inter, jnp.max(d_intra, axis=1, keepdims=True))
    w_intra = jnp.exp(d_intra - m_t)
    w_inter = jnp.exp(d_inter - m_t)

    s = lax.dot_general(q, k, (((1,), (1,)), ((), ())), preferred_element_type=F32) * scale * w_intra
    c_prev = c_ref[hd]
    n_prev = n_ref[hd:hd + 1, :]
    q_c = jnp.dot(q, c_prev.astype(BF16), preferred_element_type=F32) * scale
    num = w_inter * q_c + jnp.dot(s.astype(BF16), v, preferred_element_type=F32)
    q_n = jnp.sum(q.astype(F32) * n_prev, axis=1, keepdims=True) * scale
    den = w_inter * q_n + jnp.sum(s, axis=1, keepdims=True)
    hh = num / jnp.maximum(jnp.abs(den), jnp.exp(-m_t))
    hh = hh * lax.rsqrt(jnp.mean(hh * hh, axis=1, keepdims=True) + RMS_EPS)
    hh = hh * ng_ref[:, hd * dv:(hd + 1) * dv] * jax.nn.sigmoid(og.astype(F32))
    o_ref[0:tc, hd * dv:(hd + 1) * dv] = hh.astype(o_ref.dtype)

    b_last = b_col[tc - 1:tc, :]
    d_state = b_last - b_col + li_col
    m_new = jnp.maximum(b_last + m_prev, jnp.max(d_state, axis=0, keepdims=True))
    w_s = jnp.exp(d_state - m_new)
    decay = jnp.exp(b_last + m_prev - m_new)
    vw = (v.astype(F32) * w_s).astype(BF16)
    c_ref[hd] = decay * c_prev + lax.dot_general(k, vw, (((0,), (0,)), ((), ())), preferred_element_type=F32)
    n_ref[hd:hd + 1, :] = decay * n_prev + jnp.sum(k.astype(F32) * w_s, axis=0, keepdims=True)
    m_ref[hd:hd + 1, :] = jnp.broadcast_to(m_new, (1, m_ref.shape[1]))


def _mlstm_kernel(qkvo_ref, gc_ref, gr_ref, ng_ref, o_ref, c_ref, n_ref, m_ref, *, heads, dk, dv, t_meta, t_chunk):
    step = pl.program_id(0)

    def chunk(tc):
        for hd in range(heads):
            _mlstm_head(hd, tc, qkvo_ref, gc_ref, gr_ref, ng_ref, o_ref, c_ref, n_ref, m_ref,
                        heads=heads, dk=dk, dv=dv)

    @pl.when(step == 0)
    def _():
        c_ref[...] = jnp.zeros_like(c_ref)
        n_ref[...] = jnp.zeros_like(n_ref)
        m_ref[...] = jnp.zeros_like(m_ref)
        o_ref[...] = jnp.zeros_like(o_ref)
        chunk(t_meta)

    @pl.when(step > 0)
    def _():
        chunk(t_chunk)


def _mlstm_recurrence(qkvo, gates, gates_t, norm_g, rows_real):
    heads = MLSTM_HEADS
    rows = qkvo.shape[0]
    v_w = norm_g.shape[1]
    dv = v_w // heads
    dk = (qkvo.shape[1] - 2 * v_w) // (2 * heads)
    t = MLSTM_CHUNK
    nc = rows_real // t
    assert rows_real % t == 0 and rows - rows_real == N_META and gates_t.shape[1] == (nc + 1) * t
    blk = lambda c: (c + nc) % (nc + 1)
    return pl.pallas_call(
        functools.partial(_mlstm_kernel, heads=heads, dk=dk, dv=dv, t_meta=N_META, t_chunk=t),
        grid=(nc + 1,),
        in_specs=[
            pl.BlockSpec((t, qkvo.shape[1]), lambda c: (blk(c), 0)),
            pl.BlockSpec((t, V7X_LANES), lambda c: (blk(c), 0)),
            pl.BlockSpec((2 * heads, t), lambda c: (0, blk(c))),
            pl.BlockSpec((1, v_w), lambda c: (0, 0)),
        ],
        out_specs=pl.BlockSpec((t, v_w), lambda c: (blk(c), 0)),
        out_shape=jax.ShapeDtypeStruct((rows, v_w), BF16),
        scratch_shapes=[pltpu.VMEM((heads, dk, dv), F32), pltpu.VMEM((heads, dk), F32),
                        pltpu.VMEM((heads, V7X_LANES), F32)],
        compiler_params=_params("arbitrary"),
        name="mlstm_recurrence",
    )(qkvo, gates, gates_t, norm_g)


def _proj_res_kernel(a_ref, w_ref, g_ref, h_ref, o_ref, *, a_transposed):
    contract_a = 0 if a_transposed else 1
    y = lax.dot_general(a_ref[...], w_ref[...], (((contract_a,), (0,)), ((), ())), preferred_element_type=F32)
    o_ref[...] = h_ref[...] + _rms(y, g_ref[...])


def _proj_res(a, w, g, h, rows, tm, a_transposed=False):
    k, d = w.shape
    a_spec = pl.BlockSpec((k, tm), lambda i: (0, i)) if a_transposed else pl.BlockSpec((tm, k), lambda i: (i, 0))
    return pl.pallas_call(
        functools.partial(_proj_res_kernel, a_transposed=a_transposed),
        grid=(rows // tm,),
        in_specs=[
            a_spec,
            pl.BlockSpec((k, d), lambda i: (0, 0)),
            pl.BlockSpec((1, d), lambda i: (0, 0)),
            pl.BlockSpec((tm, d), lambda i: (i, 0)),
        ],
        out_specs=pl.BlockSpec((tm, d), lambda i: (i, 0)),
        out_shape=jax.ShapeDtypeStruct((rows, d), F32),
        compiler_params=_params("parallel"),
        name="proj_res",
    )(a, w, g, h)


def _swa_kernel(sink_ref, qt_ref, kp_ref, kc_ref, km_ref, vp_ref, vc_ref, vm_ref, bias_ref, o_ref, *, kvh, grp, hd, n_keys):
    w = qt_ref.shape[1]
    n_meta = km_ref.shape[0]
    sink_row_idx = 2 * w + n_meta
    zpad = jnp.zeros((n_keys - sink_row_idx, hd), BF16)
    bias = bias_ref[0]
    key_idx = lax.broadcasted_iota(jnp.int32, (n_keys - 2 * w, grp * w), 0) + 2 * w
    for h in range(kvh):
        base = h * grp * hd
        qt = jnp.concatenate([qt_ref[base + g * hd:base + (g + 1) * hd, :] for g in range(grp)], axis=1)
        cols = slice(h * hd, (h + 1) * hd)
        kk = jnp.concatenate([kp_ref[:, cols], kc_ref[:, cols], km_ref[:, cols], zpad], axis=0)
        vv = jnp.concatenate([vp_ref[:, cols], vc_ref[:, cols], vm_ref[:, cols], zpad], axis=0)
        s = jnp.dot(kk, qt, preferred_element_type=F32) + bias
        sink = jnp.concatenate([jnp.full((1, w), sink_ref[h * grp + g], F32) for g in range(grp)], axis=1)
        s = jnp.concatenate([s[:2 * w], jnp.where(key_idx == sink_row_idx, sink, s[2 * w:])], axis=0)
        m = jnp.max(s, axis=0, keepdims=True)
        p = jnp.exp(s - m)
        denom = jnp.sum(p, axis=0, keepdims=True)
        o = lax.dot_general(vv, p.astype(BF16), (((0,), (0,)), ((), ())), preferred_element_type=F32) / denom
        for g in range(grp):
            o_ref[base + g * hd:base + (g + 1) * hd, :] = o[:, g * w:(g + 1) * w].astype(o_ref.dtype)


def _swa_bias(n_keys):
    qi = np.arange(WINDOW)[None, :]
    r = np.arange(n_keys)[:, None]
    band = (r > qi) & (r <= qi + WINDOW) & (r < 2 * WINDOW)
    meta = (r >= 2 * WINDOW) & (r < 2 * WINDOW + N_META)
    later = band | meta
    first = (band & (r >= WINDOW)) | meta
    both = np.stack([first, later])
    both = np.where(both, 0.0, MASK_NEG).astype(np.float32)
    return jnp.asarray(np.tile(both, (1, 1, SWA_GROUP)))


def _swa(q_t, kv, sinks, rows):
    d = q_t.shape[0]
    hd = d // SWA_HEADS
    kv_w = SWA_KV_HEADS * hd
    nb = rows // WINDOW
    n_keys = -(-(2 * WINDOW + N_META + 1) // BF16_SUBLANES) * BF16_SUBLANES
    bias = _swa_bias(n_keys)
    meta_blk = rows // N_META
    prev = lambda n: jnp.maximum(n - 1, 0)
    return pl.pallas_call(
        functools.partial(_swa_kernel, kvh=SWA_KV_HEADS, grp=SWA_GROUP, hd=hd, n_keys=n_keys),
        grid=(nb,),
        in_specs=[
            pl.BlockSpec(memory_space=pltpu.SMEM),
            pl.BlockSpec((d, WINDOW), lambda n: (0, n)),
            pl.BlockSpec((WINDOW, kv_w), lambda n: (prev(n), 0)),
            pl.BlockSpec((WINDOW, kv_w), lambda n: (n, 0)),
            pl.BlockSpec((N_META, kv_w), lambda n: (meta_blk, 0)),
            pl.BlockSpec((WINDOW, kv_w), lambda n: (prev(n), 1)),
            pl.BlockSpec((WINDOW, kv_w), lambda n: (n, 1)),
            pl.BlockSpec((N_META, kv_w), lambda n: (meta_blk, 1)),
            pl.BlockSpec((1,) + bias.shape[1:], lambda n: (jnp.minimum(n, 1), 0, 0)),
        ],
        out_specs=pl.BlockSpec((d, WINDOW), lambda n: (0, n)),
        out_shape=jax.ShapeDtypeStruct((d, rows), BF16),
        compiler_params=_params("parallel"),
        name="swa_attention",
    )(sinks, q_t, kv, kv, kv, kv, kv, kv, bias)


def kernel(x, meta_tokens, norm_pre, norm_post, ffn_w_in, ffn_w_out, mlstm_w_in, mlstm_gate_bias, mlstm_norm_g,
           mlstm_w_out, kv_norm_g, w_kv, swa_w_q, swa_sinks, swa_w_o):
    batch, seq, d = x.shape
    assert batch == 1 and meta_tokens.shape[0] == N_META and seq % MLSTM_CHUNK == 0 and seq % WINDOW == 0
    depth = norm_pre.shape[0]
    n_a = mlstm_w_in.shape[0]
    heads = MLSTM_HEADS
    gain = lambda g: g.reshape(1, -1)
    ffn = lambda hh, l, k, rows: _ffn(hh, gain(norm_pre[l, 2 * k]), gain(norm_post[l, 2 * k]),
                                     ffn_w_in, ffn_w_out, l, k, rows, _row_tile(rows, FFN_TM_CAP))

    h = jnp.concatenate([x[0], meta_tokens.astype(x.dtype)], axis=0)
    rows = seq + N_META
    tm, tm_proj = _row_tile(rows, 512), _row_tile(rows, 1024)
    kv = None
    for l in range(depth):
        h = ffn(h, l, 0, rows)
        if l < n_a:
            n_main = mlstm_w_in.shape[2] - 2 * heads
            w_gates = jnp.pad(mlstm_w_in[l, :, n_main:], ((0, 0), (0, V7X_LANES - 2 * heads))).astype(BF16)
            bias = jnp.pad(mlstm_gate_bias[l], (0, V7X_LANES - 2 * heads)).reshape(1, V7X_LANES)
            g_pre = gain(norm_pre[l, 1])
            qkvo = _norm_matmul(h, g_pre, mlstm_w_in[l, :, :n_main].astype(BF16), rows, tm_proj, 1024)
            gates = _gates(h, g_pre, w_gates, bias, rows, tm_proj)
            n_chunks = seq // MLSTM_CHUNK + 1
            gates_t = jnp.pad(gates[:, :2 * heads].T, ((0, 0), (0, n_chunks * MLSTM_CHUNK - rows)))
            mixed = _mlstm_recurrence(qkvo, gates, gates_t, gain(mlstm_norm_g[l]), seq)
            h = _proj_res(mixed, mlstm_w_out[l].astype(BF16), gain(norm_post[l, 1]), h, rows, tm)
        else:
            j = l - n_a
            hd = d // SWA_HEADS
            q_t = _norm_matmul(h, gain(norm_pre[l, 1]), swa_w_q[j].T.astype(BF16), rows, tm_proj, 1024,
                               transposed=True, scale=hd ** -0.5)
            att_t = _swa(q_t, kv, swa_sinks[j], rows)
            h = _proj_res(att_t, swa_w_o[j].astype(BF16), gain(norm_post[l, 1]), h, rows, tm, a_transposed=True)
        if l == n_a - 1:
            h = ffn(h, l, 1, rows)
            kv =_norm_matmul(h, gain(kv_norm_g), w_kv.astype(BF16), rows, tm_proj, w_kv.shape[1])
            rows = seq
            tm, tm_proj = _row_tile(rows, 512), _row_tile(rows, 1024)
        else:
            h = ffn(h, l, 1, rows)
    return h.reshape(batch, seq, d)
```

```python
import functools

import numpy as np
import jax
import jax.numpy as jnp
from jax import lax
from jax.experimental import pallas as pl
from jax.experimental.pallas import tpu as pltpu

F32 = jnp.float32
BF16 = jnp.bfloat16

RMS_EPS = 1e-6
N_META = 16
MLSTM_HEADS = 8
SWA_HEADS = 32
SWA_KV_HEADS = 4
SWA_GROUP = SWA_HEADS // SWA_KV_HEADS
WINDOW = 128

V7X_LANES = 128
BF16_SUBLANES = 16
V7X_VMEM_BYTES = 64 * 1024 * 1024
VMEM_LIMIT = V7X_VMEM_BYTES * 3 // 4

FFN_TF = 256
FFN_TM_CAP = 1024
FFN_ROW_CHUNK = 320
MLSTM_CHUNK = 256
MASK_NEG = -1e30


def _row_tile(rows, cap):
    best = None
    for t in range(BF16_SUBLANES, cap + 1, BF16_SUBLANES):
        if rows % t == 0:
            best = t
    assert best is not None, (rows, cap)
    return best


def _params(*sem, vmem=VMEM_LIMIT):
    return pltpu.CompilerParams(dimension_semantics=sem, vmem_limit_bytes=vmem)


def _rms(x, g):
    return x * lax.rsqrt(jnp.mean(x * x, axis=-1, keepdims=True) + RMS_EPS) * g


def _ffn_step(first, last, h_ref, gpre_ref, gpost_ref, wg_ref, wu_ref, wo_ref, o_ref, xn_ref, *, tf, nf, dff, rc,
              chunk_done=None):
    tm = h_ref.shape[0]
    chunks = [slice(r, r + rc) for r in range(0, tm, rc)]
    if first:
        for rows in chunks:
            xn_ref[rows, :] = _rms(h_ref[rows, :], gpre_ref[...]).astype(BF16)
    xn = xn_ref[...]
    g = jnp.dot(xn, wg_ref[...].astype(BF16), preferred_element_type=F32)
    u = jnp.dot(xn, wu_ref[...].astype(BF16), preferred_element_type=F32)
    a = g * jax.nn.sigmoid(g) * u
    wo = wo_ref[...].astype(BF16)
    if not last:
        y = jnp.dot(a.astype(BF16), wo, preferred_element_type=F32)
        if first:
            o_ref[...] = y
        else:
            o_ref[...] += y
        return
    first_new = (nf - 1) * tf - (dff - tf)
    col = lax.broadcasted_iota(jnp.int32, a.shape, 1)
    a = jnp.where(col >= first_new, a, 0.0).astype(BF16)
    for k, rows in enumerate(chunks):
        y = o_ref[rows, :] + jnp.dot(a[rows, :], wo, preferred_element_type=F32)
        o_ref[rows, :] = h_ref[rows, :] + 0.5 * _rms(y, gpost_ref[...])
        if chunk_done is not None:
            chunk_done(k, rows.start)


def _ffn_kernel(*refs, layer, half, tf, nf, dff, rc, n_blocks, n_tail):
    if n_tail:
        (h_hbm, tail_hbm, gpre_ref, gpost_ref, win_hbm, wout_hbm, o_hbm, xn_ref, acc_ref, hbuf_ref, wg_buf, wu_buf, wo_buf,
         sem_h, sem_o, sem_w, sem_t) = refs
    else:
        (h_hbm, gpre_ref, gpost_ref, win_hbm, wout_hbm, o_hbm, xn_ref, acc_ref, hbuf_ref, wg_buf, wu_buf, wo_buf,
         sem_h, sem_o, sem_w) = refs
    i = pl.program_id(0)
    tm = acc_ref.shape[0]
    assert nf >= 4 and nf % 2 == 0
    slot = i % 2
    starts = list(range(0, tm, rc))
    last_off = dff - tf

    def h_copies(block, s, is_last):
        if not (n_tail and is_last):
            return (pltpu.make_async_copy(h_hbm.at[pl.ds(pl.multiple_of(block * tm, 8), tm), :], hbuf_ref.at[s],
                                          sem_h.at[s]),)
        body = tm - n_tail
        return (pltpu.make_async_copy(h_hbm.at[pl.ds((n_blocks - 1) * tm, body), :], hbuf_ref.at[s, pl.ds(0, body), :],
                                      sem_h.at[s]),
                pltpu.make_async_copy(tail_hbm, hbuf_ref.at[s, pl.ds(body, n_tail), :], sem_t.at[0]))

    def for_h(block, s, fn):
        if not n_tail:
            for c in h_copies(block, s, False):
                fn(c)
            return
        is_last = jnp.asarray(block == n_blocks - 1)

        @pl.when(jnp.logical_not(is_last))
        def _():
            for c in h_copies(block, s, False):
                fn(c)

        @pl.when(is_last)
        def _():
            for c in h_copies(block, s, True):
                fn(c)

    def o_copy(block, k, r):
        dst = o_hbm.at[pl.ds(pl.multiple_of(block * tm + r, 8), rc), :]
        return pltpu.make_async_copy(acc_ref.at[pl.ds(r, rc), :], dst, sem_o.at[k])

    def w_copies(off, s):
        if not isinstance(off, int):
            off = pl.multiple_of(off, V7X_LANES)
        up = off + dff
        if not isinstance(up, int):
            up = pl.multiple_of(up, V7X_LANES)
        return (pltpu.make_async_copy(win_hbm.at[layer, half, :, pl.ds(off, tf)], wg_buf.at[s], sem_w.at[s, 0]),
                pltpu.make_async_copy(win_hbm.at[layer, half, :, pl.ds(up, tf)], wu_buf.at[s], sem_w.at[s, 1]),
                pltpu.make_async_copy(wout_hbm.at[layer, half, pl.ds(off, tf), :], wo_buf.at[s], sem_w.at[s, 2]))

    def start_all(copies):
        for c in copies:
            c.start()

    def wait_all(copies):
        for c in copies:
            c.wait()

    def step(first, last, s, **kw):
        _ffn_step(first, last, hbuf_ref.at[slot], gpre_ref, gpost_ref, wg_buf.at[s], wu_buf.at[s], wo_buf.at[s],
                  acc_ref, xn_ref, tf=tf, nf=nf, dff=dff, rc=rc, **kw)

    @pl.when(i == 0)
    def _():
        for_h(0, 0, lambda c: c.start())
        start_all(w_copies(0, 0))

    for_h(i, slot, lambda c: c.wait())

    @pl.when(i > 0)
    def _():
        for k, r in enumerate(starts):
            o_copy(i - 1, k, r).wait()

    wait_all(w_copies(0, 0))
    start_all(w_copies(tf, 1))
    step(True, False, 0)

    @pl.when(i + 1 < n_blocks)
    def _():
        for_h(i + 1, 1 - slot, lambda c: c.start())

    def middle(j, carry):
        s = j % 2
        wait_all(w_copies(j * tf, s))
        start_all(w_copies(jnp.minimum((j + 1) * tf, last_off), 1 - s))
        step(False, False, s)
        return carry

    lax.fori_loop(1, nf - 1, middle, 0)

    wait_all(w_copies(last_off, (nf - 1) % 2))

    @pl.when(i + 1 < n_blocks)
    def _():
        start_all(w_copies(0, 0))

    step(False, True, (nf - 1) % 2, chunk_done=lambda k, r: o_copy(i, k, r).start())

    @pl.when(i == n_blocks - 1)
    def _():
        for k, r in enumerate(starts):
            o_copy(i, k, r).wait()


def _ffn(h, g_pre, g_post, w_in, w_out, layer, half, rows, tm, tail=None):
    d = h.shape[1]
    n_tail = 0 if tail is None else tail.shape[0]
    assert n_tail % 8 == 0 and n_tail < tm and h.shape[0] >= rows - n_tail
    any_spec = pl.BlockSpec(memory_space=pl.ANY)
    dff = w_out.shape[2]
    tf = FFN_TF
    nf = -(-dff // tf)
    assert dff % V7X_LANES == 0 and tf % V7X_LANES == 0 and dff >= tf
    wbytes = w_in.dtype.itemsize
    cast_tmp = 3 * d * tf * 2 if w_in.dtype != BF16 else 0
    vmem = 3 * tm * d * 4 + tm * d * 2 + 2 * 3 * d * tf * wbytes + cast_tmp + 4 * tm * tf * 4
    vmem = min(vmem * 5 // 4, V7X_VMEM_BYTES * 7 // 8)
    rc = _row_tile(tm, FFN_ROW_CHUNK)
    n_blocks = rows // tm
    return pl.pallas_call(
        functools.partial(_ffn_kernel, layer=layer, half=half, tf=tf, nf=nf, dff=dff, rc=rc, n_blocks=n_blocks,
                          n_tail=n_tail),
        grid=(n_blocks,),
        in_specs=[any_spec] * (2 if n_tail else 1) + [
            pl.BlockSpec((1, d), lambda i: (0, 0)),
            pl.BlockSpec((1, d), lambda i: (0, 0)),
            any_spec,
            any_spec,
        ],
        out_specs=any_spec,
        out_shape=jax.ShapeDtypeStruct((rows, d), F32),
        scratch_shapes=[pltpu.VMEM((tm, d), BF16), pltpu.VMEM((tm, d), F32), pltpu.VMEM((2, tm, d), F32),
                        pltpu.VMEM((2, d, tf), w_in.dtype), pltpu.VMEM((2, d, tf), w_in.dtype),
                        pltpu.VMEM((2, tf, d), w_out.dtype),
                        pltpu.SemaphoreType.DMA((2,)), pltpu.SemaphoreType.DMA((tm // rc,)),
                        pltpu.SemaphoreType.DMA((2, 3))] + ([pltpu.SemaphoreType.DMA((1,))] if n_tail else []),
        compiler_params=_params("arbitrary", vmem=vmem),
        name="ffn",
    )(*([h, tail] if n_tail else [h]), g_pre, g_post, w_in, w_out)


def _norm_matmul_step(first, refs, *, transposed, scale, rc, gate_heads):
    if gate_heads:
        h_ref, g_ref, w_ref, wg_ref, bg_ref, o_ref, go_ref, xn_ref = refs
    else:
        h_ref, g_ref, w_ref, o_ref, xn_ref = refs
    if first:
        for r in range(0, h_ref.shape[0], rc):
            xn_ref[r:r + rc, :] = _rms(h_ref[r:r + rc, :], g_ref[...]).astype(BF16)
        if gate_heads:
            z = jnp.dot(xn_ref[...], wg_ref[...], preferred_element_type=F32) + bg_ref[...]
            log_sig = jnp.minimum(z, 0.0) - jnp.log1p(jnp.exp(-jnp.abs(z)))
            col = lax.broadcasted_iota(jnp.int32, z.shape, 1)
            go_ref[...] = jnp.where(col >= gate_heads, log_sig, z)
    if transposed:
        y = lax.dot_general(w_ref[...], xn_ref[...], (((1,), (1,)), ((), ())), preferred_element_type=F32)
    else:
        y = jnp.dot(xn_ref[...], w_ref[...], preferred_element_type=F32)
    o_ref[...] = (y if scale == 1.0 else y * scale).astype(o_ref.dtype)


def _norm_matmul_kernel(*refs, **kw):
    j = pl.program_id(1)
    pl.when(j == 0)(functools.partial(_norm_matmul_step, True, refs, **kw))
    pl.when(j > 0)(functools.partial(_norm_matmul_step, False, refs, **kw))


def _norm_matmul(h, g, w, rows, tm, tn, transposed=False, scale=1.0, gates=None):
    d = h.shape[1]
    if transposed:
        n = w.shape[0]
        w_spec = pl.BlockSpec((tn, d), lambda i, j: (j, 0))
        o_spec = pl.BlockSpec((tn, tm), lambda i, j: (j, i))
        o_shape = (n, rows)
    else:
        n = w.shape[1]
        w_spec = pl.BlockSpec((d, tn), lambda i, j: (0, j))
        o_spec = pl.BlockSpec((tm, tn), lambda i, j: (i, j))
        o_shape = (rows, n)
    in_specs = [pl.BlockSpec((tm, d), lambda i, j: (i, 0)), pl.BlockSpec((1, d), lambda i, j: (0, 0)), w_spec]
    out_specs = [o_spec]
    out_shape = [jax.ShapeDtypeStruct(o_shape, BF16)]
    operands = [h, g, w]
    if gates is not None:
        in_specs += [pl.BlockSpec((d, V7X_LANES), lambda i, j: (0, 0)), pl.BlockSpec((1, V7X_LANES), lambda i, j: (0, 0))]
        out_specs += [pl.BlockSpec((tm, V7X_LANES), lambda i, j: (i, 0))]
        out_shape += [jax.ShapeDtypeStruct((rows, V7X_LANES), F32)]
        operands += list(gates)
    res = pl.pallas_call(
        functools.partial(_norm_matmul_kernel, transposed=transposed, scale=scale, rc=_row_tile(tm, FFN_ROW_CHUNK),
                          gate_heads=MLSTM_HEADS if gates is not None else 0),
        grid=(rows // tm, n // tn),
        in_specs=in_specs,
        out_specs=out_specs,
        out_shape=out_shape,
        scratch_shapes=[pltpu.VMEM((tm, d), BF16)],
        compiler_params=_params("parallel", "arbitrary"),
        name="norm_matmul",
    )(*operands)
    return res if gates is not None else res[0]


def _mlstm_head(hd, tc, qkvo_ref, gc_ref, gr_ref, ng_ref, o_ref, c_ref, n_ref, m_ref, *, heads, dk, dv):
    qk_w = heads * dk
    v_w = heads * dv
    q = qkvo_ref[0:tc, hd * dk:(hd + 1) * dk]
    k = qkvo_ref[0:tc, qk_w + hd * dk:qk_w + (hd + 1) * dk]
    v = qkvo_ref[0:tc, 2 * qk_w + hd * dv:2 * qk_w + (hd + 1) * dv]
    og = qkvo_ref[0:tc, 2 * qk_w + v_w + hd * dv:2 * qk_w + v_w + (hd + 1) * dv]
    li_col = gc_ref[0:tc, hd:hd + 1]
    lf_col = gc_ref[0:tc, heads + hd:heads + hd + 1]
    li_row = gr_ref[hd:hd + 1, 0:tc]
    lf_row = gr_ref[heads + hd:heads + hd + 1, 0:tc]
    scale = dk ** -0.5

    t_idx = lax.broadcasted_iota(jnp.int32, (tc, tc), 0)
    s_idx = lax.broadcasted_iota(jnp.int32, (tc, tc), 1)
    causal = s_idx <= t_idx
    b_col = jnp.sum(jnp.where(causal, lf_row, 0.0), axis=1, keepdims=True)
    b_row = jnp.sum(jnp.where(t_idx <= s_idx, lf_col, 0.0), axis=0, keepdims=True)

    m_prev = m_ref[hd:hd + 1, 0:1]
    d_intra = jnp.where(causal, b_col - b_row + li_row, -jnp.inf)
    d_inter = b_col + m_prev
    m_t = jnp.maximum(d_inter, jnp.max(d_intra, axis=1, keepdims=True))
    w_intra = jnp.exp(d_intra - m_t)
    w_inter = jnp.exp(d_inter - m_t)

    s = lax.dot_general(q, k, (((1,), (1,)), ((), ())), preferred_element_type=F32) * scale * w_intra
    c_prev = c_ref[hd]
    n_prev = n_ref[hd:hd + 1, :]
    q_c = jnp.dot(q, c_prev.astype(BF16), preferred_element_type=F32) * scale
    num = w_inter * q_c + jnp.dot(s.astype(BF16), v, preferred_element_type=F32)
    q_n = jnp.sum(q.astype(F32) * n_prev, axis=1, keepdims=True) * scale
    den = w_inter * q_n + jnp.sum(s, axis=1, keepdims=True)
    hh = num / jnp.maximum(jnp.abs(den), jnp.exp(-m_t))
    hh = hh * lax.rsqrt(jnp.mean(hh * hh, axis=1, keepdims=True) + RMS_EPS)
    hh = hh * ng_ref[:, hd * dv:(hd + 1) * dv] * jax.nn.sigmoid(og.astype(F32))
    o_ref[0:tc, hd * dv:(hd + 1) * dv] = hh.astype(o_ref.dtype)

    b_last = b_col[tc - 1:tc, :]
    d_state = b_last - b_col + li_col
    m_new = jnp.maximum(b_last + m_prev, jnp.max(d_state, axis=0, keepdims=True))
    w_s = jnp.exp(d_state - m_new)
    decay = jnp.exp(b_last + m_prev - m_new)
    vw = (v.astype(F32) * w_s).astype(BF16)
    c_ref[hd] = decay * c_prev + lax.dot_general(k, vw, (((0,), (0,)), ((), ())), preferred_element_type=F32)
    n_ref[hd:hd + 1, :] = decay * n_prev + jnp.sum(k.astype(F32) * w_s, axis=0, keepdims=True)
    m_ref[hd:hd + 1, :] = jnp.broadcast_to(m_new, (1, m_ref.shape[1]))


def _mlstm_kernel(qkvo_ref, gc_ref, gr_ref, ng_ref, o_ref, c_ref, n_ref, m_ref, *, heads, dk, dv, t_meta, t_chunk):
    step = pl.program_id(0)

    def chunk(tc):
        for hd in range(heads):
            _mlstm_head(hd, tc, qkvo_ref, gc_ref, gr_ref, ng_ref, o_ref, c_ref, n_ref, m_ref,
                        heads=heads, dk=dk, dv=dv)

    @pl.when(step == 0)
    def _():
        c_ref[...] = jnp.zeros_like(c_ref)
        n_ref[...] = jnp.zeros_like(n_ref)
        m_ref[...] = jnp.zeros_like(m_ref)
        o_ref[...] = jnp.zeros_like(o_ref)
        chunk(t_meta)

    @pl.when(step > 0)
    def _():
        chunk(t_chunk)


def _mlstm_recurrence(qkvo, gates, gates_t, norm_g, rows_real):
    heads = MLSTM_HEADS
    rows = qkvo.shape[0]
    v_w = norm_g.shape[1]
    dv = v_w // heads
    dk = (qkvo.shape[1] - 2 * v_w) // (2 * heads)
    t = MLSTM_CHUNK
    nc = rows_real // t
    assert rows_real % t == 0 and rows - rows_real == N_META and gates_t.shape[1] == (nc + 1) * t
    blk = lambda c: (c + nc) % (nc + 1)
    return pl.pallas_call(
        functools.partial(_mlstm_kernel, heads=heads, dk=dk, dv=dv, t_meta=N_META, t_chunk=t),
        grid=(nc + 1,),
        in_specs=[
            pl.BlockSpec((t, qkvo.shape[1]), lambda c: (blk(c), 0)),
            pl.BlockSpec((t, V7X_LANES), lambda c: (blk(c), 0)),
            pl.BlockSpec((2 * heads, t), lambda c: (0, blk(c))),
            pl.BlockSpec((1, v_w), lambda c: (0, 0)),
        ],
        out_specs=pl.BlockSpec((t, v_w), lambda c: (blk(c), 0)),
        out_shape=jax.ShapeDtypeStruct((rows, v_w), BF16),
        scratch_shapes=[pltpu.VMEM((heads, dk, dv), F32), pltpu.VMEM((heads, dk), F32),
                        pltpu.VMEM((heads, V7X_LANES), F32)],
        compiler_params=_params("arbitrary"),
        name="mlstm_recurrence",
    )(qkvo, gates, gates_t, norm_g)


def _proj_res_kernel(a_ref, w_ref, g_ref, h_ref, o_ref, *, a_transposed):
    contract_a = 0 if a_transposed else 1
    y = lax.dot_general(a_ref[...], w_ref[...], (((contract_a,), (0,)), ((), ())), preferred_element_type=F32)
    o_ref[...] = h_ref[...] + _rms(y, g_ref[...])


def _proj_res(a, w, g, h, rows, tm, a_transposed=False):
    k, d = w.shape
    a_spec = pl.BlockSpec((k, tm), lambda i: (0, i)) if a_transposed else pl.BlockSpec((tm, k), lambda i: (i, 0))
    return pl.pallas_call(
        functools.partial(_proj_res_kernel, a_transposed=a_transposed),
        grid=(rows // tm,),
        in_specs=[
            a_spec,
            pl.BlockSpec((k, d), lambda i: (0, 0)),
            pl.BlockSpec((1, d), lambda i: (0, 0)),
            pl.BlockSpec((tm, d), lambda i: (i, 0)),
        ],
        out_specs=pl.BlockSpec((tm, d), lambda i: (i, 0)),
        out_shape=jax.ShapeDtypeStruct((rows, d), F32),
        compiler_params=_params("parallel"),
        name="proj_res",
    )(a, w, g, h)


def _swa_kernel(sink_ref, qt_ref, kp_ref, kc_ref, km_ref, vp_ref, vc_ref, vm_ref, bias_ref, o_ref, *, kvh, grp, hd, n_keys):
    w = qt_ref.shape[1]
    n_meta = km_ref.shape[0]
    sink_row_idx = 2 * w + n_meta
    zpad = jnp.zeros((n_keys - sink_row_idx, hd), BF16)
    bias = bias_ref[0]
    key_idx = lax.broadcasted_iota(jnp.int32, (n_keys - 2 * w, grp * w), 0) + 2 * w
    for h in range(kvh):
        base = h * grp * hd
        qt = jnp.concatenate([qt_ref[base + g * hd:base + (g + 1) * hd, :] for g in range(grp)], axis=1)
        cols = slice(h * hd, (h + 1) * hd)
        kk = jnp.concatenate([kp_ref[:, cols], kc_ref[:, cols], km_ref[:, cols], zpad], axis=0)
        vv = jnp.concatenate([vp_ref[:, cols], vc_ref[:, cols], vm_ref[:, cols], zpad], axis=0)
        s = jnp.dot(kk, qt, preferred_element_type=F32) + bias
        sink = jnp.concatenate([jnp.full((1, w), sink_ref[h * grp + g], F32) for g in range(grp)], axis=1)
        s = jnp.concatenate([s[:2 * w], jnp.where(key_idx == sink_row_idx, sink, s[2 * w:])], axis=0)
        m = jnp.max(s, axis=0, keepdims=True)
        p = jnp.exp(s - m)
        denom = jnp.sum(p, axis=0, keepdims=True)
        o = lax.dot_general(vv, p.astype(BF16), (((0,), (0,)), ((), ())), preferred_element_type=F32) / denom
        for g in range(grp):
            o_ref[base + g * hd:base + (g + 1) * hd, :] = o[:, g * w:(g + 1) * w].astype(o_ref.dtype)


def _swa_bias(n_keys):
    qi = np.arange(WINDOW)[None, :]
    r = np.arange(n_keys)[:, None]
    band = (r > qi) & (r <= qi + WINDOW) & (r < 2 * WINDOW)
    meta = (r >= 2 * WINDOW) & (r < 2 * WINDOW + N_META)
    later = band | meta
    first = (band & (r >= WINDOW)) | meta
    both = np.stack([first, later])
    both = np.where(both, 0.0, MASK_NEG).astype(np.float32)
    return jnp.asarray(np.tile(both, (1, 1, SWA_GROUP)))


def _swa(q_t, kv, sinks, rows):
    d = q_t.shape[0]
    hd = d // SWA_HEADS
    kv_w = SWA_KV_HEADS * hd
    nb = rows // WINDOW
    n_keys = -(-(2 * WINDOW + N_META + 1) // BF16_SUBLANES) * BF16_SUBLANES
    bias = _swa_bias(n_keys)
    meta_blk = rows // N_META
    prev = lambda n: jnp.maximum(n - 1, 0)
    return pl.pallas_call(
        functools.partial(_swa_kernel, kvh=SWA_KV_HEADS, grp=SWA_GROUP, hd=hd, n_keys=n_keys),
        grid=(nb,),
        in_specs=[
            pl.BlockSpec(memory_space=pltpu.SMEM),
            pl.BlockSpec((d, WINDOW), lambda n: (0, n)),
            pl.BlockSpec((WINDOW, kv_w), lambda n: (prev(n), 0)),
            pl.BlockSpec((WINDOW, kv_w), lambda n: (n, 0)),
            pl.BlockSpec((N_META, kv_w), lambda n: (meta_blk, 0)),
            pl.BlockSpec((WINDOW, kv_w), lambda n: (prev(n), 1)),
            pl.BlockSpec((WINDOW, kv_w), lambda n: (n, 1)),
            pl.BlockSpec((N_META, kv_w), lambda n: (meta_blk, 1)),
            pl.BlockSpec((1,) + bias.shape[1:], lambda n: (jnp.minimum(n, 1), 0, 0)),
        ],
        out_specs=pl.BlockSpec((d, WINDOW), lambda n: (0, n)),
        out_shape=jax.ShapeDtypeStruct((d, rows), BF16),
        compiler_params=_params("parallel"),
        name="swa_attention",
    )(sinks, q_t, kv, kv, kv, kv, kv, kv, bias)


def kernel(x, meta_tokens, norm_pre, norm_post, ffn_w_in, ffn_w_out, mlstm_w_in, mlstm_gate_bias, mlstm_norm_g,
           mlstm_w_out, kv_norm_g, w_kv, swa_w_q, swa_sinks, swa_w_o):
    batch, seq, d = x.shape
    assert batch == 1 and meta_tokens.shape[0] == N_META and seq % MLSTM_CHUNK == 0 and seq % WINDOW == 0
    depth = norm_pre.shape[0]
    n_a = mlstm_w_in.shape[0]
    heads = MLSTM_HEADS
    gain = lambda g: g.reshape(1, -1)
    ffn = lambda hh, l, k, rows, tail=None: _ffn(hh, gain(norm_pre[l, 2 * k]), gain(norm_post[l, 2 * k]), ffn_w_in,
                                                ffn_w_out, l, k, rows, _row_tile(rows, FFN_TM_CAP), tail)

    h = x[0]
    rows = seq + N_META
    tm, tm_proj = _row_tile(rows, 512), _row_tile(rows, 1024)
    kv = None
    for l in range(depth):
        h = ffn(h, l, 0, rows, meta_tokens.astype(x.dtype) if l == 0 else None)
        if l < n_a:
            n_main = mlstm_w_in.shape[2] - 2 * heads
            w_gates = jnp.pad(mlstm_w_in[l, :, n_main:], ((0, 0), (0, V7X_LANES - 2 * heads))).astype(BF16)
            bias = jnp.pad(mlstm_gate_bias[l], (0, V7X_LANES - 2 * heads)).reshape(1, V7X_LANES)
            g_pre = gain(norm_pre[l, 1])
            qkvo, gates = _norm_matmul(h, g_pre, mlstm_w_in[l, :, :n_main].astype(BF16), rows, tm_proj, 1024,
                                       gates=(w_gates, bias))
            n_chunks = seq // MLSTM_CHUNK + 1
            gates_t = jnp.pad(gates[:, :2 * heads].T, ((0, 0), (0, n_chunks * MLSTM_CHUNK - rows)))
            mixed = _mlstm_recurrence(qkvo, gates, gates_t, gain(mlstm_norm_g[l]), seq)
            h = _proj_res(mixed, mlstm_w_out[l].astype(BF16), gain(norm_post[l, 1]), h, rows, tm)
        else:
            j = l - n_a
            hd = d // SWA_HEADS
            q_t = _norm_matmul(h, gain(norm_pre[l, 1]), swa_w_q[j].T.astype(BF16), rows, tm_proj, 1024,
                               transposed=True, scale=hd ** -0.5)
            att_t = _swa(q_t, kv, swa_sinks[j], rows)
            h = _proj_res(att_t, swa_w_o[j].astype(BF16), gain(norm_post[l, 1]), h, rows, tm, a_transposed=True)
        if l == n_a - 1:
            h = ffn(h, l, 1, rows)
            kv =_norm_matmul(h, gain(kv_norm_g), w_kv.astype(BF16), rows, tm_proj, w_kv.shape[1])
            rows = seq
            tm, tm_proj = _row_tile(rows, 512), _row_tile(rows, 1024)
        else:
            h = ffn(h, l, 1, rows)
    return h.reshape(batch, seq, d)
```

```python
import functools

import numpy as np
import jax
import jax.numpy as jnp
from jax import lax
from jax.experimental import pallas as pl
from jax.experimental.pallas import tpu as pltpu

F32 = jnp.float32
BF16 = jnp.bfloat16

RMS_EPS = 1e-6
N_META = 16
MLSTM_HEADS = 8
SWA_HEADS = 32
SWA_KV_HEADS = 4
SWA_GROUP = SWA_HEADS // SWA_KV_HEADS
WINDOW = 128

V7X_LANES = 128
BF16_SUBLANES = 16
V7X_VMEM_BYTES = 64 * 1024 * 1024
VMEM_LIMIT = V7X_VMEM_BYTES * 3 // 4

FFN_TF = 256
FFN_TM_CAP = 1024
FFN_ROW_CHUNK = 320
MLSTM_CHUNK = 256
MLSTM_CHUNKS_PER_STEP = 2
MASK_NEG = -1e30


def _row_tile(rows, cap):
    best = None
    for t in range(BF16_SUBLANES, cap + 1, BF16_SUBLANES):
        if rows % t == 0:
            best = t
    assert best is not None, (rows, cap)
    return best


def _params(*sem, vmem=VMEM_LIMIT):
    return pltpu.CompilerParams(dimension_semantics=sem, vmem_limit_bytes=vmem)


def _rms(x, g):
    return x * lax.rsqrt(jnp.mean(x * x, axis=-1, keepdims=True) + RMS_EPS) * g


def _ffn_step(first, last, h_ref, gpre_ref, gpost_ref, wg_ref, wu_ref, wo_ref, o_ref, xn_ref, *, tf, nf, dff, rc,
              chunk_done=None):
    tm = h_ref.shape[0]
    chunks = [slice(r, r + rc) for r in range(0, tm, rc)]
    if first:
        for rows in chunks:
            xn_ref[rows, :] = _rms(h_ref[rows, :], gpre_ref[...]).astype(BF16)
    xn = xn_ref[...]
    g = jnp.dot(xn, wg_ref[...].astype(BF16), preferred_element_type=F32)
    u = jnp.dot(xn, wu_ref[...].astype(BF16), preferred_element_type=F32)
    a = g * jax.nn.sigmoid(g) * u
    wo = wo_ref[...].astype(BF16)
    if not last:
        y = jnp.dot(a.astype(BF16), wo, preferred_element_type=F32)
        if first:
            o_ref[...] = y
        else:
            o_ref[...] += y
        return
    first_new = (nf - 1) * tf - (dff - tf)
    col = lax.broadcasted_iota(jnp.int32, a.shape, 1)
    a = jnp.where(col >= first_new, a, 0.0).astype(BF16)
    for k, rows in enumerate(chunks):
        y = o_ref[rows, :] + jnp.dot(a[rows, :], wo, preferred_element_type=F32)
        o_ref[rows, :] = h_ref[rows, :] + 0.5 * _rms(y, gpost_ref[...])
        if chunk_done is not None:
            chunk_done(k, rows.start)


def _ffn_kernel(*refs, layer, half, tf, nf, dff, rc, n_blocks, n_tail):
    if n_tail:
        (h_hbm, tail_hbm, gpre_ref, gpost_ref, win_hbm, wout_hbm, o_hbm, xn_ref, acc_ref, hbuf_ref, wg_buf, wu_buf, wo_buf,
         sem_h, sem_o, sem_w, sem_t) = refs
    else:
        (h_hbm, gpre_ref, gpost_ref, win_hbm, wout_hbm, o_hbm, xn_ref, acc_ref, hbuf_ref, wg_buf, wu_buf, wo_buf,
         sem_h, sem_o, sem_w) = refs
    i = pl.program_id(0)
    tm = acc_ref.shape[0]
    assert nf >= 4 and nf % 2 == 0
    slot = i % 2
    starts = list(range(0, tm, rc))
    last_off = dff - tf

    def h_copies(block, s, is_last):
        if not (n_tail and is_last):
            return (pltpu.make_async_copy(h_hbm.at[pl.ds(pl.multiple_of(block * tm, 8), tm), :], hbuf_ref.at[s],
                                          sem_h.at[s]),)
        body = tm - n_tail
        return (pltpu.make_async_copy(h_hbm.at[pl.ds((n_blocks - 1) * tm, body), :], hbuf_ref.at[s, pl.ds(0, body), :],
                                      sem_h.at[s]),
                pltpu.make_async_copy(tail_hbm, hbuf_ref.at[s, pl.ds(body, n_tail), :], sem_t.at[0]))

    def for_h(block, s, fn):
        if not n_tail:
            for c in h_copies(block, s, False):
                fn(c)
            return
        is_last = jnp.asarray(block == n_blocks - 1)

        @pl.when(jnp.logical_not(is_last))
        def _():
            for c in h_copies(block, s, False):
                fn(c)

        @pl.when(is_last)
        def _():
            for c in h_copies(block, s, True):
                fn(c)

    def o_copy(block, k, r):
        dst = o_hbm.at[pl.ds(pl.multiple_of(block * tm + r, 8), rc), :]
        return pltpu.make_async_copy(acc_ref.at[pl.ds(r, rc), :], dst, sem_o.at[k])

    def w_copies(off, s):
        if not isinstance(off, int):
            off = pl.multiple_of(off, V7X_LANES)
        up = off + dff
        if not isinstance(up, int):
            up = pl.multiple_of(up, V7X_LANES)
        return (pltpu.make_async_copy(win_hbm.at[layer, half, :, pl.ds(off, tf)], wg_buf.at[s], sem_w.at[s, 0]),
                pltpu.make_async_copy(win_hbm.at[layer, half, :, pl.ds(up, tf)], wu_buf.at[s], sem_w.at[s, 1]),
                pltpu.make_async_copy(wout_hbm.at[layer, half, pl.ds(off, tf), :], wo_buf.at[s], sem_w.at[s, 2]))

    def start_all(copies):
        for c in copies:
            c.start()

    def wait_all(copies):
        for c in copies:
            c.wait()

    def step(first, last, s, **kw):
        _ffn_step(first, last, hbuf_ref.at[slot], gpre_ref, gpost_ref, wg_buf.at[s], wu_buf.at[s], wo_buf.at[s],
                  acc_ref, xn_ref, tf=tf, nf=nf, dff=dff, rc=rc, **kw)

    @pl.when(i == 0)
    def _():
        for_h(0, 0, lambda c: c.start())
        start_all(w_copies(0, 0))

    for_h(i, slot, lambda c: c.wait())

    @pl.when(i > 0)
    def _():
        for k, r in enumerate(starts):
            o_copy(i - 1, k, r).wait()

    wait_all(w_copies(0, 0))
    start_all(w_copies(tf, 1))
    step(True, False, 0)

    @pl.when(i + 1 < n_blocks)
    def _():
        for_h(i + 1, 1 - slot, lambda c: c.start())

    def middle(j, carry):
        s = j % 2
        wait_all(w_copies(j * tf, s))
        start_all(w_copies(jnp.minimum((j + 1) * tf, last_off), 1 - s))
        step(False, False, s)
        return carry

    lax.fori_loop(1, nf - 1, middle, 0)

    wait_all(w_copies(last_off, (nf - 1) % 2))

    @pl.when(i + 1 < n_blocks)
    def _():
        start_all(w_copies(0, 0))

    step(False, True, (nf - 1) % 2, chunk_done=lambda k, r: o_copy(i, k, r).start())

    @pl.when(i == n_blocks - 1)
    def _():
        for k, r in enumerate(starts):
            o_copy(i, k, r).wait()


def _ffn(h, g_pre, g_post, w_in, w_out, layer, half, rows, tm, tail=None):
    d = h.shape[1]
    n_tail = 0 if tail is None else tail.shape[0]
    assert n_tail % 8 == 0 and n_tail < tm and h.shape[0] >= rows - n_tail
    any_spec = pl.BlockSpec(memory_space=pl.ANY)
    dff = w_out.shape[2]
    tf = FFN_TF
    nf = -(-dff // tf)
    assert dff % V7X_LANES == 0 and tf % V7X_LANES == 0 and dff >= tf
    wbytes = w_in.dtype.itemsize
    cast_tmp = 3 * d * tf * 2 if w_in.dtype != BF16 else 0
    vmem = 3 * tm * d * 4 + tm * d * 2 + 2 * 3 * d * tf * wbytes + cast_tmp + 4 * tm * tf * 4
    vmem = min(vmem * 5 // 4, V7X_VMEM_BYTES * 7 // 8)
    rc = _row_tile(tm, FFN_ROW_CHUNK)
    n_blocks = rows // tm
    return pl.pallas_call(
        functools.partial(_ffn_kernel, layer=layer, half=half, tf=tf, nf=nf, dff=dff, rc=rc, n_blocks=n_blocks,
                          n_tail=n_tail),
        grid=(n_blocks,),
        in_specs=[any_spec] * (2 if n_tail else 1) + [
            pl.BlockSpec((1, d), lambda i: (0, 0)),
            pl.BlockSpec((1, d), lambda i: (0, 0)),
            any_spec,
            any_spec,
        ],
        out_specs=any_spec,
        out_shape=jax.ShapeDtypeStruct((rows, d), F32),
        scratch_shapes=[pltpu.VMEM((tm, d), BF16), pltpu.VMEM((tm, d), F32), pltpu.VMEM((2, tm, d), F32),
                        pltpu.VMEM((2, d, tf), w_in.dtype), pltpu.VMEM((2, d, tf), w_in.dtype),
                        pltpu.VMEM((2, tf, d), w_out.dtype),
                        pltpu.SemaphoreType.DMA((2,)), pltpu.SemaphoreType.DMA((tm // rc,)),
                        pltpu.SemaphoreType.DMA((2, 3))] + ([pltpu.SemaphoreType.DMA((1,))] if n_tail else []),
        compiler_params=_params("arbitrary", vmem=vmem),
        name="ffn",
    )(*([h, tail] if n_tail else [h]), g_pre, g_post, w_in, w_out)


def _norm_matmul_step(first, refs, *, transposed, scale, rc, gate_heads):
    if gate_heads:
        h_ref, g_ref, w_ref, wg_ref, bg_ref, o_ref, go_ref, xn_ref = refs
    else:
        h_ref, g_ref, w_ref, o_ref, xn_ref = refs
    if first:
        for r in range(0, h_ref.shape[0], rc):
            xn_ref[r:r + rc, :] = _rms(h_ref[r:r + rc, :], g_ref[...]).astype(BF16)
        if gate_heads:
            z = jnp.dot(xn_ref[...], wg_ref[...], preferred_element_type=F32) + bg_ref[...]
            log_sig = jnp.minimum(z, 0.0) - jnp.log1p(jnp.exp(-jnp.abs(z)))
            col = lax.broadcasted_iota(jnp.int32, z.shape, 1)
            go_ref[...] = jnp.where(col >= gate_heads, log_sig, z)
    if transposed:
        y = lax.dot_general(w_ref[...], xn_ref[...], (((1,), (1,)), ((), ())), preferred_element_type=F32)
    else:
        y = jnp.dot(xn_ref[...], w_ref[...], preferred_element_type=F32)
    o_ref[...] = (y if scale == 1.0 else y * scale).astype(o_ref.dtype)


def _norm_matmul_kernel(*refs, **kw):
    j = pl.program_id(1)
    pl.when(j == 0)(functools.partial(_norm_matmul_step, True, refs, **kw))
    pl.when(j > 0)(functools.partial(_norm_matmul_step, False, refs, **kw))


def _norm_matmul(h, g, w, rows, tm, tn, transposed=False, scale=1.0, gates=None):
    d = h.shape[1]
    if transposed:
        n = w.shape[0]
        w_spec = pl.BlockSpec((tn, d), lambda i, j: (j, 0))
        o_spec = pl.BlockSpec((tn, tm), lambda i, j: (j, i))
        o_shape = (n, rows)
    else:
        n = w.shape[1]
        w_spec = pl.BlockSpec((d, tn), lambda i, j: (0, j))
        o_spec = pl.BlockSpec((tm, tn), lambda i, j: (i, j))
        o_shape = (rows, n)
    in_specs = [pl.BlockSpec((tm, d), lambda i, j: (i, 0)), pl.BlockSpec((1, d), lambda i, j: (0, 0)), w_spec]
    out_specs = [o_spec]
    out_shape = [jax.ShapeDtypeStruct(o_shape, BF16)]
    operands = [h, g, w]
    if gates is not None:
        in_specs += [pl.BlockSpec((d, V7X_LANES), lambda i, j: (0, 0)), pl.BlockSpec((1, V7X_LANES), lambda i, j: (0, 0))]
        out_specs += [pl.BlockSpec((tm, V7X_LANES), lambda i, j: (i, 0))]
        out_shape += [jax.ShapeDtypeStruct((rows, V7X_LANES), F32)]
        operands += list(gates)
    res = pl.pallas_call(
        functools.partial(_norm_matmul_kernel, transposed=transposed, scale=scale, rc=_row_tile(tm, FFN_ROW_CHUNK),
                          gate_heads=MLSTM_HEADS if gates is not None else 0),
        grid=(rows // tm, n // tn),
        in_specs=in_specs,
        out_specs=out_specs,
        out_shape=out_shape,
        scratch_shapes=[pltpu.VMEM((tm, d), BF16)],
        compiler_params=_params("parallel", "arbitrary"),
        name="norm_matmul",
    )(*operands)
    return res if gates is not None else res[0]


def _mlstm_head(hd, tc, r0, qkvo_ref, gc_ref, gr_ref, ng_ref, o_ref, c_ref, n_ref, m_ref, *, heads, dk, dv):
    qk_w = heads * dk
    v_w = heads * dv
    q = qkvo_ref[r0:r0 + tc, hd * dk:(hd + 1) * dk]
    k = qkvo_ref[r0:r0 + tc, qk_w + hd * dk:qk_w + (hd + 1) * dk]
    v = qkvo_ref[r0:r0 + tc, 2 * qk_w + hd * dv:2 * qk_w + (hd + 1) * dv]
    og = qkvo_ref[r0:r0 + tc, 2 * qk_w + v_w + hd * dv:2 * qk_w + v_w + (hd + 1) * dv]
    li_col = gc_ref[r0:r0 + tc, hd:hd + 1]
    lf_col = gc_ref[r0:r0 + tc, heads + hd:heads + hd + 1]
    li_row = gr_ref[hd:hd + 1, r0:r0 + tc]
    lf_row = gr_ref[heads + hd:heads + hd + 1, r0:r0 + tc]
    scale = dk ** -0.5

    t_idx = lax.broadcasted_iota(jnp.int32, (tc, tc), 0)
    s_idx = lax.broadcasted_iota(jnp.int32, (tc, tc), 1)
    causal = s_idx <= t_idx
    b_col = jnp.sum(jnp.where(causal, lf_row, 0.0), axis=1, keepdims=True)
    b_row = jnp.sum(jnp.where(t_idx <= s_idx, lf_col, 0.0), axis=0, keepdims=True)

    m_prev = m_ref[hd:hd + 1, 0:1]
    d_intra = jnp.where(causal, b_col - b_row + li_row, -jnp.inf)
    d_inter = b_col + m_prev
    m_t = jnp.maximum(d_inter, jnp.max(d_intra, axis=1, keepdims=True))
    w_intra = jnp.exp(d_intra - m_t)
    w_inter = jnp.exp(d_inter - m_t)

    s = lax.dot_general(q, k, (((1,), (1,)), ((), ())), preferred_element_type=F32) * scale * w_intra
    c_prev = c_ref[hd]
    n_prev = n_ref[hd:hd + 1, :]
    q_c = jnp.dot(q, c_prev.astype(BF16), preferred_element_type=F32) * scale
    num = w_inter * q_c + jnp.dot(s.astype(BF16), v, preferred_element_type=F32)
    q_n = jnp.sum(q.astype(F32) * n_prev, axis=1, keepdims=True) * scale
    den = w_inter * q_n + jnp.sum(s, axis=1, keepdims=True)
    hh = num / jnp.maximum(jnp.abs(den), jnp.exp(-m_t))
    hh = hh * lax.rsqrt(jnp.mean(hh * hh, axis=1, keepdims=True) + RMS_EPS)
    hh = hh * ng_ref[:, hd * dv:(hd + 1) * dv] * jax.nn.sigmoid(og.astype(F32))
    o_ref[r0:r0 + tc, hd * dv:(hd + 1) * dv] = hh.astype(o_ref.dtype)

    b_last = b_col[tc - 1:tc, :]
    d_state = b_last - b_col + li_col
    m_new = jnp.maximum(b_last + m_prev, jnp.max(d_state, axis=0, keepdims=True))
    w_s = jnp.exp(d_state - m_new)
    decay = jnp.exp(b_last + m_prev - m_new)
    vw = (v.astype(F32) * w_s).astype(BF16)
    c_ref[hd] = decay * c_prev + lax.dot_general(k, vw, (((0,), (0,)), ((), ())), preferred_element_type=F32)
    n_ref[hd:hd + 1, :] = decay * n_prev + jnp.sum(k.astype(F32) * w_s, axis=0, keepdims=True)
    m_ref[hd:hd + 1, :] = jnp.broadcast_to(m_new, (1, m_ref.shape[1]))


def _mlstm_kernel(qkvo_ref, gc_ref, gr_ref, ng_ref, o_ref, c_ref, n_ref, m_ref, *, heads, dk, dv, t_meta, t_chunk):
    step = pl.program_id(0)

    def chunk(tc, r0):
        for hd in range(heads):
            _mlstm_head(hd, tc, r0, qkvo_ref, gc_ref, gr_ref, ng_ref, o_ref, c_ref, n_ref, m_ref,
                        heads=heads, dk=dk, dv=dv)

    @pl.when(step == 0)
    def _():
        c_ref[...] = jnp.zeros_like(c_ref)
        n_ref[...] = jnp.zeros_like(n_ref)
        m_ref[...] = jnp.zeros_like(m_ref)
        o_ref[...] = jnp.zeros_like(o_ref)
        chunk(t_meta, 0)

    @pl.when(step > 0)
    def _():
        for r0 in range(0, qkvo_ref.shape[0], t_chunk):
            chunk(t_chunk, r0)


def _mlstm_recurrence(qkvo, gates, gates_t, norm_g, rows_real):
    heads = MLSTM_HEADS
    rows = qkvo.shape[0]
    v_w = norm_g.shape[1]
    dv = v_w // heads
    dk = (qkvo.shape[1] - 2 * v_w) // (2 * heads)
    t = MLSTM_CHUNK * MLSTM_CHUNKS_PER_STEP
    nc = rows_real // t
    assert rows_real % t == 0 and rows - rows_real == N_META and gates_t.shape[1] == (nc + 1) * t
    blk = lambda c: (c + nc) % (nc + 1)
    return pl.pallas_call(
        functools.partial(_mlstm_kernel, heads=heads, dk=dk, dv=dv, t_meta=N_META, t_chunk=MLSTM_CHUNK),
        grid=(nc + 1,),
        in_specs=[
            pl.BlockSpec((t, qkvo.shape[1]), lambda c: (blk(c), 0)),
            pl.BlockSpec((t, V7X_LANES), lambda c: (blk(c), 0)),
            pl.BlockSpec((2 * heads, t), lambda c: (0, blk(c))),
            pl.BlockSpec((1, v_w), lambda c: (0, 0)),
        ],
        out_specs=pl.BlockSpec((t, v_w), lambda c: (blk(c), 0)),
        out_shape=jax.ShapeDtypeStruct((rows, v_w), BF16),
        scratch_shapes=[pltpu.VMEM((heads, dk, dv), F32), pltpu.VMEM((heads, dk), F32),
                        pltpu.VMEM((heads, V7X_LANES), F32)],
        compiler_params=_params("arbitrary"),
        name="mlstm_recurrence",
    )(qkvo, gates, gates_t, norm_g)


def _proj_res_kernel(a_ref, w_ref, g_ref, h_ref, o_ref, *, a_transposed):
    contract_a = 0 if a_transposed else 1
    y = lax.dot_general(a_ref[...], w_ref[...], (((contract_a,), (0,)), ((), ())), preferred_element_type=F32)
    o_ref[...] = h_ref[...] + _rms(y, g_ref[...])


def _proj_res(a, w, g, h, rows, tm, a_transposed=False):
    k, d = w.shape
    a_spec = pl.BlockSpec((k, tm), lambda i: (0, i)) if a_transposed else pl.BlockSpec((tm, k), lambda i: (i, 0))
    return pl.pallas_call(
        functools.partial(_proj_res_kernel, a_transposed=a_transposed),
        grid=(rows // tm,),
        in_specs=[
            a_spec,
            pl.BlockSpec((k, d), lambda i: (0, 0)),
            pl.BlockSpec((1, d), lambda i: (0, 0)),
            pl.BlockSpec((tm, d), lambda i: (i, 0)),
        ],
        out_specs=pl.BlockSpec((tm, d), lambda i: (i, 0)),
        out_shape=jax.ShapeDtypeStruct((rows, d), F32),
        compiler_params=_params("parallel"),
        name="proj_res",
    )(a, w, g, h)


def _swa_kernel(sink_ref, qt_ref, kp_ref, kc_ref, km_ref, vp_ref, vc_ref, vm_ref, bias_ref, o_ref, *, kvh, grp, hd, n_keys):
    w = qt_ref.shape[1]
    n_meta = km_ref.shape[0]
    sink_row_idx = 2 * w + n_meta
    zpad = jnp.zeros((n_keys - sink_row_idx, hd), BF16)
    bias = bias_ref[0]
    key_idx = lax.broadcasted_iota(jnp.int32, (n_keys - 2 * w, grp * w), 0) + 2 * w
    for h in range(kvh):
        base = h * grp * hd
        qt = jnp.concatenate([qt_ref[base + g * hd:base + (g + 1) * hd, :] for g in range(grp)], axis=1)
        cols = slice(h * hd, (h + 1) * hd)
        kk = jnp.concatenate([kp_ref[:, cols], kc_ref[:, cols], km_ref[:, cols], zpad], axis=0)
        vv = jnp.concatenate([vp_ref[:, cols], vc_ref[:, cols], vm_ref[:, cols], zpad], axis=0)
        s = jnp.dot(kk, qt, preferred_element_type=F32) + bias
        sink = jnp.concatenate([jnp.full((1, w), sink_ref[h * grp + g], F32) for g in range(grp)], axis=1)
        s = jnp.concatenate([s[:2 * w], jnp.where(key_idx == sink_row_idx, sink, s[2 * w:])], axis=0)
        m = jnp.max(s, axis=0, keepdims=True)
        p = jnp.exp(s - m)
        denom = jnp.sum(p, axis=0, keepdims=True)
        o = lax.dot_general(vv, p.astype(BF16), (((0,), (0,)), ((), ())), preferred_element_type=F32) / denom
        for g in range(grp):
            o_ref[base + g * hd:base + (g + 1) * hd, :] = o[:, g * w:(g + 1) * w].astype(o_ref.dtype)


def _swa_bias(n_keys):
    qi = np.arange(WINDOW)[None, :]
    r = np.arange(n_keys)[:, None]
    band = (r > qi) & (r <= qi + WINDOW) & (r < 2 * WINDOW)
    meta = (r >= 2 * WINDOW) & (r < 2 * WINDOW + N_META)
    later = band | meta
    first = (band & (r >= WINDOW)) | meta
    both = np.stack([first, later])
    both = np.where(both, 0.0, MASK_NEG).astype(np.float32)
    return jnp.asarray(np.tile(both, (1, 1, SWA_GROUP)))


def _swa(q_t, kv, sinks, rows):
    d = q_t.shape[0]
    hd = d // SWA_HEADS
    kv_w = SWA_KV_HEADS * hd
    nb = rows // WINDOW
    n_keys = -(-(2 * WINDOW + N_META + 1) // BF16_SUBLANES) * BF16_SUBLANES
    bias = _swa_bias(n_keys)
    meta_blk = rows // N_META
    prev = lambda n: jnp.maximum(n - 1, 0)
    return pl.pallas_call(
        functools.partial(_swa_kernel, kvh=SWA_KV_HEADS, grp=SWA_GROUP, hd=hd, n_keys=n_keys),
        grid=(nb,),
        in_specs=[
            pl.BlockSpec(memory_space=pltpu.SMEM),
            pl.BlockSpec((d, WINDOW), lambda n: (0, n)),
            pl.BlockSpec((WINDOW, kv_w), lambda n: (prev(n), 0)),
            pl.BlockSpec((WINDOW, kv_w), lambda n: (n, 0)),
            pl.BlockSpec((N_META, kv_w), lambda n: (meta_blk, 0)),
            pl.BlockSpec((WINDOW, kv_w), lambda n: (prev(n), 1)),
            pl.BlockSpec((WINDOW, kv_w), lambda n: (n, 1)),
            pl.BlockSpec((N_META, kv_w), lambda n: (meta_blk, 1)),
            pl.BlockSpec((1,) + bias.shape[1:], lambda n: (jnp.minimum(n, 1), 0, 0)),
        ],
        out_specs=pl.BlockSpec((d, WINDOW), lambda n: (0, n)),
        out_shape=jax.ShapeDtypeStruct((d, rows), BF16),
        compiler_params=_params("parallel"),
        name="swa_attention",
    )(sinks, q_t, kv, kv, kv, kv, kv, kv, bias)


def kernel(x, meta_tokens, norm_pre, norm_post, ffn_w_in, ffn_w_out, mlstm_w_in, mlstm_gate_bias, mlstm_norm_g,
           mlstm_w_out, kv_norm_g, w_kv, swa_w_q, swa_sinks, swa_w_o):
    batch, seq, d = x.shape
    assert batch == 1 and meta_tokens.shape[0] == N_META and seq % MLSTM_CHUNK == 0 and seq % WINDOW == 0
    depth = norm_pre.shape[0]
    n_a = mlstm_w_in.shape[0]
    heads = MLSTM_HEADS
    gain = lambda g: g.reshape(1, -1)
    ffn = lambda hh, l, k, rows, tail=None: _ffn(hh, gain(norm_pre[l, 2 * k]), gain(norm_post[l, 2 * k]), ffn_w_in,
                                                ffn_w_out, l, k, rows, _row_tile(rows, FFN_TM_CAP), tail)

    h = x[0]
    rows = seq + N_META
    tm, tm_proj = _row_tile(rows, 512), _row_tile(rows, 1024)
    kv = None
    for l in range(depth):
        h = ffn(h, l, 0, rows, meta_tokens.astype(x.dtype) if l == 0 else None)
        if l < n_a:
            n_main = mlstm_w_in.shape[2] - 2 * heads
            w_gates = jnp.pad(mlstm_w_in[l, :, n_main:], ((0, 0), (0, V7X_LANES - 2 * heads))).astype(BF16)
            bias = jnp.pad(mlstm_gate_bias[l], (0, V7X_LANES - 2 * heads)).reshape(1, V7X_LANES)
            g_pre = gain(norm_pre[l, 1])
            qkvo, gates = _norm_matmul(h, g_pre, mlstm_w_in[l, :, :n_main].astype(BF16), rows, tm_proj, 1024,
                                       gates=(w_gates, bias))
            step_rows = MLSTM_CHUNK * MLSTM_CHUNKS_PER_STEP
            gates_t = jnp.pad(gates[:, :2 * heads].T, ((0, 0), (0, (seq // step_rows + 1) * step_rows - rows)))
            mixed = _mlstm_recurrence(qkvo, gates, gates_t, gain(mlstm_norm_g[l]), seq)
            h = _proj_res(mixed, mlstm_w_out[l].astype(BF16), gain(norm_post[l, 1]), h, rows, tm)
        else:
            j = l - n_a
            hd = d // SWA_HEADS
            q_t = _norm_matmul(h, gain(norm_pre[l, 1]), swa_w_q[j].T.astype(BF16), rows, tm_proj, 1024,
                               transposed=True, scale=hd ** -0.5)
            att_t = _swa(q_t, kv, swa_sinks[j], rows)
            h = _proj_res(att_t, swa_w_o[j].astype(BF16), gain(norm_post[l, 1]), h, rows, tm, a_transposed=True)
        if l == n_a - 1:
            h = ffn(h, l, 1, rows)
            kv =_norm_matmul(h, gain(kv_norm_g), w_kv.astype(BF16), rows, tm_proj, w_kv.shape[1])
            rows = seq
            tm, tm_proj = _row_tile(rows, 512), _row_tile(rows, 1024)
        else:
            h = ffn(h, l, 1, rows)
    return h.reshape(batch, seq, d)
```

```python
import functools

import numpy as np
import jax
import jax.numpy as jnp
from jax import lax
from jax.experimental import pallas as pl
from jax.experimental.pallas import tpu as pltpu

F32 = jnp.float32
BF16 = jnp.bfloat16

RMS_EPS = 1e-6
N_META = 16
MLSTM_HEADS = 8
SWA_HEADS = 32
SWA_KV_HEADS = 4
SWA_GROUP = SWA_HEADS // SWA_KV_HEADS
WINDOW = 128

V7X_LANES = 128
BF16_SUBLANES = 16
V7X_VMEM_BYTES = 64 * 1024 * 1024
VMEM_LIMIT = V7X_VMEM_BYTES * 3 // 4

FFN_TF = 256
FFN_TM_CAP = 1024
FFN_ROW_CHUNK = 320
MLSTM_CHUNK = 256
MLSTM_CHUNKS_PER_STEP = 2
MASK_NEG = -1e30


def _row_tile(rows, cap):
    best = None
    for t in range(BF16_SUBLANES, cap + 1, BF16_SUBLANES):
        if rows % t == 0:
            best = t
    assert best is not None, (rows, cap)
    return best


def _params(*sem, vmem=VMEM_LIMIT):
    return pltpu.CompilerParams(dimension_semantics=sem, vmem_limit_bytes=vmem)


def _rms(x, g):
    return x * lax.rsqrt(jnp.mean(x * x, axis=-1, keepdims=True) + RMS_EPS) * g


def _ffn_step(first, last, h_ref, gpre_ref, gpost_ref, wg_ref, wu_ref, wo_ref, o_ref, xn_ref, *, tf, nf, dff, rc,
              chunk_done=None, before_acc=None):
    tm = h_ref.shape[0]
    chunks = [slice(r, r + rc) for r in range(0, tm, rc)]
    if first:
        for rows in chunks:
            xn_ref[rows, :] = _rms(h_ref[rows, :], gpre_ref[...]).astype(BF16)
    xn = xn_ref[...]
    g = jnp.dot(xn, wg_ref[...].astype(BF16), preferred_element_type=F32)
    u = jnp.dot(xn, wu_ref[...].astype(BF16), preferred_element_type=F32)
    a = g * jax.nn.sigmoid(g) * u
    wo = wo_ref[...].astype(BF16)
    if not last:
        a = a.astype(BF16)
        if first and before_acc is not None:
            before_acc()
        y = jnp.dot(a, wo, preferred_element_type=F32)
        if first:
            o_ref[...] = y
        else:
            o_ref[...] += y
        return
    first_new = (nf - 1) * tf - (dff - tf)
    col = lax.broadcasted_iota(jnp.int32, a.shape, 1)
    a = jnp.where(col >= first_new, a, 0.0).astype(BF16)
    for k, rows in enumerate(chunks):
        y = o_ref[rows, :] + jnp.dot(a[rows, :], wo, preferred_element_type=F32)
        o_ref[rows, :] = h_ref[rows, :] + 0.5 * _rms(y, gpost_ref[...])
        if chunk_done is not None:
            chunk_done(k, rows.start)


def _ffn_kernel(*refs, layer, half, tf, nf, dff, rc, n_blocks, n_tail):
    if n_tail:
        (h_hbm, tail_hbm, gpre_ref, gpost_ref, win_hbm, wout_hbm, o_hbm, xn_ref, acc_ref, hbuf_ref, wg_buf, wu_buf, wo_buf,
         sem_h, sem_o, sem_w, sem_t) = refs
    else:
        (h_hbm, gpre_ref, gpost_ref, win_hbm, wout_hbm, o_hbm, xn_ref, acc_ref, hbuf_ref, wg_buf, wu_buf, wo_buf,
         sem_h, sem_o, sem_w) = refs
    i = pl.program_id(0)
    tm = acc_ref.shape[0]
    assert nf >= 4 and nf % 2 == 0
    slot = i % 2
    starts = list(range(0, tm, rc))
    last_off = dff - tf

    def h_copies(block, s, is_last):
        if not (n_tail and is_last):
            return (pltpu.make_async_copy(h_hbm.at[pl.ds(pl.multiple_of(block * tm, 8), tm), :], hbuf_ref.at[s],
                                          sem_h.at[s]),)
        body = tm - n_tail
        return (pltpu.make_async_copy(h_hbm.at[pl.ds((n_blocks - 1) * tm, body), :], hbuf_ref.at[s, pl.ds(0, body), :],
                                      sem_h.at[s]),
                pltpu.make_async_copy(tail_hbm, hbuf_ref.at[s, pl.ds(body, n_tail), :], sem_t.at[0]))

    def for_h(block, s, fn):
        if not n_tail:
            for c in h_copies(block, s, False):
                fn(c)
            return
        is_last = jnp.asarray(block == n_blocks - 1)

        @pl.when(jnp.logical_not(is_last))
        def _():
            for c in h_copies(block, s, False):
                fn(c)

        @pl.when(is_last)
        def _():
            for c in h_copies(block, s, True):
                fn(c)

    def o_copy(block, k, r):
        dst = o_hbm.at[pl.ds(pl.multiple_of(block * tm + r, 8), rc), :]
        return pltpu.make_async_copy(acc_ref.at[pl.ds(r, rc), :], dst, sem_o.at[k])

    def w_copies(off, s):
        if not isinstance(off, int):
            off = pl.multiple_of(off, V7X_LANES)
        up = off + dff
        if not isinstance(up, int):
            up = pl.multiple_of(up, V7X_LANES)
        return (pltpu.make_async_copy(win_hbm.at[layer, half, :, pl.ds(off, tf)], wg_buf.at[s], sem_w.at[s, 0]),
                pltpu.make_async_copy(win_hbm.at[layer, half, :, pl.ds(up, tf)], wu_buf.at[s], sem_w.at[s, 1]),
                pltpu.make_async_copy(wout_hbm.at[layer, half, pl.ds(off, tf), :], wo_buf.at[s], sem_w.at[s, 2]))

    def start_all(copies):
        for c in copies:
            c.start()

    def wait_all(copies):
        for c in copies:
            c.wait()

    def step(first, last, s, **kw):
        _ffn_step(first, last, hbuf_ref.at[slot], gpre_ref, gpost_ref, wg_buf.at[s], wu_buf.at[s], wo_buf.at[s],
                  acc_ref, xn_ref, tf=tf, nf=nf, dff=dff, rc=rc, **kw)

    @pl.when(i == 0)
    def _():
        for_h(0, 0, lambda c: c.start())
        start_all(w_copies(0, 0))

    for_h(i, slot, lambda c: c.wait())

    def wait_prev_out():
        @pl.when(i > 0)
        def _():
            for k, r in enumerate(starts):
                o_copy(i - 1, k, r).wait()

    wait_all(w_copies(0, 0))
    start_all(w_copies(tf, 1))
    step(True, False, 0, before_acc=wait_prev_out)

    @pl.when(i + 1 < n_blocks)
    def _():
        for_h(i + 1, 1 - slot, lambda c: c.start())

    def middle(j, carry):
        s = j % 2
        wait_all(w_copies(j * tf, s))
        start_all(w_copies(jnp.minimum((j + 1) * tf, last_off), 1 - s))
        step(False, False, s)
        return carry

    lax.fori_loop(1, nf - 1, middle, 0)

    wait_all(w_copies(last_off, (nf - 1) % 2))

    @pl.when(i + 1 < n_blocks)
    def _():
        start_all(w_copies(0, 0))

    step(False, True, (nf - 1) % 2, chunk_done=lambda k, r: o_copy(i, k, r).start())

    @pl.when(i == n_blocks - 1)
    def _():
        for k, r in enumerate(starts):
            o_copy(i, k, r).wait()


def _ffn(h, g_pre, g_post, w_in, w_out, layer, half, rows, tm, tail=None):
    d = h.shape[1]
    n_tail = 0 if tail is None else tail.shape[0]
    assert n_tail % 8 == 0 and n_tail < tm and h.shape[0] >= rows - n_tail
    any_spec = pl.BlockSpec(memory_space=pl.ANY)
    dff = w_out.shape[2]
    tf = FFN_TF
    nf = -(-dff // tf)
    assert dff % V7X_LANES == 0 and tf % V7X_LANES == 0 and dff >= tf
    wbytes = w_in.dtype.itemsize
    cast_tmp = 3 * d * tf * 2 if w_in.dtype != BF16 else 0
    vmem = 3 * tm * d * 4 + tm * d * 2 + 2 * 3 * d * tf * wbytes + cast_tmp + 4 * tm * tf * 4
    vmem = min(vmem * 5 // 4, V7X_VMEM_BYTES * 7 // 8)
    rc = _row_tile(tm, FFN_ROW_CHUNK)
    n_blocks = rows // tm
    return pl.pallas_call(
        functools.partial(_ffn_kernel, layer=layer, half=half, tf=tf, nf=nf, dff=dff, rc=rc, n_blocks=n_blocks,
                          n_tail=n_tail),
        grid=(n_blocks,),
        in_specs=[any_spec] * (2 if n_tail else 1) + [
            pl.BlockSpec((1, d), lambda i: (0, 0)),
            pl.BlockSpec((1, d), lambda i: (0, 0)),
            any_spec,
            any_spec,
        ],
        out_specs=any_spec,
        out_shape=jax.ShapeDtypeStruct((rows, d), F32),
        scratch_shapes=[pltpu.VMEM((tm, d), BF16), pltpu.VMEM((tm, d), F32), pltpu.VMEM((2, tm, d), F32),
                        pltpu.VMEM((2, d, tf), w_in.dtype), pltpu.VMEM((2, d, tf), w_in.dtype),
                        pltpu.VMEM((2, tf, d), w_out.dtype),
                        pltpu.SemaphoreType.DMA((2,)), pltpu.SemaphoreType.DMA((tm // rc,)),
                        pltpu.SemaphoreType.DMA((2, 3))] + ([pltpu.SemaphoreType.DMA((1,))] if n_tail else []),
        compiler_params=_params("arbitrary", vmem=vmem),
        name="ffn",
    )(*([h, tail] if n_tail else [h]), g_pre, g_post, w_in, w_out)


def _norm_matmul_step(first, refs, *, transposed, scale, rc, gate_heads):
    if gate_heads:
        h_ref, g_ref, w_ref, wg_ref, bg_ref, o_ref, go_ref, xn_ref = refs
    else:
        h_ref, g_ref, w_ref, o_ref, xn_ref = refs
    if first:
        for r in range(0, h_ref.shape[0], rc):
            xn_ref[r:r + rc, :] = _rms(h_ref[r:r + rc, :], g_ref[...]).astype(BF16)
        if gate_heads:
            z = jnp.dot(xn_ref[...], wg_ref[...], preferred_element_type=F32) + bg_ref[...]
            log_sig = jnp.minimum(z, 0.0) - jnp.log1p(jnp.exp(-jnp.abs(z)))
            col = lax.broadcasted_iota(jnp.int32, z.shape, 1)
            go_ref[...] = jnp.where(col >= gate_heads, log_sig, z)
    if transposed:
        y = lax.dot_general(w_ref[...], xn_ref[...], (((1,), (1,)), ((), ())), preferred_element_type=F32)
    else:
        y = jnp.dot(xn_ref[...], w_ref[...], preferred_element_type=F32)
    o_ref[...] = (y if scale == 1.0 else y * scale).astype(o_ref.dtype)


def _norm_matmul_kernel(*refs, **kw):
    j = pl.program_id(1)
    pl.when(j == 0)(functools.partial(_norm_matmul_step, True, refs, **kw))
    pl.when(j > 0)(functools.partial(_norm_matmul_step, False, refs, **kw))


def _norm_matmul(h, g, w, rows, tm, tn, transposed=False, scale=1.0, gates=None):
    d = h.shape[1]
    if transposed:
        n = w.shape[0]
        w_spec = pl.BlockSpec((tn, d), lambda i, j: (j, 0))
        o_spec = pl.BlockSpec((tn, tm), lambda i, j: (j, i))
        o_shape = (n, rows)
    else:
        n = w.shape[1]
        w_spec = pl.BlockSpec((d, tn), lambda i, j: (0, j))
        o_spec = pl.BlockSpec((tm, tn), lambda i, j: (i, j))
        o_shape = (rows, n)
    in_specs = [pl.BlockSpec((tm, d), lambda i, j: (i, 0)), pl.BlockSpec((1, d), lambda i, j: (0, 0)), w_spec]
    out_specs = [o_spec]
    out_shape = [jax.ShapeDtypeStruct(o_shape, BF16)]
    operands = [h, g, w]
    if gates is not None:
        in_specs += [pl.BlockSpec((d, V7X_LANES), lambda i, j: (0, 0)), pl.BlockSpec((1, V7X_LANES), lambda i, j: (0, 0))]
        out_specs += [pl.BlockSpec((tm, V7X_LANES), lambda i, j: (i, 0))]
        out_shape += [jax.ShapeDtypeStruct((rows, V7X_LANES), F32)]
        operands += list(gates)
    res = pl.pallas_call(
        functools.partial(_norm_matmul_kernel, transposed=transposed, scale=scale, rc=_row_tile(tm, FFN_ROW_CHUNK),
                          gate_heads=MLSTM_HEADS if gates is not None else 0),
        grid=(rows // tm, n // tn),
        in_specs=in_specs,
        out_specs=out_specs,
        out_shape=out_shape,
        scratch_shapes=[pltpu.VMEM((tm, d), BF16)],
        compiler_params=_params("parallel", "arbitrary"),
        name="norm_matmul",
    )(*operands)
    return res if gates is not None else res[0]


def _mlstm_head(hd, tc, r0, qkvo_ref, gc_ref, gr_ref, ng_ref, o_ref, c_ref, n_ref, m_ref, *, heads, dk, dv):
    qk_w = heads * dk
    v_w = heads * dv
    q = qkvo_ref[r0:r0 + tc, hd * dk:(hd + 1) * dk]
    k = qkvo_ref[r0:r0 + tc, qk_w + hd * dk:qk_w + (hd + 1) * dk]
    v = qkvo_ref[r0:r0 + tc, 2 * qk_w + hd * dv:2 * qk_w + (hd + 1) * dv]
    og = qkvo_ref[r0:r0 + tc, 2 * qk_w + v_w + hd * dv:2 * qk_w + v_w + (hd + 1) * dv]
    li_col = gc_ref[r0:r0 + tc, hd:hd + 1]
    lf_col = gc_ref[r0:r0 + tc, heads + hd:heads + hd + 1]
    li_row = gr_ref[hd:hd + 1, r0:r0 + tc]
    lf_row = gr_ref[heads + hd:heads + hd + 1, r0:r0 + tc]
    scale = dk ** -0.5

    t_idx = lax.broadcasted_iota(jnp.int32, (tc, tc), 0)
    s_idx = lax.broadcasted_iota(jnp.int32, (tc, tc), 1)
    causal = s_idx <= t_idx
    b_col = jnp.sum(jnp.where(causal, lf_row, 0.0), axis=1, keepdims=True)
    b_row = jnp.sum(jnp.where(t_idx <= s_idx, lf_col, 0.0), axis=0, keepdims=True)

    m_prev = m_ref[hd:hd + 1, 0:1]
    d_intra = jnp.where(causal, b_col - b_row + li_row, -jnp.inf)
    d_inter = b_col + m_prev
    m_t = jnp.maximum(d_inter, jnp.max(d_intra, axis=1, keepdims=True))
    w_intra = jnp.exp(d_intra - m_t)
    w_inter = jnp.exp(d_inter - m_t)

    s = lax.dot_general(q, k, (((1,), (1,)), ((), ())), preferred_element_type=F32) * scale * w_intra
    c_prev = c_ref[hd]
    n_prev = n_ref[hd:hd + 1, :]
    q_c = jnp.dot(q, c_prev.astype(BF16), preferred_element_type=F32) * scale
    num = w_inter * q_c + jnp.dot(s.astype(BF16), v, preferred_element_type=F32)
    q_n = jnp.sum(q.astype(F32) * n_prev, axis=1, keepdims=True) * scale
    den = w_inter * q_n + jnp.sum(s, axis=1, keepdims=True)
    hh = num / jnp.maximum(jnp.abs(den), jnp.exp(-m_t))
    hh = hh * lax.rsqrt(jnp.mean(hh * hh, axis=1, keepdims=True) + RMS_EPS)
    hh = hh * ng_ref[:, hd * dv:(hd + 1) * dv] * jax.nn.sigmoid(og.astype(F32))
    o_ref[r0:r0 + tc, hd * dv:(hd + 1) * dv] = hh.astype(o_ref.dtype)

    b_last = b_col[tc - 1:tc, :]
    d_state = b_last - b_col + li_col
    m_new = jnp.maximum(b_last + m_prev, jnp.max(d_state, axis=0, keepdims=True))
    w_s = jnp.exp(d_state - m_new)
    decay = jnp.exp(b_last + m_prev - m_new)
    vw = (v.astype(F32) * w_s).astype(BF16)
    c_ref[hd] = decay * c_prev + lax.dot_general(k, vw, (((0,), (0,)), ((), ())), preferred_element_type=F32)
    n_ref[hd:hd + 1, :] = decay * n_prev + jnp.sum(k.astype(F32) * w_s, axis=0, keepdims=True)
    m_ref[hd:hd + 1, :] = jnp.broadcast_to(m_new, (1, m_ref.shape[1]))


def _mlstm_kernel(qkvo_ref, gc_ref, gr_ref, ng_ref, o_ref, c_ref, n_ref, m_ref, *, heads, dk, dv, t_meta, t_chunk):
    step = pl.program_id(0)

    def chunk(tc, r0):
        for hd in range(heads):
            _mlstm_head(hd, tc, r0, qkvo_ref, gc_ref, gr_ref, ng_ref, o_ref, c_ref, n_ref, m_ref,
                        heads=heads, dk=dk, dv=dv)

    @pl.when(step == 0)
    def _():
        c_ref[...] = jnp.zeros_like(c_ref)
        n_ref[...] = jnp.zeros_like(n_ref)
        m_ref[...] = jnp.zeros_like(m_ref)
        o_ref[...] = jnp.zeros_like(o_ref)
        chunk(t_meta, 0)

    @pl.when(step > 0)
    def _():
        for r0 in range(0, qkvo_ref.shape[0], t_chunk):
            chunk(t_chunk, r0)


def _mlstm_recurrence(qkvo, gates, gates_t, norm_g, rows_real):
    heads = MLSTM_HEADS
    rows = qkvo.shape[0]
    v_w = norm_g.shape[1]
    dv = v_w // heads
    dk = (qkvo.shape[1] - 2 * v_w) // (2 * heads)
    t = MLSTM_CHUNK * MLSTM_CHUNKS_PER_STEP
    nc = rows_real // t
    assert rows_real % t == 0 and rows - rows_real == N_META and gates_t.shape[1] == (nc + 1) * t
    blk = lambda c: (c + nc) % (nc + 1)
    return pl.pallas_call(
        functools.partial(_mlstm_kernel, heads=heads, dk=dk, dv=dv, t_meta=N_META, t_chunk=MLSTM_CHUNK),
        grid=(nc + 1,),
        in_specs=[
            pl.BlockSpec((t, qkvo.shape[1]), lambda c: (blk(c), 0)),
            pl.BlockSpec((t, V7X_LANES), lambda c: (blk(c), 0)),
            pl.BlockSpec((2 * heads, t), lambda c: (0, blk(c))),
            pl.BlockSpec((1, v_w), lambda c: (0, 0)),
        ],
        out_specs=pl.BlockSpec((t, v_w), lambda c: (blk(c), 0)),
        out_shape=jax.ShapeDtypeStruct((rows, v_w), BF16),
        scratch_shapes=[pltpu.VMEM((heads, dk, dv), F32), pltpu.VMEM((heads, dk), F32),
                        pltpu.VMEM((heads, V7X_LANES), F32)],
        compiler_params=_params("arbitrary"),
        name="mlstm_recurrence",
    )(qkvo, gates, gates_t, norm_g)


def _proj_res_kernel(a_ref, w_ref, g_ref, h_ref, o_ref, *, a_transposed):
    contract_a = 0 if a_transposed else 1
    y = lax.dot_general(a_ref[...], w_ref[...], (((contract_a,), (0,)), ((), ())), preferred_element_type=F32)
    o_ref[...] = h_ref[...] + _rms(y, g_ref[...])


def _proj_res(a, w, g, h, rows, tm, a_transposed=False):
    k, d = w.shape
    a_spec = pl.BlockSpec((k, tm), lambda i: (0, i)) if a_transposed else pl.BlockSpec((tm, k), lambda i: (i, 0))
    return pl.pallas_call(
        functools.partial(_proj_res_kernel, a_transposed=a_transposed),
        grid=(rows // tm,),
        in_specs=[
            a_spec,
            pl.BlockSpec((k, d), lambda i: (0, 0)),
            pl.BlockSpec((1, d), lambda i: (0, 0)),
            pl.BlockSpec((tm, d), lambda i: (i, 0)),
        ],
        out_specs=pl.BlockSpec((tm, d), lambda i: (i, 0)),
        out_shape=jax.ShapeDtypeStruct((rows, d), F32),
        compiler_params=_params("parallel"),
        name="proj_res",
    )(a, w, g, h)


def _swa_kernel(sink_ref, qt_ref, kp_ref, kc_ref, km_ref, vp_ref, vc_ref, vm_ref, bias_ref, o_ref, *, kvh, grp, hd, n_keys):
    w = qt_ref.shape[1]
    n_meta = km_ref.shape[0]
    sink_row_idx = 2 * w + n_meta
    zpad = jnp.zeros((n_keys - sink_row_idx, hd), BF16)
    bias = bias_ref[0]
    key_idx = lax.broadcasted_iota(jnp.int32, (n_keys - 2 * w, grp * w), 0) + 2 * w
    for h in range(kvh):
        base = h * grp * hd
        qt = jnp.concatenate([qt_ref[base + g * hd:base + (g + 1) * hd, :] for g in range(grp)], axis=1)
        cols = slice(h * hd, (h + 1) * hd)
        kk = jnp.concatenate([kp_ref[:, cols], kc_ref[:, cols], km_ref[:, cols], zpad], axis=0)
        vv = jnp.concatenate([vp_ref[:, cols], vc_ref[:, cols], vm_ref[:, cols], zpad], axis=0)
        s = jnp.dot(kk, qt, preferred_element_type=F32) + bias
        sink = jnp.concatenate([jnp.full((1, w), sink_ref[h * grp + g], F32) for g in range(grp)], axis=1)
        s = jnp.concatenate([s[:2 * w], jnp.where(key_idx == sink_row_idx, sink, s[2 * w:])], axis=0)
        m = jnp.max(s, axis=0, keepdims=True)
        p = jnp.exp(s - m)
        denom = jnp.sum(p, axis=0, keepdims=True)
        o = lax.dot_general(vv, p.astype(BF16), (((0,), (0,)), ((), ())), preferred_element_type=F32) / denom
        for g in range(grp):
            o_ref[base + g * hd:base + (g + 1) * hd, :] = o[:, g * w:(g + 1) * w].astype(o_ref.dtype)


def _swa_bias(n_keys):
    qi = np.arange(WINDOW)[None, :]
    r = np.arange(n_keys)[:, None]
    band = (r > qi) & (r <= qi + WINDOW) & (r < 2 * WINDOW)
    meta = (r >= 2 * WINDOW) & (r < 2 * WINDOW + N_META)
    later = band | meta
    first = (band & (r >= WINDOW)) | meta
    both = np.stack([first, later])
    both = np.where(both, 0.0, MASK_NEG).astype(np.float32)
    return jnp.asarray(np.tile(both, (1, 1, SWA_GROUP)))


def _swa(q_t, kv, sinks, rows):
    d = q_t.shape[0]
    hd = d // SWA_HEADS
    kv_w = SWA_KV_HEADS * hd
    nb = rows // WINDOW
    n_keys = -(-(2 * WINDOW + N_META + 1) // BF16_SUBLANES) * BF16_SUBLANES
    bias = _swa_bias(n_keys)
    meta_blk = rows // N_META
    prev = lambda n: jnp.maximum(n - 1, 0)
    return pl.pallas_call(
        functools.partial(_swa_kernel, kvh=SWA_KV_HEADS, grp=SWA_GROUP, hd=hd, n_keys=n_keys),
        grid=(nb,),
        in_specs=[
            pl.BlockSpec(memory_space=pltpu.SMEM),
            pl.BlockSpec((d, WINDOW), lambda n: (0, n)),
            pl.BlockSpec((WINDOW, kv_w), lambda n: (prev(n), 0)),
            pl.BlockSpec((WINDOW, kv_w), lambda n: (n, 0)),
            pl.BlockSpec((N_META, kv_w), lambda n: (meta_blk, 0)),
            pl.BlockSpec((WINDOW, kv_w), lambda n: (prev(n), 1)),
            pl.BlockSpec((WINDOW, kv_w), lambda n: (n, 1)),
            pl.BlockSpec((N_META, kv_w), lambda n: (meta_blk, 1)),
            pl.BlockSpec((1,) + bias.shape[1:], lambda n: (jnp.minimum(n, 1), 0, 0)),
        ],
        out_specs=pl.BlockSpec((d, WINDOW), lambda n: (0, n)),
        out_shape=jax.ShapeDtypeStruct((d, rows), BF16),
        compiler_params=_params("parallel"),
        name="swa_attention",
    )(sinks, q_t, kv, kv, kv, kv, kv, kv, bias)


def kernel(x, meta_tokens, norm_pre, norm_post, ffn_w_in, ffn_w_out, mlstm_w_in, mlstm_gate_bias, mlstm_norm_g,
           mlstm_w_out, kv_norm_g, w_kv, swa_w_q, swa_sinks, swa_w_o):
    batch, seq, d = x.shape
    assert batch == 1 and meta_tokens.shape[0] == N_META and seq % MLSTM_CHUNK == 0 and seq % WINDOW == 0
    depth = norm_pre.shape[0]
    n_a = mlstm_w_in.shape[0]
    heads = MLSTM_HEADS
    gain = lambda g: g.reshape(1, -1)
    ffn = lambda hh, l, k, rows, tail=None: _ffn(hh, gain(norm_pre[l, 2 * k]), gain(norm_post[l, 2 * k]), ffn_w_in,
                                                ffn_w_out, l, k, rows, _row_tile(rows, FFN_TM_CAP), tail)

    h = x[0]
    rows = seq + N_META
    tm, tm_proj = _row_tile(rows, 512), _row_tile(rows, 1024)
    kv = None
    for l in range(depth):
        h = ffn(h, l, 0, rows, meta_tokens.astype(x.dtype) if l == 0 else None)
        if l < n_a:
            n_main = mlstm_w_in.shape[2] - 2 * heads
            w_gates = jnp.pad(mlstm_w_in[l, :, n_main:], ((0, 0), (0, V7X_LANES - 2 * heads))).astype(BF16)
            bias = jnp.pad(mlstm_gate_bias[l], (0, V7X_LANES - 2 * heads)).reshape(1, V7X_LANES)
            g_pre = gain(norm_pre[l, 1])
            qkvo, gates = _norm_matmul(h, g_pre, mlstm_w_in[l, :, :n_main].astype(BF16), rows, tm_proj, 1024,
                                       gates=(w_gates, bias))
            step_rows = MLSTM_CHUNK * MLSTM_CHUNKS_PER_STEP
            gates_t = jnp.pad(gates[:, :2 * heads].T, ((0, 0), (0, (seq // step_rows + 1) * step_rows - rows)))
            mixed = _mlstm_recurrence(qkvo, gates, gates_t, gain(mlstm_norm_g[l]), seq)
            h = _proj_res(mixed, mlstm_w_out[l].astype(BF16), gain(norm_post[l, 1]), h, rows, tm)
        else:
            j = l - n_a
            hd = d // SWA_HEADS
            q_t = _norm_matmul(h, gain(norm_pre[l, 1]), swa_w_q[j].T.astype(BF16), rows, tm_proj, 1024,
                               transposed=True, scale=hd ** -0.5)
            att_t = _swa(q_t, kv, swa_sinks[j], rows)
            h = _proj_res(att_t, swa_w_o[j].astype(BF16), gain(norm_post[l, 1]), h, rows, tm, a_transposed=True)
        if l == n_a - 1:
            h = ffn(h, l, 1, rows)
            kv =_norm_matmul(h, gain(kv_norm_g), w_kv.astype(BF16), rows, tm_proj, w_kv.shape[1])
            rows = seq
            tm, tm_proj = _row_tile(rows, 512), _row_tile(rows, 1024)
        else:
            h = ffn(h, l, 1, rows)
    return h.reshape(batch, seq, d)
```

```python
import functools

import numpy as np
import jax
import jax.numpy as jnp
from jax import lax
from jax.experimental import pallas as pl
from jax.experimental.pallas import tpu as pltpu

F32 = jnp.float32
BF16 = jnp.bfloat16

RMS_EPS = 1e-6
N_META = 16
MLSTM_HEADS = 8
SWA_HEADS = 32
SWA_KV_HEADS = 4
SWA_GROUP = SWA_HEADS // SWA_KV_HEADS
WINDOW = 128

V7X_LANES = 128
BF16_SUBLANES = 16
V7X_VMEM_BYTES = 64 * 1024 * 1024
VMEM_LIMIT = V7X_VMEM_BYTES * 3 // 4

FFN_TF = 256
FFN_TM_CAP = 1024
FFN_ROW_CHUNK = 320
MLSTM_CHUNK = 256
MLSTM_CHUNKS_PER_STEP = 2
SWA_BLOCKS_PER_STEP = 2
MASK_NEG = -1e30


def _row_tile(rows, cap):
    best = None
    for t in range(BF16_SUBLANES, cap + 1, BF16_SUBLANES):
        if rows % t == 0:
            best = t
    assert best is not None, (rows, cap)
    return best


def _params(*sem, vmem=VMEM_LIMIT):
    return pltpu.CompilerParams(dimension_semantics=sem, vmem_limit_bytes=vmem)


def _rms(x, g):
    return x * lax.rsqrt(jnp.mean(x * x, axis=-1, keepdims=True) + RMS_EPS) * g


def _ffn_step(first, last, h_ref, gpre_ref, gpost_ref, wg_ref, wu_ref, wo_ref, o_ref, xn_ref, *, tf, nf, dff, rc,
              chunk_done=None, before_acc=None):
    tm = h_ref.shape[0]
    chunks = [slice(r, r + rc) for r in range(0, tm, rc)]
    if first:
        for rows in chunks:
            xn_ref[rows, :] = _rms(h_ref[rows, :], gpre_ref[...]).astype(BF16)
    xn = xn_ref[...]
    g = jnp.dot(xn, wg_ref[...].astype(BF16), preferred_element_type=F32)
    u = jnp.dot(xn, wu_ref[...].astype(BF16), preferred_element_type=F32)
    a = g * jax.nn.sigmoid(g) * u
    wo = wo_ref[...].astype(BF16)
    if not last:
        a = a.astype(BF16)
        if first and before_acc is not None:
            before_acc()
        y = jnp.dot(a, wo, preferred_element_type=F32)
        if first:
            o_ref[...] = y
        else:
            o_ref[...] += y
        return
    first_new = (nf - 1) * tf - (dff - tf)
    col = lax.broadcasted_iota(jnp.int32, a.shape, 1)
    a = jnp.where(col >= first_new, a, 0.0).astype(BF16)
    for k, rows in enumerate(chunks):
        y = o_ref[rows, :] + jnp.dot(a[rows, :], wo, preferred_element_type=F32)
        o_ref[rows, :] = h_ref[rows, :] + 0.5 * _rms(y, gpost_ref[...])
        if chunk_done is not None:
            chunk_done(k, rows.start)


def _ffn_kernel(*refs, layer, half, tf, nf, dff, rc, n_blocks, n_tail):
    if n_tail:
        (h_hbm, tail_hbm, gpre_ref, gpost_ref, win_hbm, wout_hbm, o_hbm, xn_ref, acc_ref, hbuf_ref, wg_buf, wu_buf, wo_buf,
         sem_h, sem_o, sem_w, sem_t) = refs
    else:
        (h_hbm, gpre_ref, gpost_ref, win_hbm, wout_hbm, o_hbm, xn_ref, acc_ref, hbuf_ref, wg_buf, wu_buf, wo_buf,
         sem_h, sem_o, sem_w) = refs
    i = pl.program_id(0)
    tm = acc_ref.shape[0]
    assert nf >= 4 and nf % 2 == 0
    slot = i % 2
    starts = list(range(0, tm, rc))
    last_off = dff - tf

    def h_copies(block, s, is_last):
        if not (n_tail and is_last):
            return (pltpu.make_async_copy(h_hbm.at[pl.ds(pl.multiple_of(block * tm, 8), tm), :], hbuf_ref.at[s],
                                          sem_h.at[s]),)
        body = tm - n_tail
        return (pltpu.make_async_copy(h_hbm.at[pl.ds((n_blocks - 1) * tm, body), :], hbuf_ref.at[s, pl.ds(0, body), :],
                                      sem_h.at[s]),
                pltpu.make_async_copy(tail_hbm, hbuf_ref.at[s, pl.ds(body, n_tail), :], sem_t.at[0]))

    def for_h(block, s, fn):
        if not n_tail:
            for c in h_copies(block, s, False):
                fn(c)
            return
        is_last = jnp.asarray(block == n_blocks - 1)

        @pl.when(jnp.logical_not(is_last))
        def _():
            for c in h_copies(block, s, False):
                fn(c)

        @pl.when(is_last)
        def _():
            for c in h_copies(block, s, True):
                fn(c)

    def o_copy(block, k, r):
        dst = o_hbm.at[pl.ds(pl.multiple_of(block * tm + r, 8), rc), :]
        return pltpu.make_async_copy(acc_ref.at[pl.ds(r, rc), :], dst, sem_o.at[k])

    def w_copies(off, s):
        if not isinstance(off, int):
            off = pl.multiple_of(off, V7X_LANES)
        up = off + dff
        if not isinstance(up, int):
            up = pl.multiple_of(up, V7X_LANES)
        return (pltpu.make_async_copy(win_hbm.at[layer, half, :, pl.ds(off, tf)], wg_buf.at[s], sem_w.at[s, 0]),
                pltpu.make_async_copy(win_hbm.at[layer, half, :, pl.ds(up, tf)], wu_buf.at[s], sem_w.at[s, 1]),
                pltpu.make_async_copy(wout_hbm.at[layer, half, pl.ds(off, tf), :], wo_buf.at[s], sem_w.at[s, 2]))

    def start_all(copies):
        for c in copies:
            c.start()

    def wait_all(copies):
        for c in copies:
            c.wait()

    def step(first, last, s, **kw):
        _ffn_step(first, last, hbuf_ref.at[slot], gpre_ref, gpost_ref, wg_buf.at[s], wu_buf.at[s], wo_buf.at[s],
                  acc_ref, xn_ref, tf=tf, nf=nf, dff=dff, rc=rc, **kw)

    @pl.when(i == 0)
    def _():
        for_h(0, 0, lambda c: c.start())
        start_all(w_copies(0, 0))

    for_h(i, slot, lambda c: c.wait())

    def wait_prev_out():
        @pl.when(i > 0)
        def _():
            for k, r in enumerate(starts):
                o_copy(i - 1, k, r).wait()

    wait_all(w_copies(0, 0))
    start_all(w_copies(tf, 1))
    step(True, False, 0, before_acc=wait_prev_out)

    @pl.when(i + 1 < n_blocks)
    def _():
        for_h(i + 1, 1 - slot, lambda c: c.start())

    def middle(j, carry):
        s = j % 2
        wait_all(w_copies(j * tf, s))
        start_all(w_copies(jnp.minimum((j + 1) * tf, last_off), 1 - s))
        step(False, False, s)
        return carry

    lax.fori_loop(1, nf - 1, middle, 0)

    wait_all(w_copies(last_off, (nf - 1) % 2))

    @pl.when(i + 1 < n_blocks)
    def _():
        start_all(w_copies(0, 0))

    step(False, True, (nf - 1) % 2, chunk_done=lambda k, r: o_copy(i, k, r).start())

    @pl.when(i == n_blocks - 1)
    def _():
        for k, r in enumerate(starts):
            o_copy(i, k, r).wait()


def _ffn(h, g_pre, g_post, w_in, w_out, layer, half, rows, tm, tail=None):
    d = h.shape[1]
    n_tail = 0 if tail is None else tail.shape[0]
    assert n_tail % 8 == 0 and n_tail < tm and h.shape[0] >= rows - n_tail
    any_spec = pl.BlockSpec(memory_space=pl.ANY)
    dff = w_out.shape[2]
    tf = FFN_TF
    nf = -(-dff // tf)
    assert dff % V7X_LANES == 0 and tf % V7X_LANES == 0 and dff >= tf
    wbytes = w_in.dtype.itemsize
    cast_tmp = 3 * d * tf * 2 if w_in.dtype != BF16 else 0
    vmem = 3 * tm * d * 4 + tm * d * 2 + 2 * 3 * d * tf * wbytes + cast_tmp + 4 * tm * tf * 4
    vmem = min(vmem * 5 // 4, V7X_VMEM_BYTES * 7 // 8)
    rc = _row_tile(tm, FFN_ROW_CHUNK)
    n_blocks = rows // tm
    return pl.pallas_call(
        functools.partial(_ffn_kernel, layer=layer, half=half, tf=tf, nf=nf, dff=dff, rc=rc, n_blocks=n_blocks,
                          n_tail=n_tail),
        grid=(n_blocks,),
        in_specs=[any_spec] * (2 if n_tail else 1) + [
            pl.BlockSpec((1, d), lambda i: (0, 0)),
            pl.BlockSpec((1, d), lambda i: (0, 0)),
            any_spec,
            any_spec,
        ],
        out_specs=any_spec,
        out_shape=jax.ShapeDtypeStruct((rows, d), F32),
        scratch_shapes=[pltpu.VMEM((tm, d), BF16), pltpu.VMEM((tm, d), F32), pltpu.VMEM((2, tm, d), F32),
                        pltpu.VMEM((2, d, tf), w_in.dtype), pltpu.VMEM((2, d, tf), w_in.dtype),
                        pltpu.VMEM((2, tf, d), w_out.dtype),
                        pltpu.SemaphoreType.DMA((2,)), pltpu.SemaphoreType.DMA((tm // rc,)),
                        pltpu.SemaphoreType.DMA((2, 3))] + ([pltpu.SemaphoreType.DMA((1,))] if n_tail else []),
        compiler_params=_params("arbitrary", vmem=vmem),
        name="ffn",
    )(*([h, tail] if n_tail else [h]), g_pre, g_post, w_in, w_out)


def _norm_matmul_step(first, refs, *, transposed, scale, rc, gate_heads):
    if gate_heads:
        h_ref, g_ref, w_ref, wg_ref, bg_ref, o_ref, go_ref, xn_ref = refs
    else:
        h_ref, g_ref, w_ref, o_ref, xn_ref = refs
    if first:
        for r in range(0, h_ref.shape[0], rc):
            xn_ref[r:r + rc, :] = _rms(h_ref[r:r + rc, :], g_ref[...]).astype(BF16)
        if gate_heads:
            z = jnp.dot(xn_ref[...], wg_ref[...], preferred_element_type=F32) + bg_ref[...]
            log_sig = jnp.minimum(z, 0.0) - jnp.log1p(jnp.exp(-jnp.abs(z)))
            col = lax.broadcasted_iota(jnp.int32, z.shape, 1)
            go_ref[...] = jnp.where(col >= gate_heads, log_sig, z)
    if transposed:
        y = lax.dot_general(w_ref[...], xn_ref[...], (((1,), (1,)), ((), ())), preferred_element_type=F32)
    else:
        y = jnp.dot(xn_ref[...], w_ref[...], preferred_element_type=F32)
    o_ref[...] = (y if scale == 1.0 else y * scale).astype(o_ref.dtype)


def _norm_matmul_kernel(*refs, **kw):
    j = pl.program_id(1)
    pl.when(j == 0)(functools.partial(_norm_matmul_step, True, refs, **kw))
    pl.when(j > 0)(functools.partial(_norm_matmul_step, False, refs, **kw))


def _norm_matmul(h, g, w, rows, tm, tn, transposed=False, scale=1.0, gates=None):
    d = h.shape[1]
    if transposed:
        n = w.shape[0]
        w_spec = pl.BlockSpec((tn, d), lambda i, j: (j, 0))
        o_spec = pl.BlockSpec((tn, tm), lambda i, j: (j, i))
        o_shape = (n, rows)
    else:
        n = w.shape[1]
        w_spec = pl.BlockSpec((d, tn), lambda i, j: (0, j))
        o_spec = pl.BlockSpec((tm, tn), lambda i, j: (i, j))
        o_shape = (rows, n)
    in_specs = [pl.BlockSpec((tm, d), lambda i, j: (i, 0)), pl.BlockSpec((1, d), lambda i, j: (0, 0)), w_spec]
    out_specs = [o_spec]
    out_shape = [jax.ShapeDtypeStruct(o_shape, BF16)]
    operands = [h, g, w]
    if gates is not None:
        in_specs += [pl.BlockSpec((d, V7X_LANES), lambda i, j: (0, 0)), pl.BlockSpec((1, V7X_LANES), lambda i, j: (0, 0))]
        out_specs += [pl.BlockSpec((tm, V7X_LANES), lambda i, j: (i, 0))]
        out_shape += [jax.ShapeDtypeStruct((rows, V7X_LANES), F32)]
        operands += list(gates)
    res = pl.pallas_call(
        functools.partial(_norm_matmul_kernel, transposed=transposed, scale=scale, rc=_row_tile(tm, FFN_ROW_CHUNK),
                          gate_heads=MLSTM_HEADS if gates is not None else 0),
        grid=(rows // tm, n // tn),
        in_specs=in_specs,
        out_specs=out_specs,
        out_shape=out_shape,
        scratch_shapes=[pltpu.VMEM((tm, d), BF16)],
        compiler_params=_params("parallel", "arbitrary"),
        name="norm_matmul",
    )(*operands)
    return res if gates is not None else res[0]


def _mlstm_head(hd, tc, r0, qkvo_ref, gc_ref, gr_ref, ng_ref, o_ref, c_ref, n_ref, m_ref, *, heads, dk, dv):
    qk_w = heads * dk
    v_w = heads * dv
    q = qkvo_ref[r0:r0 + tc, hd * dk:(hd + 1) * dk]
    k = qkvo_ref[r0:r0 + tc, qk_w + hd * dk:qk_w + (hd + 1) * dk]
    v = qkvo_ref[r0:r0 + tc, 2 * qk_w + hd * dv:2 * qk_w + (hd + 1) * dv]
    og = qkvo_ref[r0:r0 + tc, 2 * qk_w + v_w + hd * dv:2 * qk_w + v_w + (hd + 1) * dv]
    li_col = gc_ref[r0:r0 + tc, hd:hd + 1]
    lf_col = gc_ref[r0:r0 + tc, heads + hd:heads + hd + 1]
    li_row = gr_ref[hd:hd + 1, r0:r0 + tc]
    lf_row = gr_ref[heads + hd:heads + hd + 1, r0:r0 + tc]
    scale = dk ** -0.5

    t_idx = lax.broadcasted_iota(jnp.int32, (tc, tc), 0)
    s_idx = lax.broadcasted_iota(jnp.int32, (tc, tc), 1)
    causal = s_idx <= t_idx
    b_col = jnp.sum(jnp.where(causal, lf_row, 0.0), axis=1, keepdims=True)
    b_row = jnp.sum(jnp.where(t_idx <= s_idx, lf_col, 0.0), axis=0, keepdims=True)

    m_prev = m_ref[hd:hd + 1, 0:1]
    d_intra = jnp.where(causal, b_col - b_row + li_row, -jnp.inf)
    d_inter = b_col + m_prev
    m_t = jnp.maximum(d_inter, jnp.max(d_intra, axis=1, keepdims=True))
    w_intra = jnp.exp(d_intra - m_t)
    w_inter = jnp.exp(d_inter - m_t)

    s = lax.dot_general(q, k, (((1,), (1,)), ((), ())), preferred_element_type=F32) * scale * w_intra
    c_prev = c_ref[hd]
    n_prev = n_ref[hd:hd + 1, :]
    q_c = jnp.dot(q, c_prev.astype(BF16), preferred_element_type=F32) * scale
    num = w_inter * q_c + jnp.dot(s.astype(BF16), v, preferred_element_type=F32)
    q_n = jnp.sum(q.astype(F32) * n_prev, axis=1, keepdims=True) * scale
    den = w_inter * q_n + jnp.sum(s, axis=1, keepdims=True)
    hh = num / jnp.maximum(jnp.abs(den), jnp.exp(-m_t))
    hh = hh * lax.rsqrt(jnp.mean(hh * hh, axis=1, keepdims=True) + RMS_EPS)
    hh = hh * ng_ref[:, hd * dv:(hd + 1) * dv] * jax.nn.sigmoid(og.astype(F32))
    o_ref[r0:r0 + tc, hd * dv:(hd + 1) * dv] = hh.astype(o_ref.dtype)

    b_last = b_col[tc - 1:tc, :]
    d_state = b_last - b_col + li_col
    m_new = jnp.maximum(b_last + m_prev, jnp.max(d_state, axis=0, keepdims=True))
    w_s = jnp.exp(d_state - m_new)
    decay = jnp.exp(b_last + m_prev - m_new)
    vw = (v.astype(F32) * w_s).astype(BF16)
    c_ref[hd] = decay * c_prev + lax.dot_general(k, vw, (((0,), (0,)), ((), ())), preferred_element_type=F32)
    n_ref[hd:hd + 1, :] = decay * n_prev + jnp.sum(k.astype(F32) * w_s, axis=0, keepdims=True)
    m_ref[hd:hd + 1, :] = jnp.broadcast_to(m_new, (1, m_ref.shape[1]))


def _mlstm_kernel(qkvo_ref, gc_ref, gr_ref, ng_ref, o_ref, c_ref, n_ref, m_ref, *, heads, dk, dv, t_meta, t_chunk):
    step = pl.program_id(0)

    def chunk(tc, r0):
        for hd in range(heads):
            _mlstm_head(hd, tc, r0, qkvo_ref, gc_ref, gr_ref, ng_ref, o_ref, c_ref, n_ref, m_ref,
                        heads=heads, dk=dk, dv=dv)

    @pl.when(step == 0)
    def _():
        c_ref[...] = jnp.zeros_like(c_ref)
        n_ref[...] = jnp.zeros_like(n_ref)
        m_ref[...] = jnp.zeros_like(m_ref)
        o_ref[...] = jnp.zeros_like(o_ref)
        chunk(t_meta, 0)

    @pl.when(step > 0)
    def _():
        for r0 in range(0, qkvo_ref.shape[0], t_chunk):
            chunk(t_chunk, r0)


def _mlstm_recurrence(qkvo, gates, gates_t, norm_g, rows_real):
    heads = MLSTM_HEADS
    rows = qkvo.shape[0]
    v_w = norm_g.shape[1]
    dv = v_w // heads
    dk = (qkvo.shape[1] - 2 * v_w) // (2 * heads)
    t = MLSTM_CHUNK * MLSTM_CHUNKS_PER_STEP
    nc = rows_real // t
    assert rows_real % t == 0 and rows - rows_real == N_META and gates_t.shape[1] == (nc + 1) * t
    blk = lambda c: (c + nc) % (nc + 1)
    return pl.pallas_call(
        functools.partial(_mlstm_kernel, heads=heads, dk=dk, dv=dv, t_meta=N_META, t_chunk=MLSTM_CHUNK),
        grid=(nc + 1,),
        in_specs=[
            pl.BlockSpec((t, qkvo.shape[1]), lambda c: (blk(c), 0)),
            pl.BlockSpec((t, V7X_LANES), lambda c: (blk(c), 0)),
            pl.BlockSpec((2 * heads, t), lambda c: (0, blk(c))),
            pl.BlockSpec((1, v_w), lambda c: (0, 0)),
        ],
        out_specs=pl.BlockSpec((t, v_w), lambda c: (blk(c), 0)),
        out_shape=jax.ShapeDtypeStruct((rows, v_w), BF16),
        scratch_shapes=[pltpu.VMEM((heads, dk, dv), F32), pltpu.VMEM((heads, dk), F32),
                        pltpu.VMEM((heads, V7X_LANES), F32)],
        compiler_params=_params("arbitrary"),
        name="mlstm_recurrence",
    )(qkvo, gates, gates_t, norm_g)


def _proj_res_kernel(a_ref, w_ref, g_ref, h_ref, o_ref, *, a_transposed):
    contract_a = 0 if a_transposed else 1
    y = lax.dot_general(a_ref[...], w_ref[...], (((contract_a,), (0,)), ((), ())), preferred_element_type=F32)
    o_ref[...] = h_ref[...] + _rms(y, g_ref[...])


def _proj_res(a, w, g, h, rows, tm, a_transposed=False):
    k, d = w.shape
    a_spec = pl.BlockSpec((k, tm), lambda i: (0, i)) if a_transposed else pl.BlockSpec((tm, k), lambda i: (i, 0))
    return pl.pallas_call(
        functools.partial(_proj_res_kernel, a_transposed=a_transposed),
        grid=(rows // tm,),
        in_specs=[
            a_spec,
            pl.BlockSpec((k, d), lambda i: (0, 0)),
            pl.BlockSpec((1, d), lambda i: (0, 0)),
            pl.BlockSpec((tm, d), lambda i: (i, 0)),
        ],
        out_specs=pl.BlockSpec((tm, d), lambda i: (i, 0)),
        out_shape=jax.ShapeDtypeStruct((rows, d), F32),
        compiler_params=_params("parallel"),
        name="proj_res",
    )(a, w, g, h)


def _swa_kernel(sink_ref, qt_ref, kp_ref, kc_ref, km_ref, vp_ref, vc_ref, vm_ref, bias_ref, bias_later_ref, o_ref,
                *, kvh, grp, hd, n_keys, w):
    n_meta = km_ref.shape[0]
    sink_row_idx = 2 * w + n_meta
    zpad = jnp.zeros((n_keys - sink_row_idx, hd), BF16)
    key_idx = lax.broadcasted_iota(jnp.int32, (n_keys - 2 * w, grp * w), 0) + 2 * w
    for b in range(qt_ref.shape[1] // w):
        q_cols = slice(b * w, (b + 1) * w)
        bias = bias_ref[0] if b == 0 else bias_later_ref[0]
        for h in range(kvh):
            base = h * grp * hd
            qt = jnp.concatenate([qt_ref[base + g * hd:base + (g + 1) * hd, q_cols] for g in range(grp)], axis=1)
            cols = slice(h * hd, (h + 1) * hd)
            k_prev = kp_ref[:, cols] if b == 0 else kc_ref[(b - 1) * w:b * w, cols]
            v_prev = vp_ref[:, cols] if b == 0 else vc_ref[(b - 1) * w:b * w, cols]
            kk = jnp.concatenate([k_prev, kc_ref[q_cols, cols], km_ref[:, cols], zpad], axis=0)
            vv = jnp.concatenate([v_prev, vc_ref[q_cols, cols], vm_ref[:, cols], zpad], axis=0)
            s = jnp.dot(kk, qt, preferred_element_type=F32) + bias
            sink = jnp.concatenate([jnp.full((1, w), sink_ref[h * grp + g], F32) for g in range(grp)], axis=1)
            s = jnp.concatenate([s[:2 * w], jnp.where(key_idx == sink_row_idx, sink, s[2 * w:])], axis=0)
            m = jnp.max(s, axis=0, keepdims=True)
            p = jnp.exp(s - m)
            denom = jnp.sum(p, axis=0, keepdims=True)
            o = lax.dot_general(vv, p.astype(BF16), (((0,), (0,)), ((), ())), preferred_element_type=F32) / denom
            for g in range(grp):
                o_ref[base + g * hd:base + (g + 1) * hd, q_cols] = o[:, g * w:(g + 1) * w].astype(o_ref.dtype)


def _swa_bias(n_keys):
    qi = np.arange(WINDOW)[None, :]
    r = np.arange(n_keys)[:, None]
    band = (r > qi) & (r <= qi + WINDOW) & (r < 2 * WINDOW)
    meta = (r >= 2 * WINDOW) & (r < 2 * WINDOW + N_META)
    later = band | meta
    first = (band & (r >= WINDOW)) | meta
    both = np.stack([first, later])
    both = np.where(both, 0.0, MASK_NEG).astype(np.float32)
    return jnp.asarray(np.tile(both, (1, 1, SWA_GROUP)))


def _swa(q_t, kv, sinks, rows):
    d = q_t.shape[0]
    hd = d // SWA_HEADS
    kv_w = SWA_KV_HEADS * hd
    bq = SWA_BLOCKS_PER_STEP * WINDOW
    assert rows % bq == 0
    n_keys = -(-(2 * WINDOW + N_META + 1) // BF16_SUBLANES) * BF16_SUBLANES
    bias = _swa_bias(n_keys)
    meta_blk = rows // N_META
    prev = lambda n: jnp.maximum(n * SWA_BLOCKS_PER_STEP - 1, 0)
    bias_blk = (1,) + bias.shape[1:]
    return pl.pallas_call(
        functools.partial(_swa_kernel, kvh=SWA_KV_HEADS, grp=SWA_GROUP, hd=hd, n_keys=n_keys, w=WINDOW),
        grid=(rows // bq,),
        in_specs=[
            pl.BlockSpec(memory_space=pltpu.SMEM),
            pl.BlockSpec((d, bq), lambda n: (0, n)),
            pl.BlockSpec((WINDOW, kv_w), lambda n: (prev(n), 0)),
            pl.BlockSpec((bq, kv_w), lambda n: (n, 0)),
            pl.BlockSpec((N_META, kv_w), lambda n: (meta_blk, 0)),
            pl.BlockSpec((WINDOW, kv_w), lambda n: (prev(n), 1)),
            pl.BlockSpec((bq, kv_w), lambda n: (n, 1)),
            pl.BlockSpec((N_META, kv_w), lambda n: (meta_blk, 1)),
            pl.BlockSpec(bias_blk, lambda n: (jnp.minimum(n, 1), 0, 0)),
            pl.BlockSpec(bias_blk, lambda n: (1, 0, 0)),
        ],
        out_specs=pl.BlockSpec((d, bq), lambda n: (0, n)),
        out_shape=jax.ShapeDtypeStruct((d, rows), BF16),
        compiler_params=_params("parallel"),
        name="swa_attention",
    )(sinks, q_t, kv, kv, kv, kv, kv, kv, bias, bias)


def kernel(x, meta_tokens, norm_pre, norm_post, ffn_w_in, ffn_w_out, mlstm_w_in, mlstm_gate_bias, mlstm_norm_g,
           mlstm_w_out, kv_norm_g, w_kv, swa_w_q, swa_sinks, swa_w_o):
    batch, seq, d = x.shape
    assert batch == 1 and meta_tokens.shape[0] == N_META and seq % MLSTM_CHUNK == 0 and seq % WINDOW == 0
    depth = norm_pre.shape[0]
    n_a = mlstm_w_in.shape[0]
    heads = MLSTM_HEADS
    gain = lambda g: g.reshape(1, -1)
    ffn = lambda hh, l, k, rows, tail=None: _ffn(hh, gain(norm_pre[l, 2 * k]), gain(norm_post[l, 2 * k]), ffn_w_in,
                                                ffn_w_out, l, k, rows, _row_tile(rows, FFN_TM_CAP), tail)

    h = x[0]
    rows = seq + N_META
    tm, tm_proj = _row_tile(rows, 512), _row_tile(rows, 1024)
    kv = None
    for l in range(depth):
        h = ffn(h, l, 0, rows, meta_tokens.astype(x.dtype) if l == 0 else None)
        if l < n_a:
            n_main = mlstm_w_in.shape[2] - 2 * heads
            w_gates = jnp.pad(mlstm_w_in[l, :, n_main:], ((0, 0), (0, V7X_LANES - 2 * heads))).astype(BF16)
            bias = jnp.pad(mlstm_gate_bias[l], (0, V7X_LANES - 2 * heads)).reshape(1, V7X_LANES)
            g_pre = gain(norm_pre[l, 1])
            qkvo, gates = _norm_matmul(h, g_pre, mlstm_w_in[l, :, :n_main].astype(BF16), rows, tm_proj, 1024,
                                       gates=(w_gates, bias))
            step_rows = MLSTM_CHUNK * MLSTM_CHUNKS_PER_STEP
            gates_t = jnp.pad(gates[:, :2 * heads].T, ((0, 0), (0, (seq // step_rows + 1) * step_rows - rows)))
            mixed = _mlstm_recurrence(qkvo, gates, gates_t, gain(mlstm_norm_g[l]), seq)
            h = _proj_res(mixed, mlstm_w_out[l].astype(BF16), gain(norm_post[l, 1]), h, rows, tm)
        else:
            j = l - n_a
            hd = d // SWA_HEADS
            q_t = _norm_matmul(h, gain(norm_pre[l, 1]), swa_w_q[j].T.astype(BF16), rows, tm_proj, 1024,
                               transposed=True, scale=hd ** -0.5)
            att_t = _swa(q_t, kv, swa_sinks[j], rows)
            h = _proj_res(att_t, swa_w_o[j].astype(BF16), gain(norm_post[l, 1]), h, rows, tm, a_transposed=True)
        if l == n_a - 1:
            h = ffn(h, l, 1, rows)
            kv =_norm_matmul(h, gain(kv_norm_g), w_kv.astype(BF16), rows, tm_proj, w_kv.shape[1])
            rows = seq
            tm, tm_proj = _row_tile(rows, 512), _row_tile(rows, 1024)
        else:
            h = ffn(h, l, 1, rows)
    return h.reshape(batch, seq, d)
```
